```python
import jax, jax.numpy as jnp
from jax import lax
import numpy as np

D_MODEL = 1024
BATCH = 8
SEQ = 16384
DEPTH = 2

N_META = 16
BLOCK = 128
META_PAD = BLOCK - N_META
CONV_CH = 256
CONV_K = 31
POOL_CH = 256
POOL_WINDOWS = (2, 4, 8, 16)
POOL_GROUPS = 4
POOL_GC = POOL_CH // POOL_GROUPS
POOL_OUT = D_MODEL // POOL_GROUPS
ATT_HEADS = 4
HEAD_DIM = 128
ATT_W = ATT_HEADS * HEAD_DIM
N_BRANCH = 3
D_IN = 2 * CONV_CH + POOL_CH + 3 * ATT_W + N_BRANCH * D_MODEL
D_FF = 3 * D_MODEL
FFN_K = 3
EPS = 1e-6

kernel_name = "hybrid_conv_pool_stickbreak_gated"


def rms_norm(x, g):
    x32 = x.astype(jnp.float32)
    y = x32 * lax.rsqrt(jnp.mean(x32 * x32, axis=-1, keepdims=True) + EPS)
    return y.astype(x.dtype) * g


def layer_norm(x, g, b):
    x32 = x.astype(jnp.float32)
    mu = jnp.mean(x32, axis=-1, keepdims=True)
    xc = x32 - mu
    y = xc * lax.rsqrt(jnp.mean(xc * xc, axis=-1, keepdims=True) + EPS)
    return y.astype(x.dtype) * g + b


def causal_dwconv(x, w, b):
    k, c = w.shape
    y = lax.conv_general_dilated(
        x, w[:, None, :].astype(x.dtype), window_strides=(1,), padding=[(k - 1, 0)],
        dimension_numbers=("NWC", "WIO", "NWC"), feature_group_count=c)
    return y + b


def conv_module(u, dw_w, dw_b, ln_g, ln_b, w_out, b_out):
    a, gate = jnp.split(u, 2, axis=-1)
    h = a * jax.nn.sigmoid(gate)
    h = causal_dwconv(h, dw_w, dw_b)
    h = layer_norm(h, ln_g, ln_b)
    h = jax.nn.silu(h)
    return h @ w_out + b_out


def pool_mixer(p, w_grp, scale):
    b, l, _ = p.shape
    p32 = p.astype(jnp.float32)
    csum = jnp.cumsum(p32, axis=1)
    pos = jnp.arange(l, dtype=jnp.float32)[:, None]
    groups = []
    for g, w in enumerate(POOL_WINDOWS):
        sl = slice(g * POOL_GC, (g + 1) * POOL_GC)
        cg = csum[..., sl]
        lag = jnp.pad(cg, ((0, 0), (w, 0), (0, 0)))[:, :l]
        mean = (cg - lag) / jnp.minimum(pos + 1.0, float(w))
        groups.append(mean - p32[..., sl])
    pooled = jnp.stack(groups, axis=2).astype(p.dtype)
    y = jnp.einsum("blgc,gcd->blgd", pooled, w_grp).reshape(b, l, D_MODEL)
    return y * scale


def stick_breaking_attention(q, k, v):
    b, l, h, dh = q.shape
    pad = ((0, 0), (META_PAD, 0), (0, 0), (0, 0))
    qp = jnp.pad(q.astype(jnp.float32) * (HEAD_DIM ** -0.5), pad).transpose(0, 2, 1, 3)
    kp = jnp.pad(k.astype(jnp.float32), pad).transpose(0, 2, 1, 3)
    vp = jnp.pad(v.astype(jnp.float32), pad).transpose(0, 2, 1, 3)
    lp = l + META_PAD
    n_blocks = lp // BLOCK
    tri = jnp.asarray(np.tril(np.ones((BLOCK, BLOCK), np.float32), -1))
    outs = []
    for i in range(n_blocks):
        nk = i + 1
        kl = nk * BLOCK
        qb = qp[:, :, i * BLOCK:(i + 1) * BLOCK]
        z = jnp.einsum("bhqd,bhkd->bhqk", qb, kp[:, :, :kl])
        qpos = i * BLOCK + jnp.arange(BLOCK)
        kpos = jnp.arange(kl)
        valid = (kpos[None, :] < qpos[:, None]) & (kpos[None, :] >= META_PAD)
        log_keep = jnp.where(valid, jax.nn.log_sigmoid(-z), 0.0)
        lk = log_keep.reshape(b, h, BLOCK, nk, BLOCK)
        within = jnp.einsum("bhqnj,js->bhqns", lk, tri)
        blk_sum = jnp.sum(lk, axis=-1)
        blk_tri = jnp.asarray(np.tril(np.ones((nk, nk), np.float32), -1))
        later = jnp.einsum("bhqm,mn->bhqn", blk_sum, blk_tri)
        log_keep_after = (within + later[..., None]).reshape(b, h, BLOCK, kl)
        a = jnp.where(valid, jnp.exp(z + log_keep + log_keep_after), 0.0)
        outs.append(jnp.einsum("bhqk,bhkd->bhqd", a, vp[:, :, :kl]))
    out = jnp.concatenate(outs, axis=2)
    out = out.transpose(0, 2, 1, 3).reshape(b, lp, h * dh)[:, META_PAD:]
    return out.astype(q.dtype)


def gated_conv_mlp(x, w_up, dw_w, dw_b, w_down):
    u = x @ w_up
    u = causal_dwconv(u, dw_w, dw_b)
    gate, val = jnp.split(u, 2, axis=-1)
    return (jax.nn.gelu(gate) * val) @ w_down


def _w(key, shape, fan_in):
    return jax.random.normal(key, shape, jnp.float32) * (fan_in ** -0.5)


def _fwd_setup_inputs(seed: int = 0) -> dict:
    key = jax.random.key(seed)
    ks = jax.random.split(key, 24)
    L = DEPTH
    return {
        "x": jax.random.normal(ks[0], (BATCH, SEQ, D_MODEL), jnp.float32),
        "meta": jax.random.normal(ks[1], (N_META, D_MODEL), jnp.float32),
        "norm1": 1.0 + 0.02 * jax.random.normal(ks[2], (L, D_MODEL), jnp.float32),
        "w_in": _w(ks[3], (L, D_MODEL, D_IN), D_MODEL),
        "conv_dw_w": _w(ks[4], (L, CONV_K, CONV_CH), CONV_K),
        "conv_dw_b": 0.02 * jax.random.normal(ks[5], (L, CONV_CH), jnp.float32),
        "conv_ln_g": 1.0 + 0.02 * jax.random.normal(ks[6], (L, CONV_CH), jnp.float32),
        "conv_ln_b": 0.02 * jax.random.normal(ks[7], (L, CONV_CH), jnp.float32),
        "w_conv_out": _w(ks[8], (L, CONV_CH, D_MODEL), CONV_CH),
        "b_conv_out": 0.02 * jax.random.normal(ks[9], (L, D_MODEL), jnp.float32),
        "w_pool_grp": _w(ks[10], (L, POOL_GROUPS, POOL_GC, POOL_OUT), POOL_GC),
        "pool_scale": 1.0 + 0.02 * jax.random.normal(ks[11], (L, D_MODEL), jnp.float32),
        "w_attn_out": _w(ks[12], (L, ATT_W, D_MODEL), ATT_W),
        "w_o": _w(ks[13], (L, D_MODEL, D_MODEL), D_MODEL),
        "norm2": 1.0 + 0.02 * jax.random.normal(ks[14], (L, D_MODEL), jnp.float32),
        "w_up": _w(ks[15], (L, D_MODEL, 2 * D_FF), D_MODEL),
        "ffn_dw_w": _w(ks[16], (L, FFN_K, 2 * D_FF), FFN_K),
        "ffn_dw_b": 0.02 * jax.random.normal(ks[17], (L, 2 * D_FF), jnp.float32),
        "w_down": _w(ks[18], (L, D_FF, D_MODEL), D_FF),
        "final_norm": 1.0 + 0.02 * jax.random.normal(ks[19], (D_MODEL,), jnp.float32),
    }


def _fwd_reference(x, meta, norm1, w_in, conv_dw_w, conv_dw_b, conv_ln_g, conv_ln_b, w_conv_out, b_conv_out,
              w_pool_grp, pool_scale, w_attn_out, w_o, norm2, w_up, ffn_dw_w, ffn_dw_b, w_down, final_norm):
    b = x.shape[0]
    h = jnp.concatenate([jnp.broadcast_to(meta[None].astype(x.dtype), (b, N_META, D_MODEL)), x], axis=1)
    l = h.shape[1]
    splits = [2 * CONV_CH, 2 * CONV_CH + POOL_CH, 2 * CONV_CH + POOL_CH + ATT_W,
              2 * CONV_CH + POOL_CH + 2 * ATT_W, 2 * CONV_CH + POOL_CH + 3 * ATT_W]
    for i in range(DEPTH):
        hn = rms_norm(h, norm1[i])
        proj = hn @ w_in[i]
        u_conv, u_pool, q, k, v, gates = jnp.split(proj, splits, axis=-1)
        y_a = conv_module(u_conv, conv_dw_w[i], conv_dw_b[i], conv_ln_g[i], conv_ln_b[i],
                          w_conv_out[i], b_conv_out[i])
        y_b = pool_mixer(u_pool, w_pool_grp[i], pool_scale[i])
        att = stick_breaking_attention(q.reshape(b, l, ATT_HEADS, HEAD_DIM),
                                       k.reshape(b, l, ATT_HEADS, HEAD_DIM),
                                       v.reshape(b, l, ATT_HEADS, HEAD_DIM))
        y_c = att @ w_attn_out[i]
        g = jax.nn.sigmoid(gates).reshape(b, l, N_BRANCH, D_MODEL)
        mixed = g[:, :, 0] * y_a + g[:, :, 1] * y_b + g[:, :, 2] * y_c
        h = h + mixed @ w_o[i]
        h = h + gated_conv_mlp(rms_norm(h, norm2[i]), w_up[i], ffn_dw_w[i], ffn_dw_b[i], w_down[i])
    return rms_norm(h, final_norm)[:, N_META:]


import jax as _jax
import jax.numpy as _jnp

TWIN_FORMAT = 'train_step'
FWD_PARAMS = ['x', 'meta', 'norm1', 'w_in', 'conv_dw_w', 'conv_dw_b', 'conv_ln_g', 'conv_ln_b', 'w_conv_out', 'b_conv_out', 'w_pool_grp', 'pool_scale', 'w_attn_out', 'w_o', 'norm2', 'w_up', 'ffn_dw_w', 'ffn_dw_b', 'w_down', 'final_norm']
TWIN_WEIGHTS = ['meta', 'norm1', 'w_in', 'conv_dw_w', 'conv_dw_b', 'conv_ln_g', 'conv_ln_b', 'w_conv_out', 'b_conv_out', 'w_pool_grp', 'pool_scale', 'w_attn_out', 'w_o', 'norm2', 'w_up', 'ffn_dw_w', 'ffn_dw_b', 'w_down', 'final_norm']
TWIN_DIFF_INPUT = 'x'
TWIN_INPUTS = ['x', 'meta', 'norm1', 'w_in', 'conv_dw_w', 'conv_dw_b', 'conv_ln_g', 'conv_ln_b', 'w_conv_out', 'b_conv_out', 'w_pool_grp', 'pool_scale', 'w_attn_out', 'w_o', 'norm2', 'w_up', 'ffn_dw_w', 'ffn_dw_b', 'w_down', 'final_norm', 'loss_target', 'm_meta', 'm_norm1', 'm_w_in', 'm_conv_dw_w', 'm_conv_dw_b', 'm_conv_ln_g', 'm_conv_ln_b', 'm_w_conv_out', 'm_b_conv_out', 'm_w_pool_grp', 'm_pool_scale', 'm_w_attn_out', 'm_w_o', 'm_norm2', 'm_w_up', 'm_ffn_dw_w', 'm_ffn_dw_b', 'm_w_down', 'm_final_norm', 'v_meta', 'v_norm1', 'v_w_in', 'v_conv_dw_w', 'v_conv_dw_b', 'v_conv_ln_g', 'v_conv_ln_b', 'v_w_conv_out', 'v_b_conv_out', 'v_w_pool_grp', 'v_pool_scale', 'v_w_attn_out', 'v_w_o', 'v_norm2', 'v_w_up', 'v_ffn_dw_w', 'v_ffn_dw_b', 'v_w_down', 'v_final_norm']
TWIN_OUTPUTS = ['loss', 'grad_x', 'grad_meta', 'grad_norm1', 'grad_w_in', 'grad_conv_dw_w', 'grad_conv_dw_b', 'grad_conv_ln_g', 'grad_conv_ln_b', 'grad_w_conv_out', 'grad_b_conv_out', 'grad_w_pool_grp', 'grad_pool_scale', 'grad_w_attn_out', 'grad_w_o', 'grad_norm2', 'grad_w_up', 'grad_ffn_dw_w', 'grad_ffn_dw_b', 'grad_w_down', 'grad_final_norm', 'delta_meta', 'delta_norm1', 'delta_w_in', 'delta_conv_dw_w', 'delta_conv_dw_b', 'delta_conv_ln_g', 'delta_conv_ln_b', 'delta_w_conv_out', 'delta_b_conv_out', 'delta_w_pool_grp', 'delta_pool_scale', 'delta_w_attn_out', 'delta_w_o', 'delta_norm2', 'delta_w_up', 'delta_ffn_dw_w', 'delta_ffn_dw_b', 'delta_w_down', 'delta_final_norm', 'new_m_meta', 'new_m_norm1', 'new_m_w_in', 'new_m_conv_dw_w', 'new_m_conv_dw_b', 'new_m_conv_ln_g', 'new_m_conv_ln_b', 'new_m_w_conv_out', 'new_m_b_conv_out', 'new_m_w_pool_grp', 'new_m_pool_scale', 'new_m_w_attn_out', 'new_m_w_o', 'new_m_norm2', 'new_m_w_up', 'new_m_ffn_dw_w', 'new_m_ffn_dw_b', 'new_m_w_down', 'new_m_final_norm', 'new_v_meta', 'new_v_norm1', 'new_v_w_in', 'new_v_conv_dw_w', 'new_v_conv_dw_b', 'new_v_conv_ln_g', 'new_v_conv_ln_b', 'new_v_w_conv_out', 'new_v_b_conv_out', 'new_v_w_pool_grp', 'new_v_pool_scale', 'new_v_w_attn_out', 'new_v_w_o', 'new_v_norm2', 'new_v_w_up', 'new_v_ffn_dw_w', 'new_v_ffn_dw_b', 'new_v_w_down', 'new_v_final_norm']
TWIN_LEAF_KINDS = {'loss': 'loss', 'grad_x': 'grad_x', 'grad_meta': 'grad_w', 'grad_norm1': 'grad_w', 'grad_w_in': 'grad_w', 'grad_conv_dw_w': 'grad_w', 'grad_conv_dw_b': 'grad_w', 'grad_conv_ln_g': 'grad_w', 'grad_conv_ln_b': 'grad_w', 'grad_w_conv_out': 'grad_w', 'grad_b_conv_out': 'grad_w', 'grad_w_pool_grp': 'grad_w', 'grad_pool_scale': 'grad_w', 'grad_w_attn_out': 'grad_w', 'grad_w_o': 'grad_w', 'grad_norm2': 'grad_w', 'grad_w_up': 'grad_w', 'grad_ffn_dw_w': 'grad_w', 'grad_ffn_dw_b': 'grad_w', 'grad_w_down': 'grad_w', 'grad_final_norm': 'grad_w', 'delta_meta': 'delta_w', 'delta_norm1': 'delta_w', 'delta_w_in': 'delta_w', 'delta_conv_dw_w': 'delta_w', 'delta_conv_dw_b': 'delta_w', 'delta_conv_ln_g': 'delta_w', 'delta_conv_ln_b': 'delta_w', 'delta_w_conv_out': 'delta_w', 'delta_b_conv_out': 'delta_w', 'delta_w_pool_grp': 'delta_w', 'delta_pool_scale': 'delta_w', 'delta_w_attn_out': 'delta_w', 'delta_w_o': 'delta_w', 'delta_norm2': 'delta_w', 'delta_w_up': 'delta_w', 'delta_ffn_dw_w': 'delta_w', 'delta_ffn_dw_b': 'delta_w', 'delta_w_down': 'delta_w', 'delta_final_norm': 'delta_w', 'new_m_meta': 'new_m', 'new_m_norm1': 'new_m', 'new_m_w_in': 'new_m', 'new_m_conv_dw_w': 'new_m', 'new_m_conv_dw_b': 'new_m', 'new_m_conv_ln_g': 'new_m', 'new_m_conv_ln_b': 'new_m', 'new_m_w_conv_out': 'new_m', 'new_m_b_conv_out': 'new_m', 'new_m_w_pool_grp': 'new_m', 'new_m_pool_scale': 'new_m', 'new_m_w_attn_out': 'new_m', 'new_m_w_o': 'new_m', 'new_m_norm2': 'new_m', 'new_m_w_up': 'new_m', 'new_m_ffn_dw_w': 'new_m', 'new_m_ffn_dw_b': 'new_m', 'new_m_w_down': 'new_m', 'new_m_final_norm': 'new_m', 'new_v_meta': 'new_v', 'new_v_norm1': 'new_v', 'new_v_w_in': 'new_v', 'new_v_conv_dw_w': 'new_v', 'new_v_conv_dw_b': 'new_v', 'new_v_conv_ln_g': 'new_v', 'new_v_conv_ln_b': 'new_v', 'new_v_w_conv_out': 'new_v', 'new_v_b_conv_out': 'new_v', 'new_v_w_pool_grp': 'new_v', 'new_v_pool_scale': 'new_v', 'new_v_w_attn_out': 'new_v', 'new_v_w_o': 'new_v', 'new_v_norm2': 'new_v', 'new_v_w_up': 'new_v', 'new_v_ffn_dw_w': 'new_v', 'new_v_ffn_dw_b': 'new_v', 'new_v_w_down': 'new_v', 'new_v_final_norm': 'new_v'}


def _forward(args):
    return _fwd_reference(*[args[k] for k in FWD_PARAMS])


def _output_shape():
    def fwd():
        inp = _fwd_setup_inputs(0)
        return _fwd_reference(*[inp[k] for k in FWD_PARAMS])
    out = _jax.eval_shape(fwd)
    return out.shape, out.dtype

N_MICROBATCH = 1
ADAM_LR = 0.001
ADAM_B1 = 0.9
ADAM_B2 = 0.999
ADAM_EPS = 1e-08
ADAM_WD = 0.01
ADAM_STEP = 10
PER_EXAMPLE_BATCH_AXIS = {'x': 0, 'loss_target': 0}
SHARED_INPUTS = []
_WEIGHT_DTYPES = {'meta': _jnp.float32, 'norm1': _jnp.float32, 'w_in': _jnp.float32, 'conv_dw_w': _jnp.float32, 'conv_dw_b': _jnp.float32, 'conv_ln_g': _jnp.float32, 'conv_ln_b': _jnp.float32, 'w_conv_out': _jnp.float32, 'b_conv_out': _jnp.float32, 'w_pool_grp': _jnp.float32, 'pool_scale': _jnp.float32, 'w_attn_out': _jnp.float32, 'w_o': _jnp.float32, 'norm2': _jnp.float32, 'w_up': _jnp.float32, 'ffn_dw_w': _jnp.float32, 'ffn_dw_b': _jnp.float32, 'w_down': _jnp.float32, 'final_norm': _jnp.float32}
MOMENT_SCALE = {'meta': 1.231370e-02, 'norm1': 2.403745e-01, 'w_in': 1.063650e-01, 'conv_dw_w': 2.087759e-01, 'conv_dw_b': 4.369447e-01, 'conv_ln_g': 2.367708e-01, 'conv_ln_b': 2.369914e-01, 'w_conv_out': 1.022377e-01, 'b_conv_out': 1.744715e-01, 'w_pool_grp': 1.516644e-01, 'pool_scale': 1.532353e-01, 'w_attn_out': 1.104452e-01, 'w_o': 2.120829e-01, 'norm2': 2.437906e-01, 'w_up': 9.755573e-02, 'ffn_dw_w': 1.017764e-01, 'ffn_dw_b': 9.940037e-02, 'w_down': 1.667500e-01, 'final_norm': 1.278453e+02}


def _to_microbatches(a, axis):
    t = _jnp.moveaxis(a, axis, 0)
    t = t.reshape((N_MICROBATCH, t.shape[0] // N_MICROBATCH) + t.shape[1:])
    return _jnp.moveaxis(t, 1, axis + 1)


def setup_inputs(seed: int = 0) -> dict:
    inp = _fwd_setup_inputs(seed)
    key = _jax.random.fold_in(_jax.random.key(seed), 7919)
    shape, _ = _output_shape()
    out = dict(inp)
    out["loss_target"] = _jax.random.normal(_jax.random.fold_in(key, 0), shape, _jnp.float32)
    for i, name in enumerate(TWIN_WEIGHTS):
        w = inp[name].astype(_jnp.float32)
        if MOMENT_SCALE is None:
            s = _jnp.sqrt(_jnp.mean(_jnp.square(w)) + 1e-30)
        else:
            s = MOMENT_SCALE[name]
        km, kv = _jax.random.split(_jax.random.fold_in(key, i + 1))
        out[name] = w
        out["m_" + name] = s * _jax.random.normal(km, w.shape, _jnp.float32)
        out["v_" + name] = (s * s) * _jax.random.uniform(kv, w.shape, _jnp.float32, 0.5, 1.5)
    if N_MICROBATCH > 1:
        for name, axis in PER_EXAMPLE_BATCH_AXIS.items():
            out[name] = _to_microbatches(out[name], axis)
    return {'x': out['x'], 'meta': out['meta'], 'norm1': out['norm1'], 'w_in': out['w_in'], 'conv_dw_w': out['conv_dw_w'], 'conv_dw_b': out['conv_dw_b'], 'conv_ln_g': out['conv_ln_g'], 'conv_ln_b': out['conv_ln_b'], 'w_conv_out': out['w_conv_out'], 'b_conv_out': out['b_conv_out'], 'w_pool_grp': out['w_pool_grp'], 'pool_scale': out['pool_scale'], 'w_attn_out': out['w_attn_out'], 'w_o': out['w_o'], 'norm2': out['norm2'], 'w_up': out['w_up'], 'ffn_dw_w': out['ffn_dw_w'], 'ffn_dw_b': out['ffn_dw_b'], 'w_down': out['w_down'], 'final_norm': out['final_norm'], 'loss_target': out['loss_target'], 'm_meta': out['m_meta'], 'm_norm1': out['m_norm1'], 'm_w_in': out['m_w_in'], 'm_conv_dw_w': out['m_conv_dw_w'], 'm_conv_dw_b': out['m_conv_dw_b'], 'm_conv_ln_g': out['m_conv_ln_g'], 'm_conv_ln_b': out['m_conv_ln_b'], 'm_w_conv_out': out['m_w_conv_out'], 'm_b_conv_out': out['m_b_conv_out'], 'm_w_pool_grp': out['m_w_pool_grp'], 'm_pool_scale': out['m_pool_scale'], 'm_w_attn_out': out['m_w_attn_out'], 'm_w_o': out['m_w_o'], 'm_norm2': out['m_norm2'], 'm_w_up': out['m_w_up'], 'm_ffn_dw_w': out['m_ffn_dw_w'], 'm_ffn_dw_b': out['m_ffn_dw_b'], 'm_w_down': out['m_w_down'], 'm_final_norm': out['m_final_norm'], 'v_meta': out['v_meta'], 'v_norm1': out['v_norm1'], 'v_w_in': out['v_w_in'], 'v_conv_dw_w': out['v_conv_dw_w'], 'v_conv_dw_b': out['v_conv_dw_b'], 'v_conv_ln_g': out['v_conv_ln_g'], 'v_conv_ln_b': out['v_conv_ln_b'], 'v_w_conv_out': out['v_w_conv_out'], 'v_b_conv_out': out['v_b_conv_out'], 'v_w_pool_grp': out['v_w_pool_grp'], 'v_pool_scale': out['v_pool_scale'], 'v_w_attn_out': out['v_w_attn_out'], 'v_w_o': out['v_w_o'], 'v_norm2': out['v_norm2'], 'v_w_up': out['v_w_up'], 'v_ffn_dw_w': out['v_ffn_dw_w'], 'v_ffn_dw_b': out['v_ffn_dw_b'], 'v_w_down': out['v_w_down'], 'v_final_norm': out['v_final_norm']}


def _loss(weights, diff, rest, loss_target):
    with _jax.named_scope("forward"):
        args = {**rest, TWIN_DIFF_INPUT: diff, **{k: w.astype(_WEIGHT_DTYPES[k]) for k, w in weights.items()}}
        y = _forward(args)
    with _jax.named_scope("loss_head"):
        err = _jnp.square(y.astype(_jnp.float32) - loss_target)
        return 0.5 * _jnp.sum(_jnp.mean(err, axis=-1)) if err.ndim else 0.5 * err


def _adamw(w, g, m, v):
    m = ADAM_B1 * m + (1.0 - ADAM_B1) * g
    v = ADAM_B2 * v + (1.0 - ADAM_B2) * _jnp.square(g)
    m_hat = m / (1.0 - ADAM_B1 ** ADAM_STEP)
    v_hat = v / (1.0 - ADAM_B2 ** ADAM_STEP)
    delta = -ADAM_LR * (m_hat / (_jnp.sqrt(v_hat) + ADAM_EPS) + ADAM_WD * w)
    return delta, m, v


def reference(x, meta, norm1, w_in, conv_dw_w, conv_dw_b, conv_ln_g, conv_ln_b, w_conv_out, b_conv_out, w_pool_grp, pool_scale, w_attn_out, w_o, norm2, w_up, ffn_dw_w, ffn_dw_b, w_down, final_norm, loss_target, m_meta, m_norm1, m_w_in, m_conv_dw_w, m_conv_dw_b, m_conv_ln_g, m_conv_ln_b, m_w_conv_out, m_b_conv_out, m_w_pool_grp, m_pool_scale, m_w_attn_out, m_w_o, m_norm2, m_w_up, m_ffn_dw_w, m_ffn_dw_b, m_w_down, m_final_norm, v_meta, v_norm1, v_w_in, v_conv_dw_w, v_conv_dw_b, v_conv_ln_g, v_conv_ln_b, v_w_conv_out, v_b_conv_out, v_w_pool_grp, v_pool_scale, v_w_attn_out, v_w_o, v_norm2, v_w_up, v_ffn_dw_w, v_ffn_dw_b, v_w_down, v_final_norm):
    given = dict(x=x, meta=meta, norm1=norm1, w_in=w_in, conv_dw_w=conv_dw_w, conv_dw_b=conv_dw_b, conv_ln_g=conv_ln_g, conv_ln_b=conv_ln_b, w_conv_out=w_conv_out, b_conv_out=b_conv_out, w_pool_grp=w_pool_grp, pool_scale=pool_scale, w_attn_out=w_attn_out, w_o=w_o, norm2=norm2, w_up=w_up, ffn_dw_w=ffn_dw_w, ffn_dw_b=ffn_dw_b, w_down=w_down, final_norm=final_norm, loss_target=loss_target, m_meta=m_meta, m_norm1=m_norm1, m_w_in=m_w_in, m_conv_dw_w=m_conv_dw_w, m_conv_dw_b=m_conv_dw_b, m_conv_ln_g=m_conv_ln_g, m_conv_ln_b=m_conv_ln_b, m_w_conv_out=m_w_conv_out, m_b_conv_out=m_b_conv_out, m_w_pool_grp=m_w_pool_grp, m_pool_scale=m_pool_scale, m_w_attn_out=m_w_attn_out, m_w_o=m_w_o, m_norm2=m_norm2, m_w_up=m_w_up, m_ffn_dw_w=m_ffn_dw_w, m_ffn_dw_b=m_ffn_dw_b, m_w_down=m_w_down, m_final_norm=m_final_norm, v_meta=v_meta, v_norm1=v_norm1, v_w_in=v_w_in, v_conv_dw_w=v_conv_dw_w, v_conv_dw_b=v_conv_dw_b, v_conv_ln_g=v_conv_ln_g, v_conv_ln_b=v_conv_ln_b, v_w_conv_out=v_w_conv_out, v_b_conv_out=v_b_conv_out, v_w_pool_grp=v_w_pool_grp, v_pool_scale=v_pool_scale, v_w_attn_out=v_w_attn_out, v_w_o=v_w_o, v_norm2=v_norm2, v_w_up=v_w_up, v_ffn_dw_w=v_ffn_dw_w, v_ffn_dw_b=v_ffn_dw_b, v_w_down=v_w_down, v_final_norm=v_final_norm)
    weights = {n: given[n] for n in TWIN_WEIGHTS}
    shared = {n: given[n] for n in SHARED_INPUTS}
    per_example = {n: given[n] for n in ['x']}
    grad_fn = _jax.value_and_grad(_loss, argnums=(0, 1))

    def one_microbatch(ex, loss_target):
        ex = dict(ex)
        diff = ex.pop(TWIN_DIFF_INPUT)
        return grad_fn(weights, diff, {**shared, **ex}, loss_target)

    if N_MICROBATCH == 1:
        loss, (grad_w, grad_x) = one_microbatch(per_example, given["loss_target"])
    else:
        def body(carry, xs):
            loss_sum, grad_sum = carry
            l_k, (gw_k, gx_k) = one_microbatch(xs[0], xs[1])
            with _jax.named_scope("update"):
                return (loss_sum + l_k, _jax.tree.map(_jnp.add, grad_sum, gw_k)), gx_k

        init = (_jnp.zeros((), _jnp.float32), _jax.tree.map(_jnp.zeros_like, weights))
        (loss, grad_w), grad_x = _jax.lax.scan(body, init, (per_example, given["loss_target"]))
    with _jax.named_scope("update"):
        delta_w, new_m, new_v = {}, {}, {}
        for n in TWIN_WEIGHTS:
            delta_w[n], new_m[n], new_v[n] = _adamw(weights[n], grad_w[n], given["m_" + n], given["v_" + n])
    return (loss, grad_x, *[grad_w[n] for n in TWIN_WEIGHTS], *[delta_w[n] for n in TWIN_WEIGHTS],
            *[new_m[n] for n in TWIN_WEIGHTS], *[new_v[n] for n in TWIN_WEIGHTS])
```

```python
import functools

import jax
import jax.numpy as jnp
from jax import lax
from jax.experimental import pallas as pl
from jax.experimental.pallas import tpu as pltpu

F32 = jnp.float32
_MXU = jnp.bfloat16

D = 1024
N_META = 16
BLK = 128
PAD_ROWS = BLK - N_META
N_LAYERS = 2
CONV_CH = 256
CONV_K = 31
POOL_CH = 256
POOL_WINDOWS = (2, 4, 8, 16)
POOL_GC = 64
ATT_W = 512
NH = 4
HD = 128
D_A = 2 * CONV_CH + POOL_CH + 3 * ATT_W
D_G = 3 * D
D_FF = 3 * D
FFN_K = 3
EPS = 1e-6
Q_SCALE = HD ** -0.5

TR = 384
TB = 128
HALO = 32
HALO8 = 8
TC = 768
TF = 512
VMEM_LIMIT = 48 * 1024 * 1024

ADAM_LR = 0.001
ADAM_B1 = 0.9
ADAM_B2 = 0.999
ADAM_EPS = 1e-08
ADAM_WD = 0.01
ADAM_STEP = 10

GELU_C0 = 0.7978845608028654
GELU_C1 = 0.044715

MESH = pl.DeviceIdType.MESH
ANY = pl.BlockSpec(memory_space=pl.ANY)

BLOB_LANES = 1024
ADAM_ROWS = 128


def _pcall(body, **kw):
    return pl.pallas_call(body, **kw)


def _cp(sem):
    return pltpu.CompilerParams(dimension_semantics=sem, vmem_limit_bytes=VMEM_LIMIT)


def _dot(a, b):
    return jnp.dot(a.astype(_MXU), b.astype(_MXU), preferred_element_type=F32)


def _dot_t(a, b):
    return lax.dot_general(a.astype(_MXU), b.astype(_MXU), (((1,), (1,)), ((), ())), preferred_element_type=F32)


def _tdot(a, b):
    return lax.dot_general(a.astype(_MXU), b.astype(_MXU), (((0,), (0,)), ((), ())), preferred_element_type=F32)


def _sig(x):
    return 1.0 / (1.0 + jnp.exp(-x))


def _rms_fwd(x, g):
    r = lax.rsqrt(jnp.mean(x * x, axis=-1, keepdims=True) + EPS)
    n = x * r
    return r, n, n * g


def _rms_bwd(dy, r, n, g):
    dgain = jnp.sum(dy * n, axis=0, keepdims=True)
    dn = dy * g
    dx = r * (dn - n * jnp.mean(dn * n, axis=-1, keepdims=True))
    return dx, dgain


def _row_ids(tile_rows, i, shape):
    return i * tile_rows + lax.broadcasted_iota(jnp.int32, shape, 0)


def _rms_matmul(h, gain, w, name, split_out=False):
    R, N = h.shape[0], w.shape[1]
    nj = N // TC
    half = nj // 2

    def body(h_ref, g_ref, w_ref, o_ref, xn_ref):
        @pl.when(pl.program_id(1) == 0)
        def _():
            _, _, y = _rms_fwd(h_ref[...], g_ref[...])
            xn_ref[...] = y.astype(xn_ref.dtype)

        o_ref[...] = jnp.dot(xn_ref[...], w_ref[...], preferred_element_type=F32)

    if split_out:
        out_shape = jax.ShapeDtypeStruct((2, R, N // 2), F32)
        out_spec = pl.BlockSpec((None, TR, TC), lambda i, j: (j // half, i, j % half))
    else:
        out_shape = jax.ShapeDtypeStruct((R, N), F32)
        out_spec = pl.BlockSpec((TR, TC), lambda i, j: (i, j))
    return _pcall(
        body,
        grid=(R // TR, nj),
        in_specs=[
            pl.BlockSpec((TR, D), lambda i, j: (i, 0)),
            pl.BlockSpec((1, D), lambda i, j: (0, 0)),
            pl.BlockSpec((D, TC), lambda i, j: (0, j)),
        ],
        out_specs=out_spec,
        out_shape=out_shape,
        scratch_shapes=[pltpu.VMEM((TR, D), _MXU)],
        compiler_params=_cp(("parallel", "arbitrary")),
        name=name,
    )(h, gain, w.astype(_MXU))


def _qkv_cast(proj_a, name):
    R = proj_a.shape[0]

    def body(p_ref, o_ref):
        col = lax.broadcasted_iota(jnp.int32, (1, TC), 1) + pl.program_id(1) * TC
        sc = jnp.where(col < ATT_W, Q_SCALE, 1.0).astype(F32)
        o_ref[...] = (p_ref[...] * sc).astype(o_ref.dtype)

    return _pcall(
        body,
        grid=(R // TR, 2),
        in_specs=[pl.BlockSpec((TR, TC), lambda i, j: (i, j + 1))],
        out_specs=pl.BlockSpec((TR, TC), lambda i, j: (i, j)),
        out_shape=jax.ShapeDtypeStruct((R, 3 * ATT_W), _MXU),
        compiler_params=_cp(("parallel", "parallel")),
        name=name,
    )(proj_a)


def _attn_block(q, kb, i, j, s_carry, tri):
    rowi = lax.broadcasted_iota(jnp.int32, (BLK, BLK), 0)
    coli = lax.broadcasted_iota(jnp.int32, (BLK, BLK), 1)
    z = _dot_t(q, kb)
    qpos = i * BLK + rowi
    kpos = j * BLK + coli
    valid = jnp.logical_and(kpos < qpos, kpos >= PAD_ROWS)
    en = jnp.exp(-jnp.abs(z))
    sp = jnp.maximum(z, 0.0) + jnp.log(1.0 + en)
    lk = jnp.where(valid, -sp, 0.0)
    hi = lk.astype(_MXU)
    lo = (lk - hi.astype(F32)).astype(_MXU)
    within = jnp.dot(hi, tri, preferred_element_type=F32) + jnp.dot(lo, tri, preferred_element_type=F32)
    log_a = jnp.where(valid, z + lk + within + s_carry, -1e30)
    a = jnp.exp(log_a)
    return z, en, lk, a, valid


def _tri():
    rowi = lax.broadcasted_iota(jnp.int32, (BLK, BLK), 0)
    coli = lax.broadcasted_iota(jnp.int32, (BLK, BLK), 1)
    return (rowi > coli).astype(_MXU)


def _attn_fwd(qkv, name):
    R = qkv.shape[0]
    nb = R // BLK

    def body(q_ref, k_ref, v_ref, o_ref):
        i = pl.program_id(1)
        q = q_ref[...]
        tri = _tri()

        def step(jj, carry):
            acc, s_carry = carry
            j = i - jj
            off = pl.multiple_of(j * BLK, BLK)
            kb = k_ref[pl.ds(off, BLK), :]
            vb = v_ref[pl.ds(off, BLK), :]
            _, _, lk, a, _ = _attn_block(q, kb, i, j, s_carry, tri)
            acc = acc + _dot(a, vb)
            s_carry = s_carry + jnp.sum(lk, axis=1, keepdims=True)
            return acc, s_carry

        acc, _ = lax.fori_loop(0, i + 1, step, (jnp.zeros((BLK, HD), F32), jnp.zeros((BLK, 1), F32)))
        o_ref[...] = acc

    return _pcall(
        body,
        grid=(NH, nb),
        in_specs=[
            pl.BlockSpec((BLK, HD), lambda h, i: (i, h)),
            pl.BlockSpec((R, HD), lambda h, i: (0, NH + h)),
            pl.BlockSpec((R, HD), lambda h, i: (0, 2 * NH + h)),
        ],
        out_specs=pl.BlockSpec((BLK, HD), lambda h, i: (i, h)),
        out_shape=jax.ShapeDtypeStruct((R, ATT_W), F32),
        compiler_params=_cp(("parallel", "arbitrary")),
        name=name,
    )(qkv, qkv, qkv)


def _pool_consts(i):
    lane = lax.broadcasted_iota(jnp.int32, (1, POOL_CH), 1)
    wsize = jnp.where(lane < POOL_GC, 2.0, jnp.where(lane < 2 * POOL_GC, 4.0, jnp.where(lane < 3 * POOL_GC, 8.0, 16.0)))
    return lane, wsize


def _pool_div(rows, wsize):
    pos1 = (rows - (PAD_ROWS - 1)).astype(F32)
    return jnp.clip(pos1, 1.0, wsize)


def _lane_select(lane, s2, s4, s8, s16):
    return jnp.where(lane < POOL_GC, s2, jnp.where(lane < 2 * POOL_GC, s4, jnp.where(lane < 3 * POOL_GC, s8, s16)))


def _branch_pre(proj_a, conv_w, conv_b, name):
    R = proj_a.shape[0]
    nh = TR // HALO

    def body(t_ref, h_ref, w_ref, b_ref, c_ref, p_ref, gext, pext):
        i = pl.program_id(0)
        t = t_ref[...]
        hl = jnp.where(i == 0, 0.0, h_ref[...])
        gext[pl.ds(0, HALO), :] = hl[:, :CONV_CH] * _sig(hl[:, CONV_CH : 2 * CONV_CH])
        gext[pl.ds(HALO, TR), :] = t[:, :CONV_CH] * _sig(t[:, CONV_CH : 2 * CONV_CH])
        acc = jnp.zeros((TR, CONV_CH), F32) + b_ref[...]
        for k in range(CONV_K):
            acc = acc + w_ref[pl.ds(k, 1), :] * gext[pl.ds(HALO - (CONV_K - 1) + k, TR), :]
        c_ref[...] = acc

        p = t[:, 2 * CONV_CH :]
        pext[pl.ds(0, HALO), :] = hl[:, 2 * CONV_CH :]
        pext[pl.ds(HALO, TR), :] = p

        def back(k):
            return pext[pl.ds(HALO - k, TR), :]

        s2 = p + back(1)
        s4 = s2 + back(2) + back(3)
        s8 = s4 + back(4) + back(5) + back(6) + back(7)
        s16 = s8
        for k in range(8, 16):
            s16 = s16 + back(k)
        lane, wsize = _pool_consts(i)
        div = _pool_div(_row_ids(TR, i, (TR, POOL_CH)), wsize)
        p_ref[...] = (_lane_select(lane, s2, s4, s8, s16) / div - p).astype(p_ref.dtype)

    return _pcall(
        body,
        grid=(R // TR,),
        in_specs=[
            pl.BlockSpec((TR, TC), lambda i: (i, 0)),
            pl.BlockSpec((HALO, TC), lambda i: (jnp.maximum(i * nh - 1, 0), 0)),
            pl.BlockSpec((HALO, CONV_CH), lambda i: (0, 0)),
            pl.BlockSpec((1, CONV_CH), lambda i: (0, 0)),
        ],
        out_specs=[pl.BlockSpec((TR, CONV_CH), lambda i: (i, 0)), pl.BlockSpec((TR, POOL_CH), lambda i: (i, 0))],
        out_shape=[jax.ShapeDtypeStruct((R, CONV_CH), F32), jax.ShapeDtypeStruct((R, POOL_CH), _MXU)],
        scratch_shapes=[pltpu.VMEM((TR + HALO, CONV_CH), F32), pltpu.VMEM((TR + HALO, POOL_CH), F32)],
        compiler_params=_cp(("parallel",)),
        name=name,
    )(proj_a, proj_a, conv_w, conv_b)


def _mix_values(c, pooled, att, gates, ln_g, ln_b, w_co, b_co, w_bd, p_scale, w_ao):
    mu = jnp.mean(c, axis=-1, keepdims=True)
    xc = c - mu
    rstd = lax.rsqrt(jnp.mean(xc * xc, axis=-1, keepdims=True) + EPS)
    nl = xc * rstd
    ln = nl * ln_g + ln_b
    sg = _sig(ln)
    s = ln * sg
    ya = _dot(s, w_co) + b_co
    ybr = _dot(pooled, w_bd)
    yb = ybr * p_scale
    yc = _dot(att, w_ao)
    g = _sig(gates)
    g0, g1, g2 = g[:, :D], g[:, D : 2 * D], g[:, 2 * D :]
    mixed = g0 * ya + g1 * yb + g2 * yc
    return dict(rstd=rstd, nl=nl, ln=ln, sg=sg, s=s, ya=ya, ybr=ybr, yb=yb, yc=yc, g0=g0, g1=g1, g2=g2, mixed=mixed)


_MIX_W_SPECS = [
    pl.BlockSpec((1, CONV_CH), lambda i: (0, 0)),
    pl.BlockSpec((1, CONV_CH), lambda i: (0, 0)),
    pl.BlockSpec((CONV_CH, D), lambda i: (0, 0)),
    pl.BlockSpec((1, D), lambda i: (0, 0)),
    pl.BlockSpec((POOL_CH, D), lambda i: (0, 0)),
    pl.BlockSpec((1, D), lambda i: (0, 0)),
    pl.BlockSpec((ATT_W, D), lambda i: (0, 0)),
    pl.BlockSpec((D, D), lambda i: (0, 0)),
]


def _mix_act_specs(t):
    return [
        pl.BlockSpec((t, D_G), lambda i: (i, 0)),
        pl.BlockSpec((t, CONV_CH), lambda i: (i, 0)),
        pl.BlockSpec((t, POOL_CH), lambda i: (i, 0)),
        pl.BlockSpec((t, ATT_W), lambda i: (i, 0)),
    ]


def _mix_fwd(h, gates, c, pooled, att, mw, name):
    R = h.shape[0]

    def body(h_ref, g_ref, c_ref, p_ref, a_ref, lg, lb, wco, bco, wbd, ps, wao, wo, o_ref):
        v = _mix_values(c_ref[...], p_ref[...], a_ref[...], g_ref[...], lg[...], lb[...], wco[...], bco[...],
                        wbd[...], ps[...], wao[...])
        out = h_ref[...] + _dot(v["mixed"], wo[...])
        rows = _row_ids(TB, pl.program_id(0), (TB, D))
        o_ref[...] = jnp.where(rows >= PAD_ROWS, out, 0.0)

    return _pcall(
        body,
        grid=(R // TB,),
        in_specs=[pl.BlockSpec((TB, D), lambda i: (i, 0))] + _mix_act_specs(TB) + _MIX_W_SPECS,
        out_specs=pl.BlockSpec((TB, D), lambda i: (i, 0)),
        out_shape=jax.ShapeDtypeStruct((R, D), F32),
        compiler_params=_cp(("parallel",)),
        name=name,
    )(h, gates, c, pooled, att, *mw)


def _ffn_conv(ut, uh, cw_ref, cb_ref, ext):
    ext[:, pl.ds(0, HALO8), :] = uh
    ext[:, pl.ds(HALO8, TR), :] = ut
    um1 = ext[:, pl.ds(HALO8 - 1, TR), :]
    um2 = ext[:, pl.ds(HALO8 - 2, TR), :]
    cw = cw_ref[...]
    conv = cw[0][:, None, :] * um2 + cw[1][:, None, :] * um1 + cw[2][:, None, :] * ut + cb_ref[...][:, None, :]
    return conv, um1, um2


def _gelu_parts(x):
    th = jnp.tanh(GELU_C0 * (x + GELU_C1 * x * x * x))
    return th, 0.5 * x * (1.0 + th)


def _ffn_in_specs(nrow8, order):
    n8 = TR // HALO8
    return [
        pl.BlockSpec((2, TR, TF), lambda *g: (0, order(*g)[0], order(*g)[1])),
        pl.BlockSpec((2, HALO8, TF), lambda *g: (0, jnp.maximum(order(*g)[0] * n8 - 1, 0), order(*g)[1])),
        pl.BlockSpec((FFN_K, 2, TF), lambda *g: (0, 0, order(*g)[1])),
        pl.BlockSpec((2, TF), lambda *g: (0, order(*g)[1])),
    ]


def _ffn_fwd(u3, h2, cw, cb, w_down, name):
    R = h2.shape[0]
    nj = D_FF // TF

    def body(u_ref, uh_ref, cw_ref, cb_ref, wd_ref, h_ref, o_ref, ext, acc):
        i, j = pl.program_id(0), pl.program_id(1)
        uh = jnp.where(i == 0, 0.0, uh_ref[...])
        conv, _, _ = _ffn_conv(u_ref[...], uh, cw_ref, cb_ref, ext)
        _, a = _gelu_parts(conv[0])
        part = _dot(a * conv[1], wd_ref[...])

        @pl.when(j == 0)
        def _():
            acc[...] = part

        @pl.when(j > 0)
        def _():
            acc[...] += part

        @pl.when(j == nj - 1)
        def _():
            rows = _row_ids(TR, i, (TR, D))
            o_ref[...] = jnp.where(rows >= PAD_ROWS, h_ref[...] + acc[...], 0.0)

    return _pcall(
        body,
        grid=(R // TR, nj),
        in_specs=_ffn_in_specs(R // HALO8, lambda i, j: (i, j))
        + [pl.BlockSpec((TF, D), lambda i, j: (j, 0)), pl.BlockSpec((TR, D), lambda i, j: (i, 0))],
        out_specs=pl.BlockSpec((TR, D), lambda i, j: (i, 0)),
        out_shape=jax.ShapeDtypeStruct((R, D), F32),
        scratch_shapes=[pltpu.VMEM((2, TR + HALO8, TF), F32), pltpu.VMEM((TR, D), F32)],
        compiler_params=_cp(("parallel", "arbitrary")),
        name=name,
    )(u3, u3, cw, cb, w_down.astype(_MXU), h2)


def _loss_bwd(h, target, gain, name):
    R = h.shape[0]

    def body(h_ref, t_ref, g_ref, dh_ref, loss_ref, dg_ref):
        i = pl.program_id(0)

        @pl.when(i == 0)
        def _():
            loss_ref[...] = jnp.zeros_like(loss_ref)
            dg_ref[...] = jnp.zeros_like(dg_ref)
            dh_ref[...] = jnp.zeros_like(dh_ref)

        @pl.when(i > 0)
        def _():
            g = g_ref[...]
            r, n, y = _rms_fwd(h_ref[...], g)
            e = y - t_ref[...]
            loss_ref[...] += (0.5 / D) * jnp.sum(jnp.sum(e * e, axis=1, keepdims=True), axis=0, keepdims=True)
            dx, dgain = _rms_bwd(e * (1.0 / D), r, n, g)
            dg_ref[...] += dgain
            dh_ref[...] = dx

    return _pcall(
        body,
        grid=(R // BLK,),
        in_specs=[
            pl.BlockSpec((BLK, D), lambda i: (i, 0)),
            pl.BlockSpec((BLK, D), lambda i: (jnp.maximum(i - 1, 0), 0)),
            pl.BlockSpec((1, D), lambda i: (0, 0)),
        ],
        out_specs=[
            pl.BlockSpec((BLK, D), lambda i: (i, 0)),
            pl.BlockSpec((1, 1), lambda i: (0, 0)),
            pl.BlockSpec((1, D), lambda i: (0, 0)),
        ],
        out_shape=[
            jax.ShapeDtypeStruct((R, D), F32),
            jax.ShapeDtypeStruct((1, 1), F32),
            jax.ShapeDtypeStruct((1, D), F32),
        ],
        compiler_params=_cp(("arbitrary",)),
        name=name,
    )(h, target, gain)


def _tdot_acc(a, b, name):
    R, M = a.shape
    split = b.ndim == 3
    N = 2 * b.shape[2] if split else b.shape[1]
    tn = b.shape[2] if split else N
    tm = min(M, 512)
    tk = TR

    def body(a_ref, b_ref, o_ref):
        part = _tdot(a_ref[...], b_ref[...])

        @pl.when(pl.program_id(2) == 0)
        def _():
            o_ref[...] = part

        @pl.when(pl.program_id(2) > 0)
        def _():
            o_ref[...] += part

    if split:
        b_spec = pl.BlockSpec((None, tk, tn), lambda m, n, k: (n, k, 0))
    else:
        b_spec = pl.BlockSpec((tk, tn), lambda m, n, k: (k, n))
    return _pcall(
        body,
        grid=(M // tm, N // tn, R // tk),
        in_specs=[pl.BlockSpec((tk, tm), lambda m, n, k: (k, m)), b_spec],
        out_specs=pl.BlockSpec((tm, tn), lambda m, n, k: (m, n)),
        out_shape=jax.ShapeDtypeStruct((M, N), F32),
        compiler_params=_cp(("parallel", "parallel", "arbitrary")),
        name=name,
    )(a, b)


def _ffn_bwd1(dh3, u3, cw, cb, w_down, name):
    R = dh3.shape[0]
    nj = D_FF // TF

    def body(u_ref, uh_ref, cw_ref, cb_ref, wd_ref, dh_ref, dc_ref, act_ref, dwb_ref, ext):
        j, i = pl.program_id(0), pl.program_id(1)
        ut = u_ref[...]
        uh = jnp.where(i == 0, 0.0, uh_ref[...])
        conv, um1, um2 = _ffn_conv(ut, uh, cw_ref, cb_ref, ext)
        gt, val = conv[0], conv[1]
        th, a = _gelu_parts(gt)
        dact = _dot_t(dh_ref[...], wd_ref[...])
        dgelu = 0.5 * (1.0 + th) + 0.5 * gt * (1.0 - th * th) * (GELU_C0 * (1.0 + 3.0 * GELU_C1 * gt * gt))
        dgt = dact * val * dgelu
        dval = dact * a
        act_ref[...] = (a * val).astype(act_ref.dtype)
        dc_ref[0] = dgt
        dc_ref[1] = dval

        @pl.when(i == 0)
        def _():
            dwb_ref[...] = jnp.zeros_like(dwb_ref)

        for half, dcv in ((0, dgt), (1, dval)):
            dwb_ref[half, pl.ds(0, 1), :] += jnp.sum(um2[half] * dcv, axis=0, keepdims=True)
            dwb_ref[half, pl.ds(1, 1), :] += jnp.sum(um1[half] * dcv, axis=0, keepdims=True)
            dwb_ref[half, pl.ds(2, 1), :] += jnp.sum(ut[half] * dcv, axis=0, keepdims=True)
            dwb_ref[half, pl.ds(3, 1), :] += jnp.sum(dcv, axis=0, keepdims=True)

    return _pcall(
        body,
        grid=(nj, R // TR),
        in_specs=_ffn_in_specs(R // HALO8, lambda j, i: (i, j))
        + [pl.BlockSpec((TF, D), lambda j, i: (j, 0)), pl.BlockSpec((TR, D), lambda j, i: (i, 0))],
        out_specs=[
            pl.BlockSpec((2, TR, TF), lambda j, i: (0, i, j)),
            pl.BlockSpec((TR, TF), lambda j, i: (i, j)),
            pl.BlockSpec((2, 8, TF), lambda j, i: (0, 0, j)),
        ],
        out_shape=[
            jax.ShapeDtypeStruct((2, R, D_FF), F32),
            jax.ShapeDtypeStruct((R, D_FF), _MXU),
            jax.ShapeDtypeStruct((2, 8, D_FF), F32),
        ],
        scratch_shapes=[pltpu.VMEM((2, TR + HALO8, TF), F32)],
        compiler_params=_cp(("parallel", "arbitrary")),
        name=name,
    )(u3, u3, cw, cb, w_down.astype(_MXU), dh3)


def _ffn_bwd2(dc3, cw, w_up, h2, dh3, gain, name):
    R = h2.shape[0]
    nj = D_FF // TF
    n8 = TR // HALO8
    last8 = R // HALO8 - 1
    ni = R // TR

    def body(dc_ref, dn_ref, cw_ref, wg_ref, wv_ref, h_ref, dh_ref, g_ref, du_ref, o_ref, xn_ref, dg_ref, ext, acc):
        i, j = pl.program_id(0), pl.program_id(1)
        dc = dc_ref[...]
        ext[:, pl.ds(0, TR), :] = dc
        ext[:, pl.ds(TR, HALO8), :] = jnp.where(i == ni - 1, 0.0, dn_ref[...])
        cw = cw_ref[...]
        du = (cw[2][:, None, :] * dc + cw[1][:, None, :] * ext[:, pl.ds(1, TR), :]
              + cw[0][:, None, :] * ext[:, pl.ds(2, TR), :])
        du_ref[...] = du.astype(du_ref.dtype)
        part = _dot_t(du[0], wg_ref[...]) + _dot_t(du[1], wv_ref[...])

        @pl.when(j == 0)
        def _():
            acc[...] = part

        @pl.when(j > 0)
        def _():
            acc[...] += part

        @pl.when(jnp.logical_and(i == 0, j == 0))
        def _():
            dg_ref[...] = jnp.zeros_like(dg_ref)

        @pl.when(j == nj - 1)
        def _():
            g = g_ref[...]
            r, n, y = _rms_fwd(h_ref[...], g)
            dx, dgain = _rms_bwd(acc[...], r, n, g)
            dg_ref[...] += dgain
            rows = _row_ids(TR, i, (TR, D))
            o_ref[...] = jnp.where(rows >= PAD_ROWS, dh_ref[...] + dx, 0.0)
            xn_ref[...] = y.astype(xn_ref.dtype)

    w_up = w_up.astype(_MXU)
    return _pcall(
        body,
        grid=(ni, nj),
        in_specs=[
            pl.BlockSpec((2, TR, TF), lambda i, j: (0, i, j)),
            pl.BlockSpec((2, HALO8, TF), lambda i, j: (0, jnp.minimum((i + 1) * n8, last8), j)),
            pl.BlockSpec((FFN_K, 2, TF), lambda i, j: (0, 0, j)),
            pl.BlockSpec((D, TF), lambda i, j: (0, j)),
            pl.BlockSpec((D, TF), lambda i, j: (0, nj + j)),
            pl.BlockSpec((TR, D), lambda i, j: (i, 0)),
            pl.BlockSpec((TR, D), lambda i, j: (i, 0)),
            pl.BlockSpec((1, D), lambda i, j: (0, 0)),
        ],
        out_specs=[
            pl.BlockSpec((2, TR, TF), lambda i, j: (0, i, j)),
            pl.BlockSpec((TR, D), lambda i, j: (i, 0)),
            pl.BlockSpec((TR, D), lambda i, j: (i, 0)),
            pl.BlockSpec((1, D), lambda i, j: (0, 0)),
        ],
        out_shape=[
            jax.ShapeDtypeStruct((2, R, D_FF), _MXU),
            jax.ShapeDtypeStruct((R, D), F32),
            jax.ShapeDtypeStruct((R, D), _MXU),
            jax.ShapeDtypeStruct((1, D), F32),
        ],
        scratch_shapes=[pltpu.VMEM((2, TR + HALO8, TF), F32), pltpu.VMEM((TR, D), F32)],
        compiler_params=_cp(("arbitrary", "arbitrary")),
        name=name,
    )(dc3, dc3, cw, w_up, w_up, h2, dh3, gain)


def _mix_bwd(dh2, gates, c, pooled, att, mw, name):
    R = dh2.shape[0]

    def body(dh_ref, g_ref, c_ref, p_ref, a_ref, lg, lb, wco, bco, wbd, ps, wao, wo,
             dg_ref, dc_ref, dp_ref, da_ref, mx_ref, s_ref, dya_ref, dyb_ref, dyc_ref, accd_ref, accc_ref):
        v = _mix_values(c_ref[...], p_ref[...], a_ref[...], g_ref[...], lg[...], lb[...], wco[...], bco[...],
                        wbd[...], ps[...], wao[...])
        dmix = _dot_t(dh_ref[...], wo[...])
        for k, (gk, yk) in enumerate(((v["g0"], v["ya"]), (v["g1"], v["yb"]), (v["g2"], v["yc"]))):
            dg_ref[:, k * D : (k + 1) * D] = dmix * yk * gk * (1.0 - gk)
        dya = dmix * v["g0"]
        dyb = dmix * v["g1"]
        dyc = dmix * v["g2"]
        ds = _dot_t(dya, wco[...])
        ln, sg, nl = v["ln"], v["sg"], v["nl"]
        dln = ds * (sg * (1.0 + ln * (1.0 - sg)))
        dn = dln * lg[...]
        dc = v["rstd"] * (dn - jnp.mean(dn, axis=-1, keepdims=True) - nl * jnp.mean(dn * nl, axis=-1, keepdims=True))
        dybs = dyb * ps[...]
        dc_ref[...] = dc
        dp_ref[...] = _dot_t(dybs, wbd[...])
        da_ref[...] = _dot_t(dyc, wao[...])
        mx_ref[...] = v["mixed"].astype(mx_ref.dtype)
        s_ref[...] = v["s"].astype(s_ref.dtype)
        dya_ref[...] = dya.astype(dya_ref.dtype)
        dyb_ref[...] = dybs.astype(dyb_ref.dtype)
        dyc_ref[...] = dyc.astype(dyc_ref.dtype)

        @pl.when(pl.program_id(0) == 0)
        def _():
            accd_ref[...] = jnp.zeros_like(accd_ref)
            accc_ref[...] = jnp.zeros_like(accc_ref)

        accd_ref[pl.ds(0, 1), :] += jnp.sum(dya, axis=0, keepdims=True)
        accd_ref[pl.ds(1, 1), :] += jnp.sum(dyb * v["ybr"], axis=0, keepdims=True)
        accc_ref[pl.ds(0, 1), :] += jnp.sum(dln * nl, axis=0, keepdims=True)
        accc_ref[pl.ds(1, 1), :] += jnp.sum(dln, axis=0, keepdims=True)
        accc_ref[pl.ds(2, 1), :] += jnp.sum(dc, axis=0, keepdims=True)

    def row(w):
        return pl.BlockSpec((TB, w), lambda i: (i, 0))

    return _pcall(
        body,
        grid=(R // TB,),
        in_specs=[row(D)] + _mix_act_specs(TB) + _MIX_W_SPECS,
        out_specs=[row(D_G), row(CONV_CH), row(POOL_CH), row(ATT_W), row(D), row(CONV_CH), row(D), row(D), row(D),
                   pl.BlockSpec((8, D), lambda i: (0, 0)), pl.BlockSpec((8, CONV_CH), lambda i: (0, 0))],
        out_shape=[
            jax.ShapeDtypeStruct((R, D_G), F32),
            jax.ShapeDtypeStruct((R, CONV_CH), F32),
            jax.ShapeDtypeStruct((R, POOL_CH), F32),
            jax.ShapeDtypeStruct((R, ATT_W), F32),
            jax.ShapeDtypeStruct((R, D), _MXU),
            jax.ShapeDtypeStruct((R, CONV_CH), _MXU),
            jax.ShapeDtypeStruct((R, D), _MXU),
            jax.ShapeDtypeStruct((R, D), _MXU),
            jax.ShapeDtypeStruct((R, D), _MXU),
            jax.ShapeDtypeStruct((8, D), F32),
            jax.ShapeDtypeStruct((8, CONV_CH), F32),
        ],
        compiler_params=_cp(("arbitrary",)),
        name=name,
    )(dh2, gates, c, pooled, att, *mw)


def _attn_bwd(qkv, att, datt, name):
    R = qkv.shape[0]
    nb = R // BLK

    def body(q_ref, k_ref, v_ref, o_ref, do_ref, dq_ref, dk_ref, dv_ref):
        i = pl.program_id(1)

        @pl.when(i == 0)
        def _():
            dk_ref[...] = jnp.zeros_like(dk_ref)
            dv_ref[...] = jnp.zeros_like(dv_ref)

        q = q_ref[...]
        do = do_ref[...]
        o = o_ref[...]
        dob = do.astype(_MXU)
        dof = dob.astype(F32)
        tri = _tri()

        def step(jj, carry):
            dq, s_carry, o_carry = carry
            j = i - jj
            off = pl.multiple_of(j * BLK, BLK)
            kb = k_ref[pl.ds(off, BLK), :]
            vb = v_ref[pl.ds(off, BLK), :]
            z, en, lk, a, valid = _attn_block(q, kb, i, j, s_carry, tri)
            e = _dot_t(dob, vb) * a
            hi = e.astype(_MXU)
            lo = (e - hi.astype(F32)).astype(_MXU)
            e_after = jnp.dot(hi, tri, preferred_element_type=F32) + jnp.dot(lo, tri, preferred_element_type=F32)
            rest = jnp.sum(dof * (o - o_carry), axis=1, keepdims=True)
            before = rest - (e + e_after)
            inv = 1.0 / (1.0 + en)
            beta = jnp.where(z >= 0.0, inv, en * inv)
            dz = jnp.where(valid, e * (1.0 - beta) - beta * before, 0.0)
            dzb = dz.astype(_MXU)
            dq = dq + jnp.dot(dzb, kb, preferred_element_type=F32)
            dk_ref[pl.ds(off, BLK), :] += _tdot(dzb, q)
            dv_ref[pl.ds(off, BLK), :] += _tdot(a, dob)
            s_carry = s_carry + jnp.sum(lk, axis=1, keepdims=True)
            o_carry = o_carry + _dot(a, vb)
            return dq, s_carry, o_carry

        zero2 = jnp.zeros((BLK, HD), F32)
        dq, _, _ = lax.fori_loop(0, i + 1, step, (zero2, jnp.zeros((BLK, 1), F32), zero2))
        dq_ref[...] = dq * Q_SCALE

    blk = pl.BlockSpec((BLK, HD), lambda h, i: (i, h))
    col = pl.BlockSpec((R, HD), lambda h, i: (0, h))
    return _pcall(
        body,
        grid=(NH, nb),
        in_specs=[
            blk,
            pl.BlockSpec((R, HD), lambda h, i: (0, NH + h)),
            pl.BlockSpec((R, HD), lambda h, i: (0, 2 * NH + h)),
            blk,
            blk,
        ],
        out_specs=[blk, col, col],
        out_shape=[jax.ShapeDtypeStruct((R, ATT_W), F32)] * 3,
        compiler_params=pltpu.CompilerParams(dimension_semantics=("arbitrary", "arbitrary"),
                                             vmem_limit_bytes=56 * 1024 * 1024),
        name=name,
    )(qkv, qkv, qkv, att, datt)


def _branch_bwd(dc, dpooled, proj_a, conv_w, name):
    R = dc.shape[0]
    nh = TR // HALO
    last = R // HALO - 1
    ni = R // TR

    def body(dc_ref, dcn_ref, dp_ref, dpn_ref, t_ref, h_ref, w_ref, o_ref, dcw_ref, gext, dcext, eext):
        i = pl.program_id(0)
        is_last = i == ni - 1
        t = t_ref[...]
        hl = jnp.where(i == 0, 0.0, h_ref[...])
        a = t[:, :CONV_CH]
        sg = _sig(t[:, CONV_CH : 2 * CONV_CH])
        gext[pl.ds(0, HALO), :] = hl[:, :CONV_CH] * _sig(hl[:, CONV_CH : 2 * CONV_CH])
        gext[pl.ds(HALO, TR), :] = a * sg
        dct = dc_ref[...]
        dcext[pl.ds(0, TR), :] = dct
        dcext[pl.ds(TR, HALO), :] = jnp.where(is_last, 0.0, dcn_ref[...])

        @pl.when(i == 0)
        def _():
            dcw_ref[...] = jnp.zeros_like(dcw_ref)

        dglu = jnp.zeros((TR, CONV_CH), F32)
        for k in range(CONV_K):
            dglu = dglu + w_ref[pl.ds(k, 1), :] * dcext[pl.ds(CONV_K - 1 - k, TR), :]
            dcw_ref[pl.ds(k, 1), :] += jnp.sum(gext[pl.ds(HALO - (CONV_K - 1) + k, TR), :] * dct, axis=0, keepdims=True)
        o_ref[:, :CONV_CH] = dglu * sg
        o_ref[:, CONV_CH : 2 * CONV_CH] = dglu * a * sg * (1.0 - sg)

        lane, wsize = _pool_consts(i)
        dpt = dp_ref[...]
        eext[pl.ds(0, TR), :] = dpt / _pool_div(_row_ids(TR, i, (TR, POOL_CH)), wsize)
        nxt = dpn_ref[...] / _pool_div(_row_ids(TR, i + 1, (HALO, POOL_CH)), wsize)
        eext[pl.ds(TR, HALO), :] = jnp.where(is_last, 0.0, nxt)

        def fwd(k):
            return eext[pl.ds(k, TR), :]

        s2 = fwd(0) + fwd(1)
        s4 = s2 + fwd(2) + fwd(3)
        s8 = s4 + fwd(4) + fwd(5) + fwd(6) + fwd(7)
        s16 = s8
        for k in range(8, 16):
            s16 = s16 + fwd(k)
        o_ref[:, 2 * CONV_CH :] = _lane_select(lane, s2, s4, s8, s16) - dpt

    def nxt_spec(w):
        return pl.BlockSpec((HALO, w), lambda i: (jnp.minimum((i + 1) * nh, last), 0))

    return _pcall(
        body,
        grid=(ni,),
        in_specs=[
            pl.BlockSpec((TR, CONV_CH), lambda i: (i, 0)),
            nxt_spec(CONV_CH),
            pl.BlockSpec((TR, POOL_CH), lambda i: (i, 0)),
            nxt_spec(POOL_CH),
            pl.BlockSpec((TR, TC), lambda i: (i, 0)),
            pl.BlockSpec((HALO, TC), lambda i: (jnp.maximum(i * nh - 1, 0), 0)),
            pl.BlockSpec((HALO, CONV_CH), lambda i: (0, 0)),
        ],
        out_specs=[pl.BlockSpec((TR, TC), lambda i: (i, 0)), pl.BlockSpec((HALO, CONV_CH), lambda i: (0, 0))],
        out_shape=[jax.ShapeDtypeStruct((R, TC), F32), jax.ShapeDtypeStruct((HALO, CONV_CH), F32)],
        scratch_shapes=[
            pltpu.VMEM((TR + HALO, CONV_CH), F32),
            pltpu.VMEM((TR + HALO, CONV_CH), F32),
            pltpu.VMEM((TR + HALO, POOL_CH), F32),
        ],
        compiler_params=_cp(("arbitrary",)),
        name=name,
    )(dc, dc, dpooled, dpooled, proj_a, proj_a, conv_w)


def _in_bwd(dfront, dq, dk, dv, dgates, w_a, w_g, h, dh2, gain, name):
    R = h.shape[0]
    nj = 1 + D_G // TC
    ni = R // TR
    w_a = w_a.astype(_MXU)
    w_f, w_q = w_a[:, :TC], w_a[:, TC : TC + ATT_W]
    w_k, w_v = w_a[:, TC + ATT_W : TC + 2 * ATT_W], w_a[:, TC + 2 * ATT_W :]

    def body(df_ref, dq_ref, dk_ref, dv_ref, dg_ref, wf_ref, wq_ref, wk_ref, wv_ref, wg_ref, h_ref, dh_ref, g_ref,
             o_ref, hn_ref, dgain_ref, acc):
        i, j = pl.program_id(0), pl.program_id(1)

        @pl.when(j == 0)
        def _():
            acc[...] = (_dot_t(df_ref[...], wf_ref[...]) + _dot_t(dq_ref[...], wq_ref[...])
                        + _dot_t(dk_ref[...], wk_ref[...]) + _dot_t(dv_ref[...], wv_ref[...]))

        @pl.when(j > 0)
        def _():
            acc[...] += _dot_t(dg_ref[...], wg_ref[...])

        @pl.when(jnp.logical_and(i == 0, j == 0))
        def _():
            dgain_ref[...] = jnp.zeros_like(dgain_ref)

        @pl.when(j == nj - 1)
        def _():
            g = g_ref[...]
            r, n, y = _rms_fwd(h_ref[...], g)
            dx, dgain = _rms_bwd(acc[...], r, n, g)
            dgain_ref[...] += dgain
            rows = _row_ids(TR, i, (TR, D))
            o_ref[...] = jnp.where(rows >= PAD_ROWS, dh_ref[...] + dx, 0.0)
            hn_ref[...] = y.astype(hn_ref.dtype)

    def row(w):
        return pl.BlockSpec((TR, w), lambda i, j: (i, 0))

    def whole(w):
        return pl.BlockSpec((D, w), lambda i, j: (0, 0))

    def gcol(i, j):
        return jnp.maximum(j - 1, 0)

    return _pcall(
        body,
        grid=(ni, nj),
        in_specs=[
            row(TC), row(ATT_W), row(ATT_W), row(ATT_W),
            pl.BlockSpec((TR, TC), lambda i, j: (i, gcol(i, j))),
            whole(TC), whole(ATT_W), whole(ATT_W), whole(ATT_W),
            pl.BlockSpec((D, TC), lambda i, j: (0, gcol(i, j))),
            row(D), row(D),
            pl.BlockSpec((1, D), lambda i, j: (0, 0)),
        ],
        out_specs=[row(D), row(D), pl.BlockSpec((1, D), lambda i, j: (0, 0))],
        out_shape=[jax.ShapeDtypeStruct((R, D), F32), jax.ShapeDtypeStruct((R, D), _MXU), jax.ShapeDtypeStruct((1, D), F32)],
        scratch_shapes=[pltpu.VMEM((TR, D), F32)],
        compiler_params=_cp(("arbitrary", "arbitrary")),
        name=name,
    )(dfront, dq, dk, dv, dgates, w_f, w_q, w_k, w_v, w_g.astype(_MXU), h, dh2, gain)


def _pool_blockdiag(w_grp):
    eye = jnp.eye(len(POOL_WINDOWS), dtype=w_grp.dtype)
    return jnp.einsum("gcd,gh->gchd", w_grp, eye).reshape(POOL_CH, D)


def _pool_blockdiag_grad(dw_bd):
    d4 = dw_bd.reshape(len(POOL_WINDOWS), POOL_GC, len(POOL_WINDOWS), D // len(POOL_WINDOWS))
    return jnp.stack([d4[g, :, g, :] for g in range(len(POOL_WINDOWS))])


def _local_step(x, target, p):
    row = lambda a: a.reshape(1, -1)
    h = jnp.concatenate([jnp.zeros((PAD_ROWS, D), F32), p["meta"], x], axis=0)
    saved = []
    for l in range(N_LAYERS):
        w_a, w_g = p["w_in"][l][:, :D_A], p["w_in"][l][:, D_A:]
        conv_w = jnp.concatenate([p["conv_dw_w"][l], jnp.zeros((1, CONV_CH), F32)], axis=0)
        cw3 = p["ffn_dw_w"][l].reshape(FFN_K, 2, D_FF)
        cb2 = p["ffn_dw_b"][l].reshape(2, D_FF)
        mw = (row(p["conv_ln_g"][l]), row(p["conv_ln_b"][l]), p["w_conv_out"][l].astype(_MXU), row(p["b_conv_out"][l]),
              _pool_blockdiag(p["w_pool_grp"][l]).astype(_MXU), row(p["pool_scale"][l]),
              p["w_attn_out"][l].astype(_MXU), p["w_o"][l].astype(_MXU))
        proj_a = _rms_matmul(h, row(p["norm1"][l]), w_a, f"proj_a{l}")
        gates = _rms_matmul(h, row(p["norm1"][l]), w_g, f"proj_g{l}")
        qkv = _qkv_cast(proj_a, f"qkv_cast{l}")
        c, pooled = _branch_pre(proj_a, conv_w, row(p["conv_dw_b"][l]), f"branch_pre{l}")
        att = _attn_fwd(qkv, f"attn_fwd{l}")
        h2 = _mix_fwd(h, gates, c, pooled, att, mw, f"mix_fwd{l}")
        u3 = _rms_matmul(h2, row(p["norm2"][l]), p["w_up"][l], f"ffn_up{l}", split_out=True)
        h3 = _ffn_fwd(u3, h2, cw3, cb2, p["w_down"][l], f"ffn_fwd{l}")
        saved.append(dict(h=h, w_a=w_a, w_g=w_g, conv_w=conv_w, cw3=cw3, cb2=cb2, mw=mw, proj_a=proj_a, gates=gates,
                          qkv=qkv, c=c, pooled=pooled, att=att, h2=h2, u3=u3))
        h = h3

    dh, loss, d_final = _loss_bwd(h, target, row(p["final_norm"]), "loss_bwd")

    g = {k: [None] * N_LAYERS for k in ("norm1", "w_in", "conv_dw_w", "conv_dw_b", "conv_ln_g", "conv_ln_b", "w_conv_out",
                                        "b_conv_out", "w_pool_grp", "pool_scale", "w_attn_out", "w_o", "norm2", "w_up",
                                        "ffn_dw_w", "ffn_dw_b", "w_down")}
    for l in reversed(range(N_LAYERS)):
        s = saved[l]
        dc3, act, dwb = _ffn_bwd1(dh, s["u3"], s["cw3"], s["cb2"], p["w_down"][l], f"ffn_bwd1_{l}")
        g["w_down"][l] = _tdot_acc(act, dh, f"dw_down{l}")
        g["ffn_dw_w"][l] = jnp.transpose(dwb[:, :FFN_K, :], (1, 0, 2)).reshape(FFN_K, 2 * D_FF)
        g["ffn_dw_b"][l] = dwb[:, FFN_K, :].reshape(2 * D_FF)
        du3, dh2, xn2, dg2 = _ffn_bwd2(dc3, s["cw3"], p["w_up"][l], s["h2"], dh, row(p["norm2"][l]), f"ffn_bwd2_{l}")
        g["norm2"][l] = dg2[0]
        g["w_up"][l] = _tdot_acc(xn2, du3, f"dw_up{l}")
        (dgates, dc, dpooled, datt, mixed, s_act, dya, dybs, dyc, acc_d, acc_c) = _mix_bwd(
            dh2, s["gates"], s["c"], s["pooled"], s["att"], s["mw"], f"mix_bwd{l}")
        g["w_o"][l] = _tdot_acc(mixed, dh2, f"dw_o{l}")
        g["w_conv_out"][l] = _tdot_acc(s_act, dya, f"dw_conv_out{l}")
        g["w_pool_grp"][l] = _pool_blockdiag_grad(_tdot_acc(s["pooled"], dybs, f"dw_pool{l}"))
        g["w_attn_out"][l] = _tdot_acc(s["att"], dyc, f"dw_attn_out{l}")
        g["b_conv_out"][l] = acc_d[0]
        g["pool_scale"][l] = acc_d[1]
        g["conv_ln_g"][l] = acc_c[0]
        g["conv_ln_b"][l] = acc_c[1]
        g["conv_dw_b"][l] = acc_c[2]
        dq, dk, dv = _attn_bwd(s["qkv"], s["att"], datt, f"attn_bwd{l}")
        dfront, dcw = _branch_bwd(dc, dpooled, s["proj_a"], s["conv_w"], f"branch_bwd{l}")
        g["conv_dw_w"][l] = dcw[:CONV_K]
        dh, hn, dg1 = _in_bwd(dfront, dq, dk, dv, dgates, s["w_a"], s["w_g"], s["h"], dh2, row(p["norm1"][l]),
                              f"in_bwd{l}")
        g["norm1"][l] = dg1[0]
        g["w_in"][l] = jnp.concatenate(
            [_tdot_acc(hn, part, f"dw_in{l}_{k}") for k, part in enumerate((dfront, dq, dk, dv, dgates))], axis=1)

    grads = {k: jnp.stack(v) for k, v in g.items()}
    grads["final_norm"] = d_final[0]
    grads["meta"] = dh[PAD_ROWS:BLK]
    return loss, dh[BLK:], grads


def _axes():
    return lax.axis_index("x"), lax.axis_index("y"), lax.axis_index("c")


def _allgather_xy(blobs, name):
    nt = len(blobs)
    flips = ((1, 0), (0, 1), (1, 1))

    def body(*refs):
        srcs, dsts = refs[:nt], refs[nt : 2 * nt]
        send_sems, recv_sems, local_sems = refs[2 * nt :]
        x, y, c = _axes()
        mine = 2 * x + y
        started = []
        for t in range(nt):
            own = pltpu.make_async_copy(srcs[t], dsts[t].at[mine], local_sems.at[t])
            own.start()
            started.append(own)
        for t in range(nt):
            for f, (fx, fy) in enumerate(flips):
                px = 1 - x if fx else x
                py = 1 - y if fy else y
                out = pltpu.make_async_remote_copy(srcs[t], dsts[t].at[mine], send_sems.at[t, f], recv_sems.at[t, f],
                                                   device_id=(px, py, c), device_id_type=MESH)
                out.start()
                arriving = pltpu.make_async_remote_copy(srcs[t], dsts[t].at[2 * px + py], send_sems.at[t, f],
                                                        recv_sems.at[t, f], device_id=(px, py, c), device_id_type=MESH)
                started.append((out, arriving))
        for cp in started:
            if isinstance(cp, tuple):
                cp[0].wait_send()
                cp[1].wait_recv()
            else:
                cp.wait()

    return _pcall(
        body,
        in_specs=[ANY] * nt,
        out_specs=[ANY] * nt,
        out_shape=[jax.ShapeDtypeStruct((4,) + b.shape, b.dtype) for b in blobs],
        scratch_shapes=[pltpu.SemaphoreType.DMA((nt, 3)), pltpu.SemaphoreType.DMA((nt, 3)), pltpu.SemaphoreType.DMA((nt,))],
        compiler_params=pltpu.CompilerParams(has_side_effects=True),
        name=name,
    )(*blobs)


def _grad_exchange(gblob, name):
    shape = gblob.shape[1:]

    def body(src, dst, send_sems, recv_sems, local_sem):
        x, y, c = _axes()
        me = 4 * x + 2 * y + c
        own = pltpu.make_async_copy(src.at[2 * x + y], dst.at[me], local_sem)
        own.start()
        pairs = []
        for r in range(1, 8):
            fx, fy, fc = (r >> 2) & 1, (r >> 1) & 1, r & 1
            px = 1 - x if fx else x
            py = 1 - y if fy else y
            pc = 1 - c if fc else c
            out = pltpu.make_async_remote_copy(src.at[2 * px + py], dst.at[me], send_sems.at[r - 1], recv_sems.at[r - 1],
                                               device_id=(px, py, pc), device_id_type=MESH)
            out.start()
            arriving = pltpu.make_async_remote_copy(src.at[2 * x + y], dst.at[4 * px + 2 * py + pc], send_sems.at[r - 1],
                                                    recv_sems.at[r - 1], device_id=(px, py, pc), device_id_type=MESH)
            pairs.append((out, arriving))
        for out, arriving in pairs:
            out.wait_send()
            arriving.wait_recv()
        own.wait()

    return _pcall(
        body,
        in_specs=[ANY],
        out_specs=ANY,
        out_shape=jax.ShapeDtypeStruct((8,) + shape, gblob.dtype),
        scratch_shapes=[pltpu.SemaphoreType.DMA((7,)), pltpu.SemaphoreType.DMA((7,)), pltpu.SemaphoreType.DMA],
        compiler_params=pltpu.CompilerParams(has_side_effects=True),
        name=name,
    )(gblob)


def _adamw(w, m, v, gslots, name):
    rows, lanes = w.shape

    def body(w_ref, m_ref, v_ref, gs_ref, g_ref, d_ref, nm_ref, nv_ref):
        g = gs_ref[0]
        for k in range(1, 8):
            g = g + gs_ref[k]
        w_, m_, v_ = w_ref[...], m_ref[...], v_ref[...]
        m_new = ADAM_B1 * m_ + (1.0 - ADAM_B1) * g
        v_new = ADAM_B2 * v_ + (1.0 - ADAM_B2) * (g * g)
        m_hat = m_new / (1.0 - ADAM_B1 ** ADAM_STEP)
        v_hat = v_new / (1.0 - ADAM_B2 ** ADAM_STEP)
        g_ref[...] = g
        d_ref[...] = -ADAM_LR * (m_hat / (jnp.sqrt(v_hat) + ADAM_EPS) + ADAM_WD * w_)
        nm_ref[...] = m_new
        nv_ref[...] = v_new

    blk = pl.BlockSpec((ADAM_ROWS, lanes), lambda i: (i, 0))
    return _pcall(
        body,
        grid=(rows // ADAM_ROWS,),
        in_specs=[blk, blk, blk, pl.BlockSpec((8, ADAM_ROWS, lanes), lambda i: (0, i, 0))],
        out_specs=[blk] * 4,
        out_shape=[jax.ShapeDtypeStruct((rows, lanes), F32)] * 4,
        compiler_params=_cp(("parallel",)),
        name=name,
    )(w, m, v, gslots)


_PARAMS = (
    ("meta", (N_META, D), 1),
    ("norm1", (N_LAYERS, D), None),
    ("w_in", (N_LAYERS, D, D_A + D_G), 2),
    ("conv_dw_w", (N_LAYERS, CONV_K, CONV_CH), 2),
    ("conv_dw_b", (N_LAYERS, CONV_CH), None),
    ("conv_ln_g", (N_LAYERS, CONV_CH), None),
    ("conv_ln_b", (N_LAYERS, CONV_CH), None),
    ("w_conv_out", (N_LAYERS, CONV_CH, D), 2),
    ("b_conv_out", (N_LAYERS, D), None),
    ("w_pool_grp", (N_LAYERS, len(POOL_WINDOWS), POOL_GC, D // len(POOL_WINDOWS)), 3),
    ("pool_scale", (N_LAYERS, D), None),
    ("w_attn_out", (N_LAYERS, ATT_W, D), 2),
    ("w_o", (N_LAYERS, D, D), 1),
    ("norm2", (N_LAYERS, D), None),
    ("w_up", (N_LAYERS, D, 2 * D_FF), 2),
    ("ffn_dw_w", (N_LAYERS, FFN_K, 2 * D_FF), 2),
    ("ffn_dw_b", (N_LAYERS, 2 * D_FF), None),
    ("w_down", (N_LAYERS, D_FF, D), 1),
    ("final_norm", (D,), None),
)
_BIG = ("w_in", "w_conv_out", "w_pool_grp", "w_attn_out", "w_o", "w_up", "w_down")
_SMALL_SHARDED = ("meta", "conv_dw_w", "ffn_dw_w")
_SHARD_AXIS = {n: ax for n, _, ax in _PARAMS}


def _size(shape):
    n = 1
    for d in shape:
        n *= d
    return n


def _pack(parts, lanes, row_multiple):
    flat = jnp.concatenate([a.reshape(-1) for a in parts])
    rows = -(-flat.shape[0] // lanes)
    rows = -(-rows // row_multiple) * row_multiple
    flat = jnp.pad(flat, (0, rows * lanes - flat.shape[0]))
    return flat.reshape(rows, lanes)


def _unpack(blob, shapes):
    flat = blob.reshape(-1)
    out, off = [], 0
    for s in shapes:
        n = _size(s)
        out.append(flat[off : off + n].reshape(s))
        off += n
    return out


def _shard(a, ax, s):
    n = a.shape[ax] // 4
    return lax.slice_in_dim(a, s * n, (s + 1) * n, axis=ax)


def kernel(x, meta, norm1, w_in, conv_dw_w, conv_dw_b, conv_ln_g, conv_ln_b, w_conv_out, b_conv_out, w_pool_grp, pool_scale, w_attn_out, w_o, norm2, w_up, ffn_dw_w, ffn_dw_b, w_down, final_norm, loss_target, m_meta, m_norm1, m_w_in, m_conv_dw_w, m_conv_dw_b, m_conv_ln_g, m_conv_ln_b, m_w_conv_out, m_b_conv_out, m_w_pool_grp, m_pool_scale, m_w_attn_out, m_w_o, m_norm2, m_w_up, m_ffn_dw_w, m_ffn_dw_b, m_w_down, m_final_norm, v_meta, v_norm1, v_w_in, v_conv_dw_w, v_conv_dw_b, v_conv_ln_g, v_conv_ln_b, v_w_conv_out, v_b_conv_out, v_w_pool_grp, v_pool_scale, v_w_attn_out, v_w_o, v_norm2, v_w_up, v_ffn_dw_w, v_ffn_dw_b, v_w_down, v_final_norm):
    names = [n for n, _, _ in _PARAMS]
    w_loc = dict(zip(names, (meta, norm1, w_in, conv_dw_w, conv_dw_b, conv_ln_g, conv_ln_b, w_conv_out, b_conv_out, w_pool_grp, pool_scale, w_attn_out, w_o, norm2, w_up, ffn_dw_w, ffn_dw_b, w_down, final_norm)))
    m_loc = dict(zip(names, (m_meta, m_norm1, m_w_in, m_conv_dw_w, m_conv_dw_b, m_conv_ln_g, m_conv_ln_b, m_w_conv_out, m_b_conv_out, m_w_pool_grp, m_pool_scale, m_w_attn_out, m_w_o, m_norm2, m_w_up, m_ffn_dw_w, m_ffn_dw_b, m_w_down, m_final_norm)))
    v_loc = dict(zip(names, (v_meta, v_norm1, v_w_in, v_conv_dw_w, v_conv_dw_b, v_conv_ln_g, v_conv_ln_b, v_w_conv_out, v_b_conv_out, v_w_pool_grp, v_pool_scale, v_w_attn_out, v_w_o, v_norm2, v_w_up, v_ffn_dw_w, v_ffn_dw_b, v_w_down, v_final_norm)))

    big = _pack([w_loc[n].astype(_MXU) for n in _BIG], BLOB_LANES, 16)
    small = _pack([w_loc[n] for n in _SMALL_SHARDED], BLK, 8)
    big4, small4 = _allgather_xy([big, small], "allgather_weights")
    full = {n: w_loc[n] for n, _, ax in _PARAMS if ax is None}
    for group, blob4 in ((_BIG, big4), (_SMALL_SHARDED, small4)):
        per_chip = [_unpack(blob4[s], [w_loc[n].shape for n in group]) for s in range(4)]
        for k, n in enumerate(group):
            full[n] = jnp.concatenate([per_chip[s][k] for s in range(4)], axis=_SHARD_AXIS[n])

    loss, grad_x, grads = _local_step(x[0], loss_target[0], full)

    def owner_blob(src, s):
        return _pack([src[n] if ax is None else _shard(src[n], ax, s) for n, _, ax in _PARAMS], BLOB_LANES, ADAM_ROWS)

    gblob = jnp.stack([owner_blob(grads, s) for s in range(4)])
    gslots = _grad_exchange(gblob, "grad_exchange")
    local_shapes = [w_loc[n].shape for n in names]
    pk = lambda d: _pack([d[n] for n in names], BLOB_LANES, ADAM_ROWS)
    g_b, d_b, nm_b, nv_b = _adamw(pk(w_loc), pk(m_loc), pk(v_loc), gslots, "adamw")
    loss = lax.psum(loss[0, 0], ("x", "y", "c"))
    outs = [loss, grad_x[None]]
    for blob in (g_b, d_b, nm_b, nv_b):
        outs.extend(_unpack(blob, local_shapes))
    return tuple(outs)
```

```python
import functools

import jax
import jax.numpy as jnp
from jax import lax
from jax.experimental import pallas as pl
from jax.experimental.pallas import tpu as pltpu

F32 = jnp.float32
_MXU = jnp.bfloat16

D = 1024
N_META = 16
BLK = 128
CH = 2 * BLK
ATTN_UNROLL = 4
PAD_ROWS = BLK - N_META
N_LAYERS = 2
CONV_CH = 256
CONV_K = 31
POOL_CH = 256
POOL_WINDOWS = (2, 4, 8, 16)
POOL_GC = 64
ATT_W = 512
NH = 4
HD = 128
D_A = 2 * CONV_CH + POOL_CH + 3 * ATT_W
D_G = 3 * D
D_FF = 3 * D
FFN_K = 3
EPS = 1e-6
Q_SCALE = HD ** -0.5

TR = 384
TB = 128
HALO = 32
HALO8 = 8
TC = 768
TF = 512
VMEM_LIMIT = 48 * 1024 * 1024

ADAM_LR = 0.001
ADAM_B1 = 0.9
ADAM_B2 = 0.999
ADAM_EPS = 1e-08
ADAM_WD = 0.01
ADAM_STEP = 10

GELU_C0 = 0.7978845608028654
GELU_C1 = 0.044715

MESH = pl.DeviceIdType.MESH
ANY = pl.BlockSpec(memory_space=pl.ANY)

BLOB_LANES = 1024
ADAM_ROWS = 128


def _pcall(body, **kw):
    return pl.pallas_call(body, **kw)


def _cp(sem):
    return pltpu.CompilerParams(dimension_semantics=sem, vmem_limit_bytes=VMEM_LIMIT)


def _dot(a, b):
    return jnp.dot(a.astype(_MXU), b.astype(_MXU), preferred_element_type=F32)


def _dot_t(a, b):
    return lax.dot_general(a.astype(_MXU), b.astype(_MXU), (((1,), (1,)), ((), ())), preferred_element_type=F32)


def _tdot(a, b):
    return lax.dot_general(a.astype(_MXU), b.astype(_MXU), (((0,), (0,)), ((), ())), preferred_element_type=F32)


def _sig(x):
    return 1.0 / (1.0 + jnp.exp(-x))


def _rms_fwd(x, g):
    r = lax.rsqrt(jnp.mean(x * x, axis=-1, keepdims=True) + EPS)
    n = x * r
    return r, n, n * g


def _rms_bwd(dy, r, n, g):
    dgain = jnp.sum(dy * n, axis=0, keepdims=True)
    dn = dy * g
    dx = r * (dn - n * jnp.mean(dn * n, axis=-1, keepdims=True))
    return dx, dgain


def _row_ids(tile_rows, i, shape):
    return i * tile_rows + lax.broadcasted_iota(jnp.int32, shape, 0)


def _rms_matmul(h, gain, w, name, split_out=False):
    R, N = h.shape[0], w.shape[1]
    nj = N // TC
    half = nj // 2

    def body(h_ref, g_ref, w_ref, o_ref, xn_ref):
        @pl.when(pl.program_id(1) == 0)
        def _():
            _, _, y = _rms_fwd(h_ref[...], g_ref[...])
            xn_ref[...] = y.astype(xn_ref.dtype)

        o_ref[...] = jnp.dot(xn_ref[...], w_ref[...], preferred_element_type=F32)

    if split_out:
        out_shape = jax.ShapeDtypeStruct((2, R, N // 2), F32)
        out_spec = pl.BlockSpec((None, TR, TC), lambda i, j: (j // half, i, j % half))
    else:
        out_shape = jax.ShapeDtypeStruct((R, N), F32)
        out_spec = pl.BlockSpec((TR, TC), lambda i, j: (i, j))
    return _pcall(
        body,
        grid=(R // TR, nj),
        in_specs=[
            pl.BlockSpec((TR, D), lambda i, j: (i, 0)),
            pl.BlockSpec((1, D), lambda i, j: (0, 0)),
            pl.BlockSpec((D, TC), lambda i, j: (0, j)),
        ],
        out_specs=out_spec,
        out_shape=out_shape,
        scratch_shapes=[pltpu.VMEM((TR, D), _MXU)],
        compiler_params=_cp(("parallel", "arbitrary")),
        name=name,
    )(h, gain, w.astype(_MXU))


def _qkv_cast(proj_a, name):
    R = proj_a.shape[0]

    def body(p_ref, o_ref):
        col = lax.broadcasted_iota(jnp.int32, (1, TC), 1) + pl.program_id(1) * TC
        sc = jnp.where(col < ATT_W, Q_SCALE, 1.0).astype(F32)
        o_ref[...] = (p_ref[...] * sc).astype(o_ref.dtype)

    return _pcall(
        body,
        grid=(R // TR, 2),
        in_specs=[pl.BlockSpec((TR, TC), lambda i, j: (i, j + 1))],
        out_specs=pl.BlockSpec((TR, TC), lambda i, j: (i, j)),
        out_shape=jax.ShapeDtypeStruct((R, 3 * ATT_W), _MXU),
        compiler_params=_cp(("parallel", "parallel")),
        name=name,
    )(proj_a)


def _sum_mat(w):
    rowi = lax.broadcasted_iota(jnp.int32, (w, w + BLK), 0)
    coli = lax.broadcasted_iota(jnp.int32, (w, w + BLK), 1)
    return jnp.logical_or(rowi > coli, coli >= w).astype(_MXU)


def _hi_lo_rows(x):
    hi = x.astype(_MXU)
    lo = (x - hi.astype(F32)).astype(_MXU)
    return jnp.concatenate([hi, lo], axis=0)


def _fold_sums(r, w):
    r = r[:BLK] + r[BLK:]
    return r[:, :w], r[:, w:]


def _attn_first(q, kc, valid, sm):
    z = _dot_t(q, kc)
    en = jnp.exp(-jnp.abs(z))
    t = jnp.log(1.0 + en)
    lk = jnp.minimum(-z, 0.0) - t
    log_beta = jnp.minimum(z, 0.0) - t
    if valid is not None:
        lk = jnp.where(valid, lk, 0.0)
    return z, en, log_beta, jnp.dot(_hi_lo_rows(lk), sm, preferred_element_type=F32)


def _attn_weights(log_beta, r, valid, s_rep):
    w = log_beta.shape[1]
    after, total = _fold_sums(r, w)
    log_a = log_beta + after + jnp.tile(s_rep, (1, w // BLK))
    if valid is not None:
        log_a = jnp.where(valid, log_a, -1e30)
    return jnp.exp(log_a), total


def _attn_walk(i, first, second, carry):
    ci = jnp.maximum(i - 1, 0) // 2

    def at(c, masked):
        return first(pl.multiple_of(BLK + c * CH, BLK), CH, masked)

    carry = second(at(ci, True), carry)

    def group(t, cr):
        state = at(ci - 1 - ATTN_UNROLL * t, False)
        for u in range(1, ATTN_UNROLL):
            ahead = at(ci - 1 - ATTN_UNROLL * t - u, False)
            cr = second(state, cr)
            state = ahead
        return second(state, cr)

    carry = lax.fori_loop(0, ci // ATTN_UNROLL, group, carry)
    rem = ci % ATTN_UNROLL
    carry = lax.fori_loop(0, rem, lambda t, cr: second(at(rem - 1 - t, False), cr), carry)
    return second(first(0, BLK, True), carry)


def _attn_mask(i, off, w):
    qpos = i * BLK + lax.broadcasted_iota(jnp.int32, (BLK, w), 0)
    kpos = off + lax.broadcasted_iota(jnp.int32, (BLK, w), 1)
    return jnp.logical_and(kpos < qpos, kpos >= PAD_ROWS)


def _attn_fwd(qkv, name):
    R = qkv.shape[0]
    nb = R // BLK

    def body(q_ref, k_ref, v_ref, o_ref):
        i = pl.program_id(1)
        q = q_ref[...]
        sms = {CH: _sum_mat(CH), BLK: _sum_mat(BLK)}

        def first(off, w, masked):
            valid = _attn_mask(i, off, w) if masked else None
            _, _, log_beta, r = _attn_first(q, k_ref[pl.ds(off, w), :], valid, sms[w])
            return off, w, valid, log_beta, r

        def second(state, carry):
            off, w, valid, log_beta, r = state
            acc, s_rep = carry
            a, total = _attn_weights(log_beta, r, valid, s_rep)
            return acc + _dot(a, v_ref[pl.ds(off, w), :]), s_rep + total

        zero = jnp.zeros((BLK, HD), F32)
        acc, _ = _attn_walk(i, first, second, (zero, zero))
        o_ref[...] = acc

    return _pcall(
        body,
        grid=(NH, nb),
        in_specs=[
            pl.BlockSpec((BLK, HD), lambda h, i: (i, h)),
            pl.BlockSpec((R, HD), lambda h, i: (0, NH + h)),
            pl.BlockSpec((R, HD), lambda h, i: (0, 2 * NH + h)),
        ],
        out_specs=pl.BlockSpec((BLK, HD), lambda h, i: (i, h)),
        out_shape=jax.ShapeDtypeStruct((R, ATT_W), F32),
        compiler_params=_cp(("parallel", "arbitrary")),
        name=name,
    )(qkv, qkv, qkv)


def _pool_consts(i):
    lane = lax.broadcasted_iota(jnp.int32, (1, POOL_CH), 1)
    wsize = jnp.where(lane < POOL_GC, 2.0, jnp.where(lane < 2 * POOL_GC, 4.0, jnp.where(lane < 3 * POOL_GC, 8.0, 16.0)))
    return lane, wsize


def _pool_div(rows, wsize):
    pos1 = (rows - (PAD_ROWS - 1)).astype(F32)
    return jnp.clip(pos1, 1.0, wsize)


def _lane_select(lane, s2, s4, s8, s16):
    return jnp.where(lane < POOL_GC, s2, jnp.where(lane < 2 * POOL_GC, s4, jnp.where(lane < 3 * POOL_GC, s8, s16)))


def _branch_pre(proj_a, conv_w, conv_b, name):
    R = proj_a.shape[0]
    nh = TR // HALO

    def body(t_ref, h_ref, w_ref, b_ref, c_ref, p_ref, gext, pext):
        i = pl.program_id(0)
        t = t_ref[...]
        hl = jnp.where(i == 0, 0.0, h_ref[...])
        gext[pl.ds(0, HALO), :] = hl[:, :CONV_CH] * _sig(hl[:, CONV_CH : 2 * CONV_CH])
        gext[pl.ds(HALO, TR), :] = t[:, :CONV_CH] * _sig(t[:, CONV_CH : 2 * CONV_CH])
        acc = jnp.zeros((TR, CONV_CH), F32) + b_ref[...]
        for k in range(CONV_K):
            acc = acc + w_ref[pl.ds(k, 1), :] * gext[pl.ds(HALO - (CONV_K - 1) + k, TR), :]
        c_ref[...] = acc

        p = t[:, 2 * CONV_CH :]
        pext[pl.ds(0, HALO), :] = hl[:, 2 * CONV_CH :]
        pext[pl.ds(HALO, TR), :] = p

        def back(k):
            return pext[pl.ds(HALO - k, TR), :]

        s2 = p + back(1)
        s4 = s2 + back(2) + back(3)
        s8 = s4 + back(4) + back(5) + back(6) + back(7)
        s16 = s8
        for k in range(8, 16):
            s16 = s16 + back(k)
        lane, wsize = _pool_consts(i)
        div = _pool_div(_row_ids(TR, i, (TR, POOL_CH)), wsize)
        p_ref[...] = (_lane_select(lane, s2, s4, s8, s16) / div - p).astype(p_ref.dtype)

    return _pcall(
        body,
        grid=(R // TR,),
        in_specs=[
            pl.BlockSpec((TR, TC), lambda i: (i, 0)),
            pl.BlockSpec((HALO, TC), lambda i: (jnp.maximum(i * nh - 1, 0), 0)),
            pl.BlockSpec((HALO, CONV_CH), lambda i: (0, 0)),
            pl.BlockSpec((1, CONV_CH), lambda i: (0, 0)),
        ],
        out_specs=[pl.BlockSpec((TR, CONV_CH), lambda i: (i, 0)), pl.BlockSpec((TR, POOL_CH), lambda i: (i, 0))],
        out_shape=[jax.ShapeDtypeStruct((R, CONV_CH), F32), jax.ShapeDtypeStruct((R, POOL_CH), _MXU)],
        scratch_shapes=[pltpu.VMEM((TR + HALO, CONV_CH), F32), pltpu.VMEM((TR + HALO, POOL_CH), F32)],
        compiler_params=_cp(("parallel",)),
        name=name,
    )(proj_a, proj_a, conv_w, conv_b)


def _mix_values(c, pooled, att, gates, ln_g, ln_b, w_co, b_co, w_bd, p_scale, w_ao):
    mu = jnp.mean(c, axis=-1, keepdims=True)
    xc = c - mu
    rstd = lax.rsqrt(jnp.mean(xc * xc, axis=-1, keepdims=True) + EPS)
    nl = xc * rstd
    ln = nl * ln_g + ln_b
    sg = _sig(ln)
    s = ln * sg
    ya = _dot(s, w_co) + b_co
    ybr = _dot(pooled, w_bd)
    yb = ybr * p_scale
    yc = _dot(att, w_ao)
    g = _sig(gates)
    g0, g1, g2 = g[:, :D], g[:, D : 2 * D], g[:, 2 * D :]
    mixed = g0 * ya + g1 * yb + g2 * yc
    return dict(rstd=rstd, nl=nl, ln=ln, sg=sg, s=s, ya=ya, ybr=ybr, yb=yb, yc=yc, g0=g0, g1=g1, g2=g2, mixed=mixed)


_MIX_W_SPECS = [
    pl.BlockSpec((1, CONV_CH), lambda i: (0, 0)),
    pl.BlockSpec((1, CONV_CH), lambda i: (0, 0)),
    pl.BlockSpec((CONV_CH, D), lambda i: (0, 0)),
    pl.BlockSpec((1, D), lambda i: (0, 0)),
    pl.BlockSpec((POOL_CH, D), lambda i: (0, 0)),
    pl.BlockSpec((1, D), lambda i: (0, 0)),
    pl.BlockSpec((ATT_W, D), lambda i: (0, 0)),
    pl.BlockSpec((D, D), lambda i: (0, 0)),
]


def _mix_act_specs(t):
    return [
        pl.BlockSpec((t, D_G), lambda i: (i, 0)),
        pl.BlockSpec((t, CONV_CH), lambda i: (i, 0)),
        pl.BlockSpec((t, POOL_CH), lambda i: (i, 0)),
        pl.BlockSpec((t, ATT_W), lambda i: (i, 0)),
    ]


def _mix_fwd(h, gates, c, pooled, att, mw, name):
    R = h.shape[0]

    def body(h_ref, g_ref, c_ref, p_ref, a_ref, lg, lb, wco, bco, wbd, ps, wao, wo, o_ref):
        v = _mix_values(c_ref[...], p_ref[...], a_ref[...], g_ref[...], lg[...], lb[...], wco[...], bco[...],
                        wbd[...], ps[...], wao[...])
        out = h_ref[...] + _dot(v["mixed"], wo[...])
        rows = _row_ids(TB, pl.program_id(0), (TB, D))
        o_ref[...] = jnp.where(rows >= PAD_ROWS, out, 0.0)

    return _pcall(
        body,
        grid=(R // TB,),
        in_specs=[pl.BlockSpec((TB, D), lambda i: (i, 0))] + _mix_act_specs(TB) + _MIX_W_SPECS,
        out_specs=pl.BlockSpec((TB, D), lambda i: (i, 0)),
        out_shape=jax.ShapeDtypeStruct((R, D), F32),
        compiler_params=_cp(("parallel",)),
        name=name,
    )(h, gates, c, pooled, att, *mw)


def _ffn_conv(ut, uh, cw_ref, cb_ref, ext):
    ext[:, pl.ds(0, HALO8), :] = uh
    ext[:, pl.ds(HALO8, TR), :] = ut
    um1 = ext[:, pl.ds(HALO8 - 1, TR), :]
    um2 = ext[:, pl.ds(HALO8 - 2, TR), :]
    cw = cw_ref[...]
    conv = cw[0][:, None, :] * um2 + cw[1][:, None, :] * um1 + cw[2][:, None, :] * ut + cb_ref[...][:, None, :]
    return conv, um1, um2


def _gelu_parts(x):
    th = jnp.tanh(GELU_C0 * (x + GELU_C1 * x * x * x))
    return th, 0.5 * x * (1.0 + th)


def _ffn_in_specs(nrow8, order):
    n8 = TR // HALO8
    return [
        pl.BlockSpec((2, TR, TF), lambda *g: (0, order(*g)[0], order(*g)[1])),
        pl.BlockSpec((2, HALO8, TF), lambda *g: (0, jnp.maximum(order(*g)[0] * n8 - 1, 0), order(*g)[1])),
        pl.BlockSpec((FFN_K, 2, TF), lambda *g: (0, 0, order(*g)[1])),
        pl.BlockSpec((2, TF), lambda *g: (0, order(*g)[1])),
    ]


def _ffn_fwd(u3, h2, cw, cb, w_down, name):
    R = h2.shape[0]
    nj = D_FF // TF

    def body(u_ref, uh_ref, cw_ref, cb_ref, wd_ref, h_ref, o_ref, ext, acc):
        i, j = pl.program_id(0), pl.program_id(1)
        uh = jnp.where(i == 0, 0.0, uh_ref[...])
        conv, _, _ = _ffn_conv(u_ref[...], uh, cw_ref, cb_ref, ext)
        _, a = _gelu_parts(conv[0])
        part = _dot(a * conv[1], wd_ref[...])

        @pl.when(j == 0)
        def _():
            acc[...] = part

        @pl.when(j > 0)
        def _():
            acc[...] += part

        @pl.when(j == nj - 1)
        def _():
            rows = _row_ids(TR, i, (TR, D))
            o_ref[...] = jnp.where(rows >= PAD_ROWS, h_ref[...] + acc[...], 0.0)

    return _pcall(
        body,
        grid=(R // TR, nj),
        in_specs=_ffn_in_specs(R // HALO8, lambda i, j: (i, j))
        + [pl.BlockSpec((TF, D), lambda i, j: (j, 0)), pl.BlockSpec((TR, D), lambda i, j: (i, 0))],
        out_specs=pl.BlockSpec((TR, D), lambda i, j: (i, 0)),
        out_shape=jax.ShapeDtypeStruct((R, D), F32),
        scratch_shapes=[pltpu.VMEM((2, TR + HALO8, TF), F32), pltpu.VMEM((TR, D), F32)],
        compiler_params=_cp(("parallel", "arbitrary")),
        name=name,
    )(u3, u3, cw, cb, w_down.astype(_MXU), h2)


def _loss_bwd(h, target, gain, name):
    R = h.shape[0]

    def body(h_ref, t_ref, g_ref, dh_ref, loss_ref, dg_ref):
        i = pl.program_id(0)

        @pl.when(i == 0)
        def _():
            loss_ref[...] = jnp.zeros_like(loss_ref)
            dg_ref[...] = jnp.zeros_like(dg_ref)
            dh_ref[...] = jnp.zeros_like(dh_ref)

        @pl.when(i > 0)
        def _():
            g = g_ref[...]
            r, n, y = _rms_fwd(h_ref[...], g)
            e = y - t_ref[...]
            loss_ref[...] += (0.5 / D) * jnp.sum(jnp.sum(e * e, axis=1, keepdims=True), axis=0, keepdims=True)
            dx, dgain = _rms_bwd(e * (1.0 / D), r, n, g)
            dg_ref[...] += dgain
            dh_ref[...] = dx

    return _pcall(
        body,
        grid=(R // BLK,),
        in_specs=[
            pl.BlockSpec((BLK, D), lambda i: (i, 0)),
            pl.BlockSpec((BLK, D), lambda i: (jnp.maximum(i - 1, 0), 0)),
            pl.BlockSpec((1, D), lambda i: (0, 0)),
        ],
        out_specs=[
            pl.BlockSpec((BLK, D), lambda i: (i, 0)),
            pl.BlockSpec((1, 1), lambda i: (0, 0)),
            pl.BlockSpec((1, D), lambda i: (0, 0)),
        ],
        out_shape=[
            jax.ShapeDtypeStruct((R, D), F32),
            jax.ShapeDtypeStruct((1, 1), F32),
            jax.ShapeDtypeStruct((1, D), F32),
        ],
        compiler_params=_cp(("arbitrary",)),
        name=name,
    )(h, target, gain)


def _tdot_acc(a, b, name):
    R, M = a.shape
    split = b.ndim == 3
    N = 2 * b.shape[2] if split else b.shape[1]
    tn = b.shape[2] if split else N
    tm = min(M, 512)
    tk = TR

    def body(a_ref, b_ref, o_ref):
        part = _tdot(a_ref[...], b_ref[...])

        @pl.when(pl.program_id(2) == 0)
        def _():
            o_ref[...] = part

        @pl.when(pl.program_id(2) > 0)
        def _():
            o_ref[...] += part

    if split:
        b_spec = pl.BlockSpec((None, tk, tn), lambda m, n, k: (n, k, 0))
    else:
        b_spec = pl.BlockSpec((tk, tn), lambda m, n, k: (k, n))
    return _pcall(
        body,
        grid=(M // tm, N // tn, R // tk),
        in_specs=[pl.BlockSpec((tk, tm), lambda m, n, k: (k, m)), b_spec],
        out_specs=pl.BlockSpec((tm, tn), lambda m, n, k: (m, n)),
        out_shape=jax.ShapeDtypeStruct((M, N), F32),
        compiler_params=_cp(("parallel", "parallel", "arbitrary")),
        name=name,
    )(a, b)


def _ffn_bwd1(dh3, u3, cw, cb, w_down, name):
    R = dh3.shape[0]
    nj = D_FF // TF

    def body(u_ref, uh_ref, cw_ref, cb_ref, wd_ref, dh_ref, dc_ref, act_ref, dwb_ref, ext):
        j, i = pl.program_id(0), pl.program_id(1)
        ut = u_ref[...]
        uh = jnp.where(i == 0, 0.0, uh_ref[...])
        conv, um1, um2 = _ffn_conv(ut, uh, cw_ref, cb_ref, ext)
        gt, val = conv[0], conv[1]
        th, a = _gelu_parts(gt)
        dact = _dot_t(dh_ref[...], wd_ref[...])
        dgelu = 0.5 * (1.0 + th) + 0.5 * gt * (1.0 - th * th) * (GELU_C0 * (1.0 + 3.0 * GELU_C1 * gt * gt))
        dgt = dact * val * dgelu
        dval = dact * a
        act_ref[...] = (a * val).astype(act_ref.dtype)
        dc_ref[0] = dgt
        dc_ref[1] = dval

        @pl.when(i == 0)
        def _():
            dwb_ref[...] = jnp.zeros_like(dwb_ref)

        for half, dcv in ((0, dgt), (1, dval)):
            dwb_ref[half, pl.ds(0, 1), :] += jnp.sum(um2[half] * dcv, axis=0, keepdims=True)
            dwb_ref[half, pl.ds(1, 1), :] += jnp.sum(um1[half] * dcv, axis=0, keepdims=True)
            dwb_ref[half, pl.ds(2, 1), :] += jnp.sum(ut[half] * dcv, axis=0, keepdims=True)
            dwb_ref[half, pl.ds(3, 1), :] += jnp.sum(dcv, axis=0, keepdims=True)

    return _pcall(
        body,
        grid=(nj, R // TR),
        in_specs=_ffn_in_specs(R // HALO8, lambda j, i: (i, j))
        + [pl.BlockSpec((TF, D), lambda j, i: (j, 0)), pl.BlockSpec((TR, D), lambda j, i: (i, 0))],
        out_specs=[
            pl.BlockSpec((2, TR, TF), lambda j, i: (0, i, j)),
            pl.BlockSpec((TR, TF), lambda j, i: (i, j)),
            pl.BlockSpec((2, 8, TF), lambda j, i: (0, 0, j)),
        ],
        out_shape=[
            jax.ShapeDtypeStruct((2, R, D_FF), F32),
            jax.ShapeDtypeStruct((R, D_FF), _MXU),
            jax.ShapeDtypeStruct((2, 8, D_FF), F32),
        ],
        scratch_shapes=[pltpu.VMEM((2, TR + HALO8, TF), F32)],
        compiler_params=_cp(("parallel", "arbitrary")),
        name=name,
    )(u3, u3, cw, cb, w_down.astype(_MXU), dh3)


def _ffn_bwd2(dc3, cw, w_up, h2, dh3, gain, name):
    R = h2.shape[0]
    nj = D_FF // TF
    n8 = TR // HALO8
    last8 = R // HALO8 - 1
    ni = R // TR

    def body(dc_ref, dn_ref, cw_ref, wg_ref, wv_ref, h_ref, dh_ref, g_ref, du_ref, o_ref, xn_ref, dg_ref, ext, acc):
        i, j = pl.program_id(0), pl.program_id(1)
        dc = dc_ref[...]
        ext[:, pl.ds(0, TR), :] = dc
        ext[:, pl.ds(TR, HALO8), :] = jnp.where(i == ni - 1, 0.0, dn_ref[...])
        cw = cw_ref[...]
        du = (cw[2][:, None, :] * dc + cw[1][:, None, :] * ext[:, pl.ds(1, TR), :]
              + cw[0][:, None, :] * ext[:, pl.ds(2, TR), :])
        du_ref[...] = du.astype(du_ref.dtype)
        part = _dot_t(du[0], wg_ref[...]) + _dot_t(du[1], wv_ref[...])

        @pl.when(j == 0)
        def _():
            acc[...] = part

        @pl.when(j > 0)
        def _():
            acc[...] += part

        @pl.when(jnp.logical_and(i == 0, j == 0))
        def _():
            dg_ref[...] = jnp.zeros_like(dg_ref)

        @pl.when(j == nj - 1)
        def _():
            g = g_ref[...]
            r, n, y = _rms_fwd(h_ref[...], g)
            dx, dgain = _rms_bwd(acc[...], r, n, g)
            dg_ref[...] += dgain
            rows = _row_ids(TR, i, (TR, D))
            o_ref[...] = jnp.where(rows >= PAD_ROWS, dh_ref[...] + dx, 0.0)
            xn_ref[...] = y.astype(xn_ref.dtype)

    w_up = w_up.astype(_MXU)
    return _pcall(
        body,
        grid=(ni, nj),
        in_specs=[
            pl.BlockSpec((2, TR, TF), lambda i, j: (0, i, j)),
            pl.BlockSpec((2, HALO8, TF), lambda i, j: (0, jnp.minimum((i + 1) * n8, last8), j)),
            pl.BlockSpec((FFN_K, 2, TF), lambda i, j: (0, 0, j)),
            pl.BlockSpec((D, TF), lambda i, j: (0, j)),
            pl.BlockSpec((D, TF), lambda i, j: (0, nj + j)),
            pl.BlockSpec((TR, D), lambda i, j: (i, 0)),
            pl.BlockSpec((TR, D), lambda i, j: (i, 0)),
            pl.BlockSpec((1, D), lambda i, j: (0, 0)),
        ],
        out_specs=[
            pl.BlockSpec((2, TR, TF), lambda i, j: (0, i, j)),
            pl.BlockSpec((TR, D), lambda i, j: (i, 0)),
            pl.BlockSpec((TR, D), lambda i, j: (i, 0)),
            pl.BlockSpec((1, D), lambda i, j: (0, 0)),
        ],
        out_shape=[
            jax.ShapeDtypeStruct((2, R, D_FF), _MXU),
            jax.ShapeDtypeStruct((R, D), F32),
            jax.ShapeDtypeStruct((R, D), _MXU),
            jax.ShapeDtypeStruct((1, D), F32),
        ],
        scratch_shapes=[pltpu.VMEM((2, TR + HALO8, TF), F32), pltpu.VMEM((TR, D), F32)],
        compiler_params=_cp(("arbitrary", "arbitrary")),
        name=name,
    )(dc3, dc3, cw, w_up, w_up, h2, dh3, gain)


def _mix_bwd(dh2, gates, c, pooled, att, mw, name):
    R = dh2.shape[0]

    def body(dh_ref, g_ref, c_ref, p_ref, a_ref, lg, lb, wco, bco, wbd, ps, wao, wo,
             dg_ref, dc_ref, dp_ref, da_ref, mx_ref, s_ref, dya_ref, dyb_ref, dyc_ref, accd_ref, accc_ref):
        v = _mix_values(c_ref[...], p_ref[...], a_ref[...], g_ref[...], lg[...], lb[...], wco[...], bco[...],
                        wbd[...], ps[...], wao[...])
        dmix = _dot_t(dh_ref[...], wo[...])
        for k, (gk, yk) in enumerate(((v["g0"], v["ya"]), (v["g1"], v["yb"]), (v["g2"], v["yc"]))):
            dg_ref[:, k * D : (k + 1) * D] = dmix * yk * gk * (1.0 - gk)
        dya = dmix * v["g0"]
        dyb = dmix * v["g1"]
        dyc = dmix * v["g2"]
        ds = _dot_t(dya, wco[...])
        ln, sg, nl = v["ln"], v["sg"], v["nl"]
        dln = ds * (sg * (1.0 + ln * (1.0 - sg)))
        dn = dln * lg[...]
        dc = v["rstd"] * (dn - jnp.mean(dn, axis=-1, keepdims=True) - nl * jnp.mean(dn * nl, axis=-1, keepdims=True))
        dybs = dyb * ps[...]
        dc_ref[...] = dc
        dp_ref[...] = _dot_t(dybs, wbd[...])
        da_ref[...] = _dot_t(dyc, wao[...])
        mx_ref[...] = v["mixed"].astype(mx_ref.dtype)
        s_ref[...] = v["s"].astype(s_ref.dtype)
        dya_ref[...] = dya.astype(dya_ref.dtype)
        dyb_ref[...] = dybs.astype(dyb_ref.dtype)
        dyc_ref[...] = dyc.astype(dyc_ref.dtype)

        @pl.when(pl.program_id(0) == 0)
        def _():
            accd_ref[...] = jnp.zeros_like(accd_ref)
            accc_ref[...] = jnp.zeros_like(accc_ref)

        accd_ref[pl.ds(0, 1), :] += jnp.sum(dya, axis=0, keepdims=True)
        accd_ref[pl.ds(1, 1), :] += jnp.sum(dyb * v["ybr"], axis=0, keepdims=True)
        accc_ref[pl.ds(0, 1), :] += jnp.sum(dln * nl, axis=0, keepdims=True)
        accc_ref[pl.ds(1, 1), :] += jnp.sum(dln, axis=0, keepdims=True)
        accc_ref[pl.ds(2, 1), :] += jnp.sum(dc, axis=0, keepdims=True)

    def row(w):
        return pl.BlockSpec((TB, w), lambda i: (i, 0))

    return _pcall(
        body,
        grid=(R // TB,),
        in_specs=[row(D)] + _mix_act_specs(TB) + _MIX_W_SPECS,
        out_specs=[row(D_G), row(CONV_CH), row(POOL_CH), row(ATT_W), row(D), row(CONV_CH), row(D), row(D), row(D),
                   pl.BlockSpec((8, D), lambda i: (0, 0)), pl.BlockSpec((8, CONV_CH), lambda i: (0, 0))],
        out_shape=[
            jax.ShapeDtypeStruct((R, D_G), F32),
            jax.ShapeDtypeStruct((R, CONV_CH), F32),
            jax.ShapeDtypeStruct((R, POOL_CH), F32),
            jax.ShapeDtypeStruct((R, ATT_W), F32),
            jax.ShapeDtypeStruct((R, D), _MXU),
            jax.ShapeDtypeStruct((R, CONV_CH), _MXU),
            jax.ShapeDtypeStruct((R, D), _MXU),
            jax.ShapeDtypeStruct((R, D), _MXU),
            jax.ShapeDtypeStruct((R, D), _MXU),
            jax.ShapeDtypeStruct((8, D), F32),
            jax.ShapeDtypeStruct((8, CONV_CH), F32),
        ],
        compiler_params=_cp(("arbitrary",)),
        name=name,
    )(dh2, gates, c, pooled, att, *mw)


def _attn_bwd(qkv, att, datt, name):
    R = qkv.shape[0]
    nb = R // BLK

    def body(q_ref, k_ref, v_ref, o_ref, do_ref, dq_ref, dk_ref, dv_ref):
        i = pl.program_id(1)

        @pl.when(i == 0)
        def _():
            dk_ref[...] = jnp.zeros_like(dk_ref)
            dv_ref[...] = jnp.zeros_like(dv_ref)

        q = q_ref[...]
        dob = do_ref[...].astype(_MXU)
        dof = dob.astype(F32)
        sms = {CH: _sum_mat(CH), BLK: _sum_mat(BLK)}
        e_all = jnp.broadcast_to(jnp.sum(dof * o_ref[...], axis=1, keepdims=True), (BLK, BLK))

        def first(off, w, masked):
            valid = _attn_mask(i, off, w) if masked else None
            kc = k_ref[pl.ds(off, w), :]
            z, en, log_beta, r = _attn_first(q, kc, valid, sms[w])
            inv = 1.0 / (1.0 + en)
            beta = jnp.where(z >= 0.0, inv, en * inv)
            return off, w, valid, log_beta, r, beta, _dot_t(dob, v_ref[pl.ds(off, w), :])

        def second(state, carry):
            off, w, valid, log_beta, r, beta, da = state
            dq, s_rep, e_done = carry
            a, total = _attn_weights(log_beta, r, valid, s_rep)
            ab = a.astype(_MXU)
            e = ab.astype(F32) * da
            e_after, e_total = _fold_sums(jnp.dot(_hi_lo_rows(e), sms[w], preferred_element_type=F32), w)
            before = jnp.tile(e_all - e_done, (1, w // BLK)) - (e + e_after)
            dz = e * (1.0 - beta) - beta * before
            if valid is not None:
                dz = jnp.where(valid, dz, 0.0)
            dzb = dz.astype(_MXU)
            dk_ref[pl.ds(off, w), :] += _tdot(dzb, q)
            dv_ref[pl.ds(off, w), :] += _tdot(ab, dob)
            dq = dq + jnp.dot(dzb, k_ref[pl.ds(off, w), :], preferred_element_type=F32)
            return dq, s_rep + total, e_done + e_total

        zero = jnp.zeros((BLK, HD), F32)
        dq, _, _ = _attn_walk(i, first, second, (zero, zero, zero))
        dq_ref[...] = dq * Q_SCALE

    blk = pl.BlockSpec((BLK, HD), lambda h, i: (i, h))
    col = pl.BlockSpec((R, HD), lambda h, i: (0, h))
    return _pcall(
        body,
        grid=(NH, nb),
        in_specs=[
            blk,
            pl.BlockSpec((R, HD), lambda h, i: (0, NH + h)),
            pl.BlockSpec((R, HD), lambda h, i: (0, 2 * NH + h)),
            blk,
            blk,
        ],
        out_specs=[blk, col, col],
        out_shape=[jax.ShapeDtypeStruct((R, ATT_W), F32)] * 3,
        compiler_params=pltpu.CompilerParams(dimension_semantics=("arbitrary", "arbitrary"),
                                             vmem_limit_bytes=56 * 1024 * 1024),
        name=name,
    )(qkv, qkv, qkv, att, datt)


def _branch_bwd(dc, dpooled, proj_a, conv_w, name):
    R = dc.shape[0]
    nh = TR // HALO
    last = R // HALO - 1
    ni = R // TR

    def body(dc_ref, dcn_ref, dp_ref, dpn_ref, t_ref, h_ref, w_ref, o_ref, dcw_ref, gext, dcext, eext):
        i = pl.program_id(0)
        is_last = i == ni - 1
        t = t_ref[...]
        hl = jnp.where(i == 0, 0.0, h_ref[...])
        a = t[:, :CONV_CH]
        sg = _sig(t[:, CONV_CH : 2 * CONV_CH])
        gext[pl.ds(0, HALO), :] = hl[:, :CONV_CH] * _sig(hl[:, CONV_CH : 2 * CONV_CH])
        gext[pl.ds(HALO, TR), :] = a * sg
        dct = dc_ref[...]
        dcext[pl.ds(0, TR), :] = dct
        dcext[pl.ds(TR, HALO), :] = jnp.where(is_last, 0.0, dcn_ref[...])

        @pl.when(i == 0)
        def _():
            dcw_ref[...] = jnp.zeros_like(dcw_ref)

        dglu = jnp.zeros((TR, CONV_CH), F32)
        for k in range(CONV_K):
            dglu = dglu + w_ref[pl.ds(k, 1), :] * dcext[pl.ds(CONV_K - 1 - k, TR), :]
            dcw_ref[pl.ds(k, 1), :] += jnp.sum(gext[pl.ds(HALO - (CONV_K - 1) + k, TR), :] * dct, axis=0, keepdims=True)
        o_ref[:, :CONV_CH] = dglu * sg
        o_ref[:, CONV_CH : 2 * CONV_CH] = dglu * a * sg * (1.0 - sg)

        lane, wsize = _pool_consts(i)
        dpt = dp_ref[...]
        eext[pl.ds(0, TR), :] = dpt / _pool_div(_row_ids(TR, i, (TR, POOL_CH)), wsize)
        nxt = dpn_ref[...] / _pool_div(_row_ids(TR, i + 1, (HALO, POOL_CH)), wsize)
        eext[pl.ds(TR, HALO), :] = jnp.where(is_last, 0.0, nxt)

        def fwd(k):
            return eext[pl.ds(k, TR), :]

        s2 = fwd(0) + fwd(1)
        s4 = s2 + fwd(2) + fwd(3)
        s8 = s4 + fwd(4) + fwd(5) + fwd(6) + fwd(7)
        s16 = s8
        for k in range(8, 16):
            s16 = s16 + fwd(k)
        o_ref[:, 2 * CONV_CH :] = _lane_select(lane, s2, s4, s8, s16) - dpt

    def nxt_spec(w):
        return pl.BlockSpec((HALO, w), lambda i: (jnp.minimum((i + 1) * nh, last), 0))

    return _pcall(
        body,
        grid=(ni,),
        in_specs=[
            pl.BlockSpec((TR, CONV_CH), lambda i: (i, 0)),
            nxt_spec(CONV_CH),
            pl.BlockSpec((TR, POOL_CH), lambda i: (i, 0)),
            nxt_spec(POOL_CH),
            pl.BlockSpec((TR, TC), lambda i: (i, 0)),
            pl.BlockSpec((HALO, TC), lambda i: (jnp.maximum(i * nh - 1, 0), 0)),
            pl.BlockSpec((HALO, CONV_CH), lambda i: (0, 0)),
        ],
        out_specs=[pl.BlockSpec((TR, TC), lambda i: (i, 0)), pl.BlockSpec((HALO, CONV_CH), lambda i: (0, 0))],
        out_shape=[jax.ShapeDtypeStruct((R, TC), F32), jax.ShapeDtypeStruct((HALO, CONV_CH), F32)],
        scratch_shapes=[
            pltpu.VMEM((TR + HALO, CONV_CH), F32),
            pltpu.VMEM((TR + HALO, CONV_CH), F32),
            pltpu.VMEM((TR + HALO, POOL_CH), F32),
        ],
        compiler_params=_cp(("arbitrary",)),
        name=name,
    )(dc, dc, dpooled, dpooled, proj_a, proj_a, conv_w)


def _in_bwd(dfront, dq, dk, dv, dgates, w_a, w_g, h, dh2, gain, name):
    R = h.shape[0]
    nj = 1 + D_G // TC
    ni = R // TR
    w_a = w_a.astype(_MXU)
    w_f, w_q = w_a[:, :TC], w_a[:, TC : TC + ATT_W]
    w_k, w_v = w_a[:, TC + ATT_W : TC + 2 * ATT_W], w_a[:, TC + 2 * ATT_W :]

    def body(df_ref, dq_ref, dk_ref, dv_ref, dg_ref, wf_ref, wq_ref, wk_ref, wv_ref, wg_ref, h_ref, dh_ref, g_ref,
             o_ref, hn_ref, dgain_ref, acc):
        i, j = pl.program_id(0), pl.program_id(1)

        @pl.when(j == 0)
        def _():
            acc[...] = (_dot_t(df_ref[...], wf_ref[...]) + _dot_t(dq_ref[...], wq_ref[...])
                        + _dot_t(dk_ref[...], wk_ref[...]) + _dot_t(dv_ref[...], wv_ref[...]))

        @pl.when(j > 0)
        def _():
            acc[...] += _dot_t(dg_ref[...], wg_ref[...])

        @pl.when(jnp.logical_and(i == 0, j == 0))
        def _():
            dgain_ref[...] = jnp.zeros_like(dgain_ref)

        @pl.when(j == nj - 1)
        def _():
            g = g_ref[...]
            r, n, y = _rms_fwd(h_ref[...], g)
            dx, dgain = _rms_bwd(acc[...], r, n, g)
            dgain_ref[...] += dgain
            rows = _row_ids(TR, i, (TR, D))
            o_ref[...] = jnp.where(rows >= PAD_ROWS, dh_ref[...] + dx, 0.0)
            hn_ref[...] = y.astype(hn_ref.dtype)

    def row(w):
        return pl.BlockSpec((TR, w), lambda i, j: (i, 0))

    def whole(w):
        return pl.BlockSpec((D, w), lambda i, j: (0, 0))

    def gcol(i, j):
        return jnp.maximum(j - 1, 0)

    return _pcall(
        body,
        grid=(ni, nj),
        in_specs=[
            row(TC), row(ATT_W), row(ATT_W), row(ATT_W),
            pl.BlockSpec((TR, TC), lambda i, j: (i, gcol(i, j))),
            whole(TC), whole(ATT_W), whole(ATT_W), whole(ATT_W),
            pl.BlockSpec((D, TC), lambda i, j: (0, gcol(i, j))),
            row(D), row(D),
            pl.BlockSpec((1, D), lambda i, j: (0, 0)),
        ],
        out_specs=[row(D), row(D), pl.BlockSpec((1, D), lambda i, j: (0, 0))],
        out_shape=[jax.ShapeDtypeStruct((R, D), F32), jax.ShapeDtypeStruct((R, D), _MXU), jax.ShapeDtypeStruct((1, D), F32)],
        scratch_shapes=[pltpu.VMEM((TR, D), F32)],
        compiler_params=_cp(("arbitrary", "arbitrary")),
        name=name,
    )(dfront, dq, dk, dv, dgates, w_f, w_q, w_k, w_v, w_g.astype(_MXU), h, dh2, gain)


def _pool_blockdiag(w_grp):
    eye = jnp.eye(len(POOL_WINDOWS), dtype=w_grp.dtype)
    return jnp.einsum("gcd,gh->gchd", w_grp, eye).reshape(POOL_CH, D)


def _pool_blockdiag_grad(dw_bd):
    d4 = dw_bd.reshape(len(POOL_WINDOWS), POOL_GC, len(POOL_WINDOWS), D // len(POOL_WINDOWS))
    return jnp.stack([d4[g, :, g, :] for g in range(len(POOL_WINDOWS))])


def _local_step(x, target, p):
    row = lambda a: a.reshape(1, -1)
    h = jnp.concatenate([jnp.zeros((PAD_ROWS, D), F32), p["meta"], x], axis=0)
    saved = []
    for l in range(N_LAYERS):
        w_a, w_g = p["w_in"][l][:, :D_A], p["w_in"][l][:, D_A:]
        conv_w = jnp.concatenate([p["conv_dw_w"][l], jnp.zeros((1, CONV_CH), F32)], axis=0)
        cw3 = p["ffn_dw_w"][l].reshape(FFN_K, 2, D_FF)
        cb2 = p["ffn_dw_b"][l].reshape(2, D_FF)
        mw = (row(p["conv_ln_g"][l]), row(p["conv_ln_b"][l]), p["w_conv_out"][l].astype(_MXU), row(p["b_conv_out"][l]),
              _pool_blockdiag(p["w_pool_grp"][l]).astype(_MXU), row(p["pool_scale"][l]),
              p["w_attn_out"][l].astype(_MXU), p["w_o"][l].astype(_MXU))
        proj_a = _rms_matmul(h, row(p["norm1"][l]), w_a, f"proj_a{l}")
        gates = _rms_matmul(h, row(p["norm1"][l]), w_g, f"proj_g{l}")
        qkv = _qkv_cast(proj_a, f"qkv_cast{l}")
        c, pooled = _branch_pre(proj_a, conv_w, row(p["conv_dw_b"][l]), f"branch_pre{l}")
        att = _attn_fwd(qkv, f"attn_fwd{l}")
        h2 = _mix_fwd(h, gates, c, pooled, att, mw, f"mix_fwd{l}")
        u3 = _rms_matmul(h2, row(p["norm2"][l]), p["w_up"][l], f"ffn_up{l}", split_out=True)
        h3 = _ffn_fwd(u3, h2, cw3, cb2, p["w_down"][l], f"ffn_fwd{l}")
        saved.append(dict(h=h, w_a=w_a, w_g=w_g, conv_w=conv_w, cw3=cw3, cb2=cb2, mw=mw, proj_a=proj_a, gates=gates,
                          qkv=qkv, c=c, pooled=pooled, att=att, h2=h2, u3=u3))
        h = h3

    dh, loss, d_final = _loss_bwd(h, target, row(p["final_norm"]), "loss_bwd")

    g = {k: [None] * N_LAYERS for k in ("norm1", "w_in", "conv_dw_w", "conv_dw_b", "conv_ln_g", "conv_ln_b", "w_conv_out",
                                        "b_conv_out", "w_pool_grp", "pool_scale", "w_attn_out", "w_o", "norm2", "w_up",
                                        "ffn_dw_w", "ffn_dw_b", "w_down")}
    for l in reversed(range(N_LAYERS)):
        s = saved[l]
        dc3, act, dwb = _ffn_bwd1(dh, s["u3"], s["cw3"], s["cb2"], p["w_down"][l], f"ffn_bwd1_{l}")
        g["w_down"][l] = _tdot_acc(act, dh, f"dw_down{l}")
        g["ffn_dw_w"][l] = jnp.transpose(dwb[:, :FFN_K, :], (1, 0, 2)).reshape(FFN_K, 2 * D_FF)
        g["ffn_dw_b"][l] = dwb[:, FFN_K, :].reshape(2 * D_FF)
        du3, dh2, xn2, dg2 = _ffn_bwd2(dc3, s["cw3"], p["w_up"][l], s["h2"], dh, row(p["norm2"][l]), f"ffn_bwd2_{l}")
        g["norm2"][l] = dg2[0]
        g["w_up"][l] = _tdot_acc(xn2, du3, f"dw_up{l}")
        (dgates, dc, dpooled, datt, mixed, s_act, dya, dybs, dyc, acc_d, acc_c) = _mix_bwd(
            dh2, s["gates"], s["c"], s["pooled"], s["att"], s["mw"], f"mix_bwd{l}")
        g["w_o"][l] = _tdot_acc(mixed, dh2, f"dw_o{l}")
        g["w_conv_out"][l] = _tdot_acc(s_act, dya, f"dw_conv_out{l}")
        g["w_pool_grp"][l] = _pool_blockdiag_grad(_tdot_acc(s["pooled"], dybs, f"dw_pool{l}"))
        g["w_attn_out"][l] = _tdot_acc(s["att"], dyc, f"dw_attn_out{l}")
        g["b_conv_out"][l] = acc_d[0]
        g["pool_scale"][l] = acc_d[1]
        g["conv_ln_g"][l] = acc_c[0]
        g["conv_ln_b"][l] = acc_c[1]
        g["conv_dw_b"][l] = acc_c[2]
        dq, dk, dv = _attn_bwd(s["qkv"], s["att"], datt, f"attn_bwd{l}")
        dfront, dcw = _branch_bwd(dc, dpooled, s["proj_a"], s["conv_w"], f"branch_bwd{l}")
        g["conv_dw_w"][l] = dcw[:CONV_K]
        dh, hn, dg1 = _in_bwd(dfront, dq, dk, dv, dgates, s["w_a"], s["w_g"], s["h"], dh2, row(p["norm1"][l]),
                              f"in_bwd{l}")
        g["norm1"][l] = dg1[0]
        g["w_in"][l] = jnp.concatenate(
            [_tdot_acc(hn, part, f"dw_in{l}_{k}") for k, part in enumerate((dfront, dq, dk, dv, dgates))], axis=1)

    grads = {k: jnp.stack(v) for k, v in g.items()}
    grads["final_norm"] = d_final[0]
    grads["meta"] = dh[PAD_ROWS:BLK]
    return loss, dh[BLK:], grads


def _axes():
    return lax.axis_index("x"), lax.axis_index("y"), lax.axis_index("c")


def _allgather_xy(blobs, name):
    nt = len(blobs)
    flips = ((1, 0), (0, 1), (1, 1))

    def body(*refs):
        srcs, dsts = refs[:nt], refs[nt : 2 * nt]
        send_sems, recv_sems, local_sems = refs[2 * nt :]
        x, y, c = _axes()
        mine = 2 * x + y
        started = []
        for t in range(nt):
            own = pltpu.make_async_copy(srcs[t], dsts[t].at[mine], local_sems.at[t])
            own.start()
            started.append(own)
        for t in range(nt):
            for f, (fx, fy) in enumerate(flips):
                px = 1 - x if fx else x
                py = 1 - y if fy else y
                out = pltpu.make_async_remote_copy(srcs[t], dsts[t].at[mine], send_sems.at[t, f], recv_sems.at[t, f],
                                                   device_id=(px, py, c), device_id_type=MESH)
                out.start()
                arriving = pltpu.make_async_remote_copy(srcs[t], dsts[t].at[2 * px + py], send_sems.at[t, f],
                                                        recv_sems.at[t, f], device_id=(px, py, c), device_id_type=MESH)
                started.append((out, arriving))
        for cp in started:
            if isinstance(cp, tuple):
                cp[0].wait_send()
                cp[1].wait_recv()
            else:
                cp.wait()

    return _pcall(
        body,
        in_specs=[ANY] * nt,
        out_specs=[ANY] * nt,
        out_shape=[jax.ShapeDtypeStruct((4,) + b.shape, b.dtype) for b in blobs],
        scratch_shapes=[pltpu.SemaphoreType.DMA((nt, 3)), pltpu.SemaphoreType.DMA((nt, 3)), pltpu.SemaphoreType.DMA((nt,))],
        compiler_params=pltpu.CompilerParams(has_side_effects=True),
        name=name,
    )(*blobs)


def _grad_exchange(gblob, name):
    shape = gblob.shape[1:]

    def body(src, dst, send_sems, recv_sems, local_sem):
        x, y, c = _axes()
        sibling = (x, y, 1 - c)
        chips = [(1 - x, y), (x, 1 - y), (1 - x, 1 - y)]

        def slot(px, py, pc):
            return dst.at[4 * px + 2 * py + pc]

        def copy(k, source, target_slot, to):
            return pltpu.make_async_remote_copy(source, target_slot, send_sems.at[k], recv_sems.at[k], device_id=to,
                                                device_id_type=MESH)

        mine = src.at[2 * x + y]
        own = pltpu.make_async_copy(mine, slot(x, y, c), local_sem)
        own.start()
        first = [copy(0, mine, slot(x, y, c), sibling)]
        first += [copy(1 + j, src.at[2 * px + py], slot(x, y, c), (px, py, c)) for j, (px, py) in enumerate(chips)]
        for cp in first:
            cp.start()
        passed = [copy(4 + j, slot(px, py, c), slot(px, py, c), sibling) for j, (px, py) in enumerate(chips)]
        for j, (px, py) in enumerate(chips):
            copy(1 + j, mine, slot(px, py, c), (px, py, c)).wait_recv()
            passed[j].start()
        copy(0, mine, slot(x, y, 1 - c), sibling).wait_recv()
        for j, (px, py) in enumerate(chips):
            copy(4 + j, mine, slot(px, py, 1 - c), sibling).wait_recv()
        for cp in first + passed:
            cp.wait_send()
        own.wait()

    return _pcall(
        body,
        in_specs=[ANY],
        out_specs=ANY,
        out_shape=jax.ShapeDtypeStruct((8,) + shape, gblob.dtype),
        scratch_shapes=[pltpu.SemaphoreType.DMA((7,)), pltpu.SemaphoreType.DMA((7,)), pltpu.SemaphoreType.DMA],
        compiler_params=pltpu.CompilerParams(has_side_effects=True),
        name=name,
    )(gblob)


def _adamw(w, m, v, gslots, name):
    rows, lanes = w.shape

    def body(w_ref, m_ref, v_ref, gs_ref, g_ref, d_ref, nm_ref, nv_ref):
        g = gs_ref[0]
        for k in range(1, 8):
            g = g + gs_ref[k]
        w_, m_, v_ = w_ref[...], m_ref[...], v_ref[...]
        m_new = ADAM_B1 * m_ + (1.0 - ADAM_B1) * g
        v_new = ADAM_B2 * v_ + (1.0 - ADAM_B2) * (g * g)
        m_hat = m_new / (1.0 - ADAM_B1 ** ADAM_STEP)
        v_hat = v_new / (1.0 - ADAM_B2 ** ADAM_STEP)
        g_ref[...] = g
        d_ref[...] = -ADAM_LR * (m_hat / (jnp.sqrt(v_hat) + ADAM_EPS) + ADAM_WD * w_)
        nm_ref[...] = m_new
        nv_ref[...] = v_new

    blk = pl.BlockSpec((ADAM_ROWS, lanes), lambda i: (i, 0))
    return _pcall(
        body,
        grid=(rows // ADAM_ROWS,),
        in_specs=[blk, blk, blk, pl.BlockSpec((8, ADAM_ROWS, lanes), lambda i: (0, i, 0))],
        out_specs=[blk] * 4,
        out_shape=[jax.ShapeDtypeStruct((rows, lanes), F32)] * 4,
        compiler_params=_cp(("parallel",)),
        name=name,
    )(w, m, v, gslots)


_PARAMS = (
    ("meta", (N_META, D), 1),
    ("norm1", (N_LAYERS, D), None),
    ("w_in", (N_LAYERS, D, D_A + D_G), 2),
    ("conv_dw_w", (N_LAYERS, CONV_K, CONV_CH), 2),
    ("conv_dw_b", (N_LAYERS, CONV_CH), None),
    ("conv_ln_g", (N_LAYERS, CONV_CH), None),
    ("conv_ln_b", (N_LAYERS, CONV_CH), None),
    ("w_conv_out", (N_LAYERS, CONV_CH, D), 2),
    ("b_conv_out", (N_LAYERS, D), None),
    ("w_pool_grp", (N_LAYERS, len(POOL_WINDOWS), POOL_GC, D // len(POOL_WINDOWS)), 3),
    ("pool_scale", (N_LAYERS, D), None),
    ("w_attn_out", (N_LAYERS, ATT_W, D), 2),
    ("w_o", (N_LAYERS, D, D), 1),
    ("norm2", (N_LAYERS, D), None),
    ("w_up", (N_LAYERS, D, 2 * D_FF), 2),
    ("ffn_dw_w", (N_LAYERS, FFN_K, 2 * D_FF), 2),
    ("ffn_dw_b", (N_LAYERS, 2 * D_FF), None),
    ("w_down", (N_LAYERS, D_FF, D), 1),
    ("final_norm", (D,), None),
)
_BIG = ("w_in", "w_conv_out", "w_pool_grp", "w_attn_out", "w_o", "w_up", "w_down")
_SMALL_SHARDED = ("meta", "conv_dw_w", "ffn_dw_w")
_SHARD_AXIS = {n: ax for n, _, ax in _PARAMS}


def _size(shape):
    n = 1
    for d in shape:
        n *= d
    return n


def _pack(parts, lanes, row_multiple):
    flat = jnp.concatenate([a.reshape(-1) for a in parts])
    rows = -(-flat.shape[0] // lanes)
    rows = -(-rows // row_multiple) * row_multiple
    flat = jnp.pad(flat, (0, rows * lanes - flat.shape[0]))
    return flat.reshape(rows, lanes)


def _unpack(blob, shapes):
    flat = blob.reshape(-1)
    out, off = [], 0
    for s in shapes:
        n = _size(s)
        out.append(flat[off : off + n].reshape(s))
        off += n
    return out


def _shard(a, ax, s):
    n = a.shape[ax] // 4
    return lax.slice_in_dim(a, s * n, (s + 1) * n, axis=ax)


def kernel(x, meta, norm1, w_in, conv_dw_w, conv_dw_b, conv_ln_g, conv_ln_b, w_conv_out, b_conv_out, w_pool_grp, pool_scale, w_attn_out, w_o, norm2, w_up, ffn_dw_w, ffn_dw_b, w_down, final_norm, loss_target, m_meta, m_norm1, m_w_in, m_conv_dw_w, m_conv_dw_b, m_conv_ln_g, m_conv_ln_b, m_w_conv_out, m_b_conv_out, m_w_pool_grp, m_pool_scale, m_w_attn_out, m_w_o, m_norm2, m_w_up, m_ffn_dw_w, m_ffn_dw_b, m_w_down, m_final_norm, v_meta, v_norm1, v_w_in, v_conv_dw_w, v_conv_dw_b, v_conv_ln_g, v_conv_ln_b, v_w_conv_out, v_b_conv_out, v_w_pool_grp, v_pool_scale, v_w_attn_out, v_w_o, v_norm2, v_w_up, v_ffn_dw_w, v_ffn_dw_b, v_w_down, v_final_norm):
    names = [n for n, _, _ in _PARAMS]
    w_loc = dict(zip(names, (meta, norm1, w_in, conv_dw_w, conv_dw_b, conv_ln_g, conv_ln_b, w_conv_out, b_conv_out, w_pool_grp, pool_scale, w_attn_out, w_o, norm2, w_up, ffn_dw_w, ffn_dw_b, w_down, final_norm)))
    m_loc = dict(zip(names, (m_meta, m_norm1, m_w_in, m_conv_dw_w, m_conv_dw_b, m_conv_ln_g, m_conv_ln_b, m_w_conv_out, m_b_conv_out, m_w_pool_grp, m_pool_scale, m_w_attn_out, m_w_o, m_norm2, m_w_up, m_ffn_dw_w, m_ffn_dw_b, m_w_down, m_final_norm)))
    v_loc = dict(zip(names, (v_meta, v_norm1, v_w_in, v_conv_dw_w, v_conv_dw_b, v_conv_ln_g, v_conv_ln_b, v_w_conv_out, v_b_conv_out, v_w_pool_grp, v_pool_scale, v_w_attn_out, v_w_o, v_norm2, v_w_up, v_ffn_dw_w, v_ffn_dw_b, v_w_down, v_final_norm)))

    big = _pack([w_loc[n].astype(_MXU) for n in _BIG], BLOB_LANES, 16)
    small = _pack([w_loc[n] for n in _SMALL_SHARDED], BLK, 8)
    big4, small4 = _allgather_xy([big, small], "allgather_weights")
    full = {n: w_loc[n] for n, _, ax in _PARAMS if ax is None}
    for group, blob4 in ((_BIG, big4), (_SMALL_SHARDED, small4)):
        per_chip = [_unpack(blob4[s], [w_loc[n].shape for n in group]) for s in range(4)]
        for k, n in enumerate(group):
            full[n] = jnp.concatenate([per_chip[s][k] for s in range(4)], axis=_SHARD_AXIS[n])

    loss, grad_x, grads = _local_step(x[0], loss_target[0], full)

    def owner_blob(src, s):
        return _pack([src[n] if ax is None else _shard(src[n], ax, s) for n, _, ax in _PARAMS], BLOB_LANES, ADAM_ROWS)

    gblob = jnp.stack([owner_blob(grads, s) for s in range(4)])
    gslots = _grad_exchange(gblob, "grad_exchange")
    local_shapes = [w_loc[n].shape for n in names]
    pk = lambda d: _pack([d[n] for n in names], BLOB_LANES, ADAM_ROWS)
    g_b, d_b, nm_b, nv_b = _adamw(pk(w_loc), pk(m_loc), pk(v_loc), gslots, "adamw")
    loss = lax.psum(loss[0, 0], ("x", "y", "c"))
    outs = [loss, grad_x[None]]
    for blob in (g_b, d_b, nm_b, nv_b):
        outs.extend(_unpack(blob, local_shapes))
    return tuple(outs)
```

```python
import functools

import jax
import jax.numpy as jnp
from jax import lax
from jax.experimental import pallas as pl
from jax.experimental.pallas import tpu as pltpu

F32 = jnp.float32
_MXU = jnp.bfloat16

D = 1024
N_META = 16
BLK = 128
CH = 2 * BLK
ATTN_UNROLLS = (8, 2, 1)
PAD_ROWS = BLK - N_META
N_LAYERS = 2
CONV_CH = 256
CONV_K = 31
POOL_CH = 256
POOL_WINDOWS = (2, 4, 8, 16)
POOL_GC = 64
ATT_W = 512
NH = 4
HD = 128
D_A = 2 * CONV_CH + POOL_CH + 3 * ATT_W
D_G = 3 * D
D_FF = 3 * D
FFN_K = 3
EPS = 1e-6
Q_SCALE = HD ** -0.5
LOG2E = 1.4426950408889634
LN2 = 0.6931471805599453

TR = 384
TB = 128
HALO = 32
HALO8 = 8
TC = 768
TF = 512
VMEM_LIMIT = 48 * 1024 * 1024

ADAM_LR = 0.001
ADAM_B1 = 0.9
ADAM_B2 = 0.999
ADAM_EPS = 1e-08
ADAM_WD = 0.01
ADAM_STEP = 10

GELU_C0 = 0.7978845608028654
GELU_C1 = 0.044715

MESH = pl.DeviceIdType.MESH
ANY = pl.BlockSpec(memory_space=pl.ANY)

BLOB_LANES = 1024
ADAM_ROWS = 128


def _pcall(body, **kw):
    return pl.pallas_call(body, **kw)


def _cp(sem):
    return pltpu.CompilerParams(dimension_semantics=sem, vmem_limit_bytes=VMEM_LIMIT)


def _dot(a, b):
    return jnp.dot(a.astype(_MXU), b.astype(_MXU), preferred_element_type=F32)


def _dot_t(a, b):
    return lax.dot_general(a.astype(_MXU), b.astype(_MXU), (((1,), (1,)), ((), ())), preferred_element_type=F32)


def _tdot(a, b):
    return lax.dot_general(a.astype(_MXU), b.astype(_MXU), (((0,), (0,)), ((), ())), preferred_element_type=F32)


def _sig(x):
    return 1.0 / (1.0 + jnp.exp(-x))


def _rms_fwd(x, g):
    r = lax.rsqrt(jnp.mean(x * x, axis=-1, keepdims=True) + EPS)
    n = x * r
    return r, n, n * g


def _rms_bwd(dy, r, n, g):
    dgain = jnp.sum(dy * n, axis=0, keepdims=True)
    dn = dy * g
    dx = r * (dn - n * jnp.mean(dn * n, axis=-1, keepdims=True))
    return dx, dgain


def _row_ids(tile_rows, i, shape):
    return i * tile_rows + lax.broadcasted_iota(jnp.int32, shape, 0)


def _rms_matmul(h, gain, w, name, split_out=False):
    R, N = h.shape[0], w.shape[1]
    nj = N // TC
    half = nj // 2

    def body(h_ref, g_ref, w_ref, o_ref, xn_ref):
        @pl.when(pl.program_id(1) == 0)
        def _():
            _, _, y = _rms_fwd(h_ref[...], g_ref[...])
            xn_ref[...] = y.astype(xn_ref.dtype)

        o_ref[...] = jnp.dot(xn_ref[...], w_ref[...], preferred_element_type=F32)

    if split_out:
        out_shape = jax.ShapeDtypeStruct((2, R, N // 2), F32)
        out_spec = pl.BlockSpec((None, TR, TC), lambda i, j: (j // half, i, j % half))
    else:
        out_shape = jax.ShapeDtypeStruct((R, N), F32)
        out_spec = pl.BlockSpec((TR, TC), lambda i, j: (i, j))
    return _pcall(
        body,
        grid=(R // TR, nj),
        in_specs=[
            pl.BlockSpec((TR, D), lambda i, j: (i, 0)),
            pl.BlockSpec((1, D), lambda i, j: (0, 0)),
            pl.BlockSpec((D, TC), lambda i, j: (0, j)),
        ],
        out_specs=out_spec,
        out_shape=out_shape,
        scratch_shapes=[pltpu.VMEM((TR, D), _MXU)],
        compiler_params=_cp(("parallel", "arbitrary")),
        name=name,
    )(h, gain, w.astype(_MXU))


def _qkv_cast(proj_a, name):
    R = proj_a.shape[0]

    def body(p_ref, o_ref):
        col = lax.broadcasted_iota(jnp.int32, (1, TC), 1) + pl.program_id(1) * TC
        sc = jnp.where(col < ATT_W, Q_SCALE * LOG2E, 1.0).astype(F32)
        o_ref[...] = (p_ref[...] * sc).astype(o_ref.dtype)

    return _pcall(
        body,
        grid=(R // TR, 2),
        in_specs=[pl.BlockSpec((TR, TC), lambda i, j: (i, j + 1))],
        out_specs=pl.BlockSpec((TR, TC), lambda i, j: (i, j)),
        out_shape=jax.ShapeDtypeStruct((R, 3 * ATT_W), _MXU),
        compiler_params=_cp(("parallel", "parallel")),
        name=name,
    )(proj_a)


def _sum_mat(w):
    rowi = lax.broadcasted_iota(jnp.int32, (w, w), 0)
    coli = lax.broadcasted_iota(jnp.int32, (w, w), 1)
    return (rowi > coli).astype(_MXU)


def _hi_lo_rows(x):
    hi = x.astype(_MXU)
    lo = (x - hi.astype(F32)).astype(_MXU)
    return jnp.concatenate([hi, lo], axis=0)


def _fold_sums(r):
    return r[:BLK] + r[BLK:]


def _attn_logs(z2, valid):
    m = jnp.minimum(z2, 0.0)
    d = m - z2
    t = jnp.log2(1.0 + jnp.exp2(m + d))
    lk = d - t
    if valid is not None:
        lk = jnp.where(valid, lk, 0.0)
    return lk, m - t


def _attn_weights(log_beta, r, valid, s_after):
    log_a = log_beta + r + s_after
    if valid is not None:
        log_a = jnp.where(valid, log_a, -1e30)
    return jnp.exp2(log_a)


def _attn_walk(i, phases, n_free, carry):
    ci = jnp.maximum(i - 1, 0) // 2

    def advance(states, stages, cr):
        for ph in stages:
            for u in range(len(states)):
                states[u], cr = ph(states[u], cr)
        return states, cr

    def at(c):
        return pl.multiple_of(BLK + c * CH, BLK)

    ends = [phases[0](at(ci), CH, True), phases[0](0, BLK, True)]
    ends, _ = advance(ends, phases[1 : 1 + n_free], None)
    _, carry = advance(ends[:1], phases[1 + n_free :], carry)
    left = ci
    for unroll in ATTN_UNROLLS:
        def group(t, cr, unroll=unroll, left=left):
            states = [phases[0](at(left - 1 - unroll * t - u), CH, False) for u in range(unroll)]
            return advance(states, phases[1:], cr)[1]

        carry = lax.fori_loop(0, left // unroll, group, carry)
        left = left % unroll
    return advance(ends[1:], phases[1 + n_free :], carry)[1]


def _attn_mask(i, off, w):
    qpos = i * BLK + lax.broadcasted_iota(jnp.int32, (BLK, w), 0)
    kpos = off + lax.broadcasted_iota(jnp.int32, (BLK, w), 1)
    return jnp.logical_and(kpos < qpos, kpos >= PAD_ROWS)


def _attn_fwd(qkv, name):
    R = qkv.shape[0]
    nb = R // BLK

    def body(q_ref, k_ref, v_ref, o_ref):
        i = pl.program_id(1)
        q = q_ref[...]
        sms = {CH: _sum_mat(CH), BLK: _sum_mat(BLK)}

        def scores(off, w, masked):
            valid = _attn_mask(i, off, w) if masked else None
            return off, w, valid, _dot_t(q, k_ref[pl.ds(off, w), :])

        def sums(state, carry):
            off, w, valid, z = state
            lk, log_beta = _attn_logs(z, valid)
            return (off, w, valid, log_beta, _dot(lk, sms[w]), jnp.sum(lk, axis=1, keepdims=True)), carry

        def output(state, carry):
            off, w, valid, log_beta, r, total = state
            acc, s_after = carry
            a = _attn_weights(log_beta, r, valid, s_after)
            return None, (acc + _dot(a, v_ref[pl.ds(off, w), :]), s_after + total)

        acc, _ = _attn_walk(i, (scores, sums, output), 1, (jnp.zeros((BLK, HD), F32), jnp.zeros((BLK, 1), F32)))
        o_ref[...] = acc

    return _pcall(
        body,
        grid=(NH, nb),
        in_specs=[
            pl.BlockSpec((BLK, HD), lambda h, i: (i, h)),
            pl.BlockSpec((R, HD), lambda h, i: (0, NH + h)),
            pl.BlockSpec((R, HD), lambda h, i: (0, 2 * NH + h)),
        ],
        out_specs=pl.BlockSpec((BLK, HD), lambda h, i: (i, h)),
        out_shape=jax.ShapeDtypeStruct((R, ATT_W), F32),
        compiler_params=_cp(("parallel", "arbitrary")),
        name=name,
    )(qkv, qkv, qkv)


def _pool_consts(i):
    lane = lax.broadcasted_iota(jnp.int32, (1, POOL_CH), 1)
    wsize = jnp.where(lane < POOL_GC, 2.0, jnp.where(lane < 2 * POOL_GC, 4.0, jnp.where(lane < 3 * POOL_GC, 8.0, 16.0)))
    return lane, wsize


def _pool_div(rows, wsize):
    pos1 = (rows - (PAD_ROWS - 1)).astype(F32)
    return jnp.clip(pos1, 1.0, wsize)


def _lane_select(lane, s2, s4, s8, s16):
    return jnp.where(lane < POOL_GC, s2, jnp.where(lane < 2 * POOL_GC, s4, jnp.where(lane < 3 * POOL_GC, s8, s16)))


def _branch_pre(proj_a, conv_w, conv_b, name):
    R = proj_a.shape[0]
    nh = TR // HALO

    def body(t_ref, h_ref, w_ref, b_ref, c_ref, p_ref, gext, pext):
        i = pl.program_id(0)
        t = t_ref[...]
        hl = jnp.where(i == 0, 0.0, h_ref[...])
        gext[pl.ds(0, HALO), :] = hl[:, :CONV_CH] * _sig(hl[:, CONV_CH : 2 * CONV_CH])
        gext[pl.ds(HALO, TR), :] = t[:, :CONV_CH] * _sig(t[:, CONV_CH : 2 * CONV_CH])
        acc = jnp.zeros((TR, CONV_CH), F32) + b_ref[...]
        for k in range(CONV_K):
            acc = acc + w_ref[pl.ds(k, 1), :] * gext[pl.ds(HALO - (CONV_K - 1) + k, TR), :]
        c_ref[...] = acc

        p = t[:, 2 * CONV_CH :]
        pext[pl.ds(0, HALO), :] = hl[:, 2 * CONV_CH :]
        pext[pl.ds(HALO, TR), :] = p

        def back(k):
            return pext[pl.ds(HALO - k, TR), :]

        s2 = p + back(1)
        s4 = s2 + back(2) + back(3)
        s8 = s4 + back(4) + back(5) + back(6) + back(7)
        s16 = s8
        for k in range(8, 16):
            s16 = s16 + back(k)
        lane, wsize = _pool_consts(i)
        div = _pool_div(_row_ids(TR, i, (TR, POOL_CH)), wsize)
        p_ref[...] = (_lane_select(lane, s2, s4, s8, s16) / div - p).astype(p_ref.dtype)

    return _pcall(
        body,
        grid=(R // TR,),
        in_specs=[
            pl.BlockSpec((TR, TC), lambda i: (i, 0)),
            pl.BlockSpec((HALO, TC), lambda i: (jnp.maximum(i * nh - 1, 0), 0)),
            pl.BlockSpec((HALO, CONV_CH), lambda i: (0, 0)),
            pl.BlockSpec((1, CONV_CH), lambda i: (0, 0)),
        ],
        out_specs=[pl.BlockSpec((TR, CONV_CH), lambda i: (i, 0)), pl.BlockSpec((TR, POOL_CH), lambda i: (i, 0))],
        out_shape=[jax.ShapeDtypeStruct((R, CONV_CH), F32), jax.ShapeDtypeStruct((R, POOL_CH), _MXU)],
        scratch_shapes=[pltpu.VMEM((TR + HALO, CONV_CH), F32), pltpu.VMEM((TR + HALO, POOL_CH), F32)],
        compiler_params=_cp(("parallel",)),
        name=name,
    )(proj_a, proj_a, conv_w, conv_b)


def _mix_values(c, pooled, att, gates, ln_g, ln_b, w_co, b_co, w_bd, p_scale, w_ao):
    mu = jnp.mean(c, axis=-1, keepdims=True)
    xc = c - mu
    rstd = lax.rsqrt(jnp.mean(xc * xc, axis=-1, keepdims=True) + EPS)
    nl = xc * rstd
    ln = nl * ln_g + ln_b
    sg = _sig(ln)
    s = ln * sg
    ya = _dot(s, w_co) + b_co
    ybr = _dot(pooled, w_bd)
    yb = ybr * p_scale
    yc = _dot(att, w_ao)
    g = _sig(gates)
    g0, g1, g2 = g[:, :D], g[:, D : 2 * D], g[:, 2 * D :]
    mixed = g0 * ya + g1 * yb + g2 * yc
    return dict(rstd=rstd, nl=nl, ln=ln, sg=sg, s=s, ya=ya, ybr=ybr, yb=yb, yc=yc, g0=g0, g1=g1, g2=g2, mixed=mixed)


_MIX_W_SPECS = [
    pl.BlockSpec((1, CONV_CH), lambda i: (0, 0)),
    pl.BlockSpec((1, CONV_CH), lambda i: (0, 0)),
    pl.BlockSpec((CONV_CH, D), lambda i: (0, 0)),
    pl.BlockSpec((1, D), lambda i: (0, 0)),
    pl.BlockSpec((POOL_CH, D), lambda i: (0, 0)),
    pl.BlockSpec((1, D), lambda i: (0, 0)),
    pl.BlockSpec((ATT_W, D), lambda i: (0, 0)),
    pl.BlockSpec((D, D), lambda i: (0, 0)),
]


def _mix_act_specs(t):
    return [
        pl.BlockSpec((t, D_G), lambda i: (i, 0)),
        pl.BlockSpec((t, CONV_CH), lambda i: (i, 0)),
        pl.BlockSpec((t, POOL_CH), lambda i: (i, 0)),
        pl.BlockSpec((t, ATT_W), lambda i: (i, 0)),
    ]


def _mix_fwd(h, gates, c, pooled, att, mw, name):
    R = h.shape[0]

    def body(h_ref, g_ref, c_ref, p_ref, a_ref, lg, lb, wco, bco, wbd, ps, wao, wo, o_ref):
        v = _mix_values(c_ref[...], p_ref[...], a_ref[...], g_ref[...], lg[...], lb[...], wco[...], bco[...],
                        wbd[...], ps[...], wao[...])
        out = h_ref[...] + _dot(v["mixed"], wo[...])
        rows = _row_ids(TB, pl.program_id(0), (TB, D))
        o_ref[...] = jnp.where(rows >= PAD_ROWS, out, 0.0)

    return _pcall(
        body,
        grid=(R // TB,),
        in_specs=[pl.BlockSpec((TB, D), lambda i: (i, 0))] + _mix_act_specs(TB) + _MIX_W_SPECS,
        out_specs=pl.BlockSpec((TB, D), lambda i: (i, 0)),
        out_shape=jax.ShapeDtypeStruct((R, D), F32),
        compiler_params=_cp(("parallel",)),
        name=name,
    )(h, gates, c, pooled, att, *mw)


def _ffn_conv(ut, uh, cw_ref, cb_ref, ext):
    ext[:, pl.ds(0, HALO8), :] = uh
    ext[:, pl.ds(HALO8, TR), :] = ut
    um1 = ext[:, pl.ds(HALO8 - 1, TR), :]
    um2 = ext[:, pl.ds(HALO8 - 2, TR), :]
    cw = cw_ref[...]
    conv = cw[0][:, None, :] * um2 + cw[1][:, None, :] * um1 + cw[2][:, None, :] * ut + cb_ref[...][:, None, :]
    return conv, um1, um2


def _gelu_parts(x):
    th = jnp.tanh(GELU_C0 * (x + GELU_C1 * x * x * x))
    return th, 0.5 * x * (1.0 + th)


def _ffn_in_specs(nrow8, order):
    n8 = TR // HALO8
    return [
        pl.BlockSpec((2, TR, TF), lambda *g: (0, order(*g)[0], order(*g)[1])),
        pl.BlockSpec((2, HALO8, TF), lambda *g: (0, jnp.maximum(order(*g)[0] * n8 - 1, 0), order(*g)[1])),
        pl.BlockSpec((FFN_K, 2, TF), lambda *g: (0, 0, order(*g)[1])),
        pl.BlockSpec((2, TF), lambda *g: (0, order(*g)[1])),
    ]


def _ffn_fwd(u3, h2, cw, cb, w_down, name):
    R = h2.shape[0]
    nj = D_FF // TF

    def body(u_ref, uh_ref, cw_ref, cb_ref, wd_ref, h_ref, o_ref, ext, acc):
        i, j = pl.program_id(0), pl.program_id(1)
        uh = jnp.where(i == 0, 0.0, uh_ref[...])
        conv, _, _ = _ffn_conv(u_ref[...], uh, cw_ref, cb_ref, ext)
        _, a = _gelu_parts(conv[0])
        part = _dot(a * conv[1], wd_ref[...])

        @pl.when(j == 0)
        def _():
            acc[...] = part

        @pl.when(j > 0)
        def _():
            acc[...] += part

        @pl.when(j == nj - 1)
        def _():
            rows = _row_ids(TR, i, (TR, D))
            o_ref[...] = jnp.where(rows >= PAD_ROWS, h_ref[...] + acc[...], 0.0)

    return _pcall(
        body,
        grid=(R // TR, nj),
        in_specs=_ffn_in_specs(R // HALO8, lambda i, j: (i, j))
        + [pl.BlockSpec((TF, D), lambda i, j: (j, 0)), pl.BlockSpec((TR, D), lambda i, j: (i, 0))],
        out_specs=pl.BlockSpec((TR, D), lambda i, j: (i, 0)),
        out_shape=jax.ShapeDtypeStruct((R, D), F32),
        scratch_shapes=[pltpu.VMEM((2, TR + HALO8, TF), F32), pltpu.VMEM((TR, D), F32)],
        compiler_params=_cp(("parallel", "arbitrary")),
        name=name,
    )(u3, u3, cw, cb, w_down.astype(_MXU), h2)


def _loss_bwd(h, target, gain, name):
    R = h.shape[0]

    def body(h_ref, t_ref, g_ref, dh_ref, loss_ref, dg_ref):
        i = pl.program_id(0)

        @pl.when(i == 0)
        def _():
            loss_ref[...] = jnp.zeros_like(loss_ref)
            dg_ref[...] = jnp.zeros_like(dg_ref)
            dh_ref[...] = jnp.zeros_like(dh_ref)

        @pl.when(i > 0)
        def _():
            g = g_ref[...]
            r, n, y = _rms_fwd(h_ref[...], g)
            e = y - t_ref[...]
            loss_ref[...] += (0.5 / D) * jnp.sum(jnp.sum(e * e, axis=1, keepdims=True), axis=0, keepdims=True)
            dx, dgain = _rms_bwd(e * (1.0 / D), r, n, g)
            dg_ref[...] += dgain
            dh_ref[...] = dx

    return _pcall(
        body,
        grid=(R // BLK,),
        in_specs=[
            pl.BlockSpec((BLK, D), lambda i: (i, 0)),
            pl.BlockSpec((BLK, D), lambda i: (jnp.maximum(i - 1, 0), 0)),
            pl.BlockSpec((1, D), lambda i: (0, 0)),
        ],
        out_specs=[
            pl.BlockSpec((BLK, D), lambda i: (i, 0)),
            pl.BlockSpec((1, 1), lambda i: (0, 0)),
            pl.BlockSpec((1, D), lambda i: (0, 0)),
        ],
        out_shape=[
            jax.ShapeDtypeStruct((R, D), F32),
            jax.ShapeDtypeStruct((1, 1), F32),
            jax.ShapeDtypeStruct((1, D), F32),
        ],
        compiler_params=_cp(("arbitrary",)),
        name=name,
    )(h, target, gain)


def _tdot_acc(a, b, name):
    R, M = a.shape
    split = b.ndim == 3
    N = 2 * b.shape[2] if split else b.shape[1]
    tn = b.shape[2] if split else N
    tm = min(M, 512)
    tk = TR

    def body(a_ref, b_ref, o_ref):
        part = _tdot(a_ref[...], b_ref[...])

        @pl.when(pl.program_id(2) == 0)
        def _():
            o_ref[...] = part

        @pl.when(pl.program_id(2) > 0)
        def _():
            o_ref[...] += part

    if split:
        b_spec = pl.BlockSpec((None, tk, tn), lambda m, n, k: (n, k, 0))
    else:
        b_spec = pl.BlockSpec((tk, tn), lambda m, n, k: (k, n))
    return _pcall(
        body,
        grid=(M // tm, N // tn, R // tk),
        in_specs=[pl.BlockSpec((tk, tm), lambda m, n, k: (k, m)), b_spec],
        out_specs=pl.BlockSpec((tm, tn), lambda m, n, k: (m, n)),
        out_shape=jax.ShapeDtypeStruct((M, N), F32),
        compiler_params=_cp(("parallel", "parallel", "arbitrary")),
        name=name,
    )(a, b)


def _ffn_bwd1(dh3, u3, cw, cb, w_down, name):
    R = dh3.shape[0]
    nj = D_FF // TF

    def body(u_ref, uh_ref, cw_ref, cb_ref, wd_ref, dh_ref, dc_ref, act_ref, dwb_ref, ext):
        j, i = pl.program_id(0), pl.program_id(1)
        ut = u_ref[...]
        uh = jnp.where(i == 0, 0.0, uh_ref[...])
        conv, um1, um2 = _ffn_conv(ut, uh, cw_ref, cb_ref, ext)
        gt, val = conv[0], conv[1]
        th, a = _gelu_parts(gt)
        dact = _dot_t(dh_ref[...], wd_ref[...])
        dgelu = 0.5 * (1.0 + th) + 0.5 * gt * (1.0 - th * th) * (GELU_C0 * (1.0 + 3.0 * GELU_C1 * gt * gt))
        dgt = dact * val * dgelu
        dval = dact * a
        act_ref[...] = (a * val).astype(act_ref.dtype)
        dc_ref[0] = dgt
        dc_ref[1] = dval

        @pl.when(i == 0)
        def _():
            dwb_ref[...] = jnp.zeros_like(dwb_ref)

        for half, dcv in ((0, dgt), (1, dval)):
            dwb_ref[half, pl.ds(0, 1), :] += jnp.sum(um2[half] * dcv, axis=0, keepdims=True)
            dwb_ref[half, pl.ds(1, 1), :] += jnp.sum(um1[half] * dcv, axis=0, keepdims=True)
            dwb_ref[half, pl.ds(2, 1), :] += jnp.sum(ut[half] * dcv, axis=0, keepdims=True)
            dwb_ref[half, pl.ds(3, 1), :] += jnp.sum(dcv, axis=0, keepdims=True)

    return _pcall(
        body,
        grid=(nj, R // TR),
        in_specs=_ffn_in_specs(R // HALO8, lambda j, i: (i, j))
        + [pl.BlockSpec((TF, D), lambda j, i: (j, 0)), pl.BlockSpec((TR, D), lambda j, i: (i, 0))],
        out_specs=[
            pl.BlockSpec((2, TR, TF), lambda j, i: (0, i, j)),
            pl.BlockSpec((TR, TF), lambda j, i: (i, j)),
            pl.BlockSpec((2, 8, TF), lambda j, i: (0, 0, j)),
        ],
        out_shape=[
            jax.ShapeDtypeStruct((2, R, D_FF), F32),
            jax.ShapeDtypeStruct((R, D_FF), _MXU),
            jax.ShapeDtypeStruct((2, 8, D_FF), F32),
        ],
        scratch_shapes=[pltpu.VMEM((2, TR + HALO8, TF), F32)],
        compiler_params=_cp(("parallel", "arbitrary")),
        name=name,
    )(u3, u3, cw, cb, w_down.astype(_MXU), dh3)


def _ffn_bwd2(dc3, cw, w_up, h2, dh3, gain, name):
    R = h2.shape[0]
    nj = D_FF // TF
    n8 = TR // HALO8
    last8 = R // HALO8 - 1
    ni = R // TR

    def body(dc_ref, dn_ref, cw_ref, wg_ref, wv_ref, h_ref, dh_ref, g_ref, du_ref, o_ref, xn_ref, dg_ref, ext, acc):
        i, j = pl.program_id(0), pl.program_id(1)
        dc = dc_ref[...]
        ext[:, pl.ds(0, TR), :] = dc
        ext[:, pl.ds(TR, HALO8), :] = jnp.where(i == ni - 1, 0.0, dn_ref[...])
        cw = cw_ref[...]
        du = (cw[2][:, None, :] * dc + cw[1][:, None, :] * ext[:, pl.ds(1, TR), :]
              + cw[0][:, None, :] * ext[:, pl.ds(2, TR), :])
        du_ref[...] = du.astype(du_ref.dtype)
        part = _dot_t(du[0], wg_ref[...]) + _dot_t(du[1], wv_ref[...])

        @pl.when(j == 0)
        def _():
            acc[...] = part

        @pl.when(j > 0)
        def _():
            acc[...] += part

        @pl.when(jnp.logical_and(i == 0, j == 0))
        def _():
            dg_ref[...] = jnp.zeros_like(dg_ref)

        @pl.when(j == nj - 1)
        def _():
            g = g_ref[...]
            r, n, y = _rms_fwd(h_ref[...], g)
            dx, dgain = _rms_bwd(acc[...], r, n, g)
            dg_ref[...] += dgain
            rows = _row_ids(TR, i, (TR, D))
            o_ref[...] = jnp.where(rows >= PAD_ROWS, dh_ref[...] + dx, 0.0)
            xn_ref[...] = y.astype(xn_ref.dtype)

    w_up = w_up.astype(_MXU)
    return _pcall(
        body,
        grid=(ni, nj),
        in_specs=[
            pl.BlockSpec((2, TR, TF), lambda i, j: (0, i, j)),
            pl.BlockSpec((2, HALO8, TF), lambda i, j: (0, jnp.minimum((i + 1) * n8, last8), j)),
            pl.BlockSpec((FFN_K, 2, TF), lambda i, j: (0, 0, j)),
            pl.BlockSpec((D, TF), lambda i, j: (0, j)),
            pl.BlockSpec((D, TF), lambda i, j: (0, nj + j)),
            pl.BlockSpec((TR, D), lambda i, j: (i, 0)),
            pl.BlockSpec((TR, D), lambda i, j: (i, 0)),
            pl.BlockSpec((1, D), lambda i, j: (0, 0)),
        ],
        out_specs=[
            pl.BlockSpec((2, TR, TF), lambda i, j: (0, i, j)),
            pl.BlockSpec((TR, D), lambda i, j: (i, 0)),
            pl.BlockSpec((TR, D), lambda i, j: (i, 0)),
            pl.BlockSpec((1, D), lambda i, j: (0, 0)),
        ],
        out_shape=[
            jax.ShapeDtypeStruct((2, R, D_FF), _MXU),
            jax.ShapeDtypeStruct((R, D), F32),
            jax.ShapeDtypeStruct((R, D), _MXU),
            jax.ShapeDtypeStruct((1, D), F32),
        ],
        scratch_shapes=[pltpu.VMEM((2, TR + HALO8, TF), F32), pltpu.VMEM((TR, D), F32)],
        compiler_params=_cp(("arbitrary", "arbitrary")),
        name=name,
    )(dc3, dc3, cw, w_up, w_up, h2, dh3, gain)


def _mix_bwd(dh2, gates, c, pooled, att, mw, name):
    R = dh2.shape[0]

    def body(dh_ref, g_ref, c_ref, p_ref, a_ref, lg, lb, wco, bco, wbd, ps, wao, wo,
             dg_ref, dc_ref, dp_ref, da_ref, mx_ref, s_ref, dya_ref, dyb_ref, dyc_ref, accd_ref, accc_ref):
        v = _mix_values(c_ref[...], p_ref[...], a_ref[...], g_ref[...], lg[...], lb[...], wco[...], bco[...],
                        wbd[...], ps[...], wao[...])
        dmix = _dot_t(dh_ref[...], wo[...])
        for k, (gk, yk) in enumerate(((v["g0"], v["ya"]), (v["g1"], v["yb"]), (v["g2"], v["yc"]))):
            dg_ref[:, k * D : (k + 1) * D] = dmix * yk * gk * (1.0 - gk)
        dya = dmix * v["g0"]
        dyb = dmix * v["g1"]
        dyc = dmix * v["g2"]
        ds = _dot_t(dya, wco[...])
        ln, sg, nl = v["ln"], v["sg"], v["nl"]
        dln = ds * (sg * (1.0 + ln * (1.0 - sg)))
        dn = dln * lg[...]
        dc = v["rstd"] * (dn - jnp.mean(dn, axis=-1, keepdims=True) - nl * jnp.mean(dn * nl, axis=-1, keepdims=True))
        dybs = dyb * ps[...]
        dc_ref[...] = dc
        dp_ref[...] = _dot_t(dybs, wbd[...])
        da_ref[...] = _dot_t(dyc, wao[...])
        mx_ref[...] = v["mixed"].astype(mx_ref.dtype)
        s_ref[...] = v["s"].astype(s_ref.dtype)
        dya_ref[...] = dya.astype(dya_ref.dtype)
        dyb_ref[...] = dybs.astype(dyb_ref.dtype)
        dyc_ref[...] = dyc.astype(dyc_ref.dtype)

        @pl.when(pl.program_id(0) == 0)
        def _():
            accd_ref[...] = jnp.zeros_like(accd_ref)
            accc_ref[...] = jnp.zeros_like(accc_ref)

        accd_ref[pl.ds(0, 1), :] += jnp.sum(dya, axis=0, keepdims=True)
        accd_ref[pl.ds(1, 1), :] += jnp.sum(dyb * v["ybr"], axis=0, keepdims=True)
        accc_ref[pl.ds(0, 1), :] += jnp.sum(dln * nl, axis=0, keepdims=True)
        accc_ref[pl.ds(1, 1), :] += jnp.sum(dln, axis=0, keepdims=True)
        accc_ref[pl.ds(2, 1), :] += jnp.sum(dc, axis=0, keepdims=True)

    def row(w):
        return pl.BlockSpec((TB, w), lambda i: (i, 0))

    return _pcall(
        body,
        grid=(R // TB,),
        in_specs=[row(D)] + _mix_act_specs(TB) + _MIX_W_SPECS,
        out_specs=[row(D_G), row(CONV_CH), row(POOL_CH), row(ATT_W), row(D), row(CONV_CH), row(D), row(D), row(D),
                   pl.BlockSpec((8, D), lambda i: (0, 0)), pl.BlockSpec((8, CONV_CH), lambda i: (0, 0))],
        out_shape=[
            jax.ShapeDtypeStruct((R, D_G), F32),
            jax.ShapeDtypeStruct((R, CONV_CH), F32),
            jax.ShapeDtypeStruct((R, POOL_CH), F32),
            jax.ShapeDtypeStruct((R, ATT_W), F32),
            jax.ShapeDtypeStruct((R, D), _MXU),
            jax.ShapeDtypeStruct((R, CONV_CH), _MXU),
            jax.ShapeDtypeStruct((R, D), _MXU),
            jax.ShapeDtypeStruct((R, D), _MXU),
            jax.ShapeDtypeStruct((R, D), _MXU),
            jax.ShapeDtypeStruct((8, D), F32),
            jax.ShapeDtypeStruct((8, CONV_CH), F32),
        ],
        compiler_params=_cp(("arbitrary",)),
        name=name,
    )(dh2, gates, c, pooled, att, *mw)


def _attn_bwd(qkv, att, datt, name):
    R = qkv.shape[0]
    nb = R // BLK

    def body(q_ref, k_ref, v_ref, o_ref, do_ref, dq_ref, dk_ref, dv_ref):
        i = pl.program_id(1)

        @pl.when(i == 0)
        def _():
            dk_ref[...] = jnp.zeros_like(dk_ref)
            dv_ref[...] = jnp.zeros_like(dv_ref)

        q = q_ref[...]
        dob = do_ref[...].astype(_MXU)
        dof = dob.astype(F32)
        sms = {CH: _sum_mat(CH), BLK: _sum_mat(BLK)}
        e_all = jnp.sum(dof * o_ref[...], axis=1, keepdims=True)

        def scores(off, w, masked):
            valid = _attn_mask(i, off, w) if masked else None
            z = _dot_t(q, k_ref[pl.ds(off, w), :])
            return off, w, valid, z, _dot_t(dob, v_ref[pl.ds(off, w), :])

        def sums(state, carry):
            off, w, valid, z, da = state
            lk, log_beta = _attn_logs(z, valid)
            return (off, w, valid, log_beta, _dot(lk, sms[w]), jnp.sum(lk, axis=1, keepdims=True), da), carry

        def weights(state, carry):
            off, w, valid, log_beta, r, total, da = state
            dq, s_after, e_done = carry
            ab = _attn_weights(log_beta, r, valid, s_after).astype(_MXU)
            e = ab.astype(F32) * da
            re = jnp.dot(_hi_lo_rows(e), sms[w], preferred_element_type=F32)
            return (off, w, valid, jnp.exp2(log_beta), ab, e, re), (dq, s_after + total, e_done)

        def grads(state, carry):
            off, w, valid, beta, ab, e, re = state
            dq, s_after, e_done = carry
            dz = e - beta * ((e_all - e_done) - _fold_sums(re))
            if valid is not None:
                dz = jnp.where(valid, dz, 0.0)
            dzb = (dz * LN2).astype(_MXU)
            dk_ref[pl.ds(off, w), :] += _tdot(dzb, q)
            dv_ref[pl.ds(off, w), :] += _tdot(ab, dob)
            dq = dq + jnp.dot(dzb, k_ref[pl.ds(off, w), :], preferred_element_type=F32)
            return None, (dq, s_after, e_done + jnp.sum(e, axis=1, keepdims=True))

        zero = jnp.zeros((BLK, 1), F32)
        dq, _, _ = _attn_walk(i, (scores, sums, weights, grads), 1, (jnp.zeros((BLK, HD), F32), zero, zero))
        dq_ref[...] = dq * (Q_SCALE * LOG2E)

    blk = pl.BlockSpec((BLK, HD), lambda h, i: (i, h))
    col = pl.BlockSpec((R, HD), lambda h, i: (0, h))
    return _pcall(
        body,
        grid=(NH, nb),
        in_specs=[
            blk,
            pl.BlockSpec((R, HD), lambda h, i: (0, NH + h)),
            pl.BlockSpec((R, HD), lambda h, i: (0, 2 * NH + h)),
            blk,
            blk,
        ],
        out_specs=[blk, col, col],
        out_shape=[jax.ShapeDtypeStruct((R, ATT_W), F32)] * 3,
        compiler_params=pltpu.CompilerParams(dimension_semantics=("arbitrary", "arbitrary"),
                                             vmem_limit_bytes=56 * 1024 * 1024),
        name=name,
    )(qkv, qkv, qkv, att, datt)


def _branch_bwd(dc, dpooled, proj_a, conv_w, name):
    R = dc.shape[0]
    nh = TR // HALO
    last = R // HALO - 1
    ni = R // TR

    def body(dc_ref, dcn_ref, dp_ref, dpn_ref, t_ref, h_ref, w_ref, o_ref, dcw_ref, gext, dcext, eext):
        i = pl.program_id(0)
        is_last = i == ni - 1
        t = t_ref[...]
        hl = jnp.where(i == 0, 0.0, h_ref[...])
        a = t[:, :CONV_CH]
        sg = _sig(t[:, CONV_CH : 2 * CONV_CH])
        gext[pl.ds(0, HALO), :] = hl[:, :CONV_CH] * _sig(hl[:, CONV_CH : 2 * CONV_CH])
        gext[pl.ds(HALO, TR), :] = a * sg
        dct = dc_ref[...]
        dcext[pl.ds(0, TR), :] = dct
        dcext[pl.ds(TR, HALO), :] = jnp.where(is_last, 0.0, dcn_ref[...])

        @pl.when(i == 0)
        def _():
            dcw_ref[...] = jnp.zeros_like(dcw_ref)

        dglu = jnp.zeros((TR, CONV_CH), F32)
        for k in range(CONV_K):
            dglu = dglu + w_ref[pl.ds(k, 1), :] * dcext[pl.ds(CONV_K - 1 - k, TR), :]
            dcw_ref[pl.ds(k, 1), :] += jnp.sum(gext[pl.ds(HALO - (CONV_K - 1) + k, TR), :] * dct, axis=0, keepdims=True)
        o_ref[:, :CONV_CH] = dglu * sg
        o_ref[:, CONV_CH : 2 * CONV_CH] = dglu * a * sg * (1.0 - sg)

        lane, wsize = _pool_consts(i)
        dpt = dp_ref[...]
        eext[pl.ds(0, TR), :] = dpt / _pool_div(_row_ids(TR, i, (TR, POOL_CH)), wsize)
        nxt = dpn_ref[...] / _pool_div(_row_ids(TR, i + 1, (HALO, POOL_CH)), wsize)
        eext[pl.ds(TR, HALO), :] = jnp.where(is_last, 0.0, nxt)

        def fwd(k):
            return eext[pl.ds(k, TR), :]

        s2 = fwd(0) + fwd(1)
        s4 = s2 + fwd(2) + fwd(3)
        s8 = s4 + fwd(4) + fwd(5) + fwd(6) + fwd(7)
        s16 = s8
        for k in range(8, 16):
            s16 = s16 + fwd(k)
        o_ref[:, 2 * CONV_CH :] = _lane_select(lane, s2, s4, s8, s16) - dpt

    def nxt_spec(w):
        return pl.BlockSpec((HALO, w), lambda i: (jnp.minimum((i + 1) * nh, last), 0))

    return _pcall(
        body,
        grid=(ni,),
        in_specs=[
            pl.BlockSpec((TR, CONV_CH), lambda i: (i, 0)),
            nxt_spec(CONV_CH),
            pl.BlockSpec((TR, POOL_CH), lambda i: (i, 0)),
            nxt_spec(POOL_CH),
            pl.BlockSpec((TR, TC), lambda i: (i, 0)),
            pl.BlockSpec((HALO, TC), lambda i: (jnp.maximum(i * nh - 1, 0), 0)),
            pl.BlockSpec((HALO, CONV_CH), lambda i: (0, 0)),
        ],
        out_specs=[pl.BlockSpec((TR, TC), lambda i: (i, 0)), pl.BlockSpec((HALO, CONV_CH), lambda i: (0, 0))],
        out_shape=[jax.ShapeDtypeStruct((R, TC), F32), jax.ShapeDtypeStruct((HALO, CONV_CH), F32)],
        scratch_shapes=[
            pltpu.VMEM((TR + HALO, CONV_CH), F32),
            pltpu.VMEM((TR + HALO, CONV_CH), F32),
            pltpu.VMEM((TR + HALO, POOL_CH), F32),
        ],
        compiler_params=_cp(("arbitrary",)),
        name=name,
    )(dc, dc, dpooled, dpooled, proj_a, proj_a, conv_w)


def _in_bwd(dfront, dq, dk, dv, dgates, w_a, w_g, h, dh2, gain, name):
    R = h.shape[0]
    nj = 1 + D_G // TC
    ni = R // TR
    w_a = w_a.astype(_MXU)
    w_f, w_q = w_a[:, :TC], w_a[:, TC : TC + ATT_W]
    w_k, w_v = w_a[:, TC + ATT_W : TC + 2 * ATT_W], w_a[:, TC + 2 * ATT_W :]

    def body(df_ref, dq_ref, dk_ref, dv_ref, dg_ref, wf_ref, wq_ref, wk_ref, wv_ref, wg_ref, h_ref, dh_ref, g_ref,
             o_ref, hn_ref, dgain_ref, acc):
        i, j = pl.program_id(0), pl.program_id(1)

        @pl.when(j == 0)
        def _():
            acc[...] = (_dot_t(df_ref[...], wf_ref[...]) + _dot_t(dq_ref[...], wq_ref[...])
                        + _dot_t(dk_ref[...], wk_ref[...]) + _dot_t(dv_ref[...], wv_ref[...]))

        @pl.when(j > 0)
        def _():
            acc[...] += _dot_t(dg_ref[...], wg_ref[...])

        @pl.when(jnp.logical_and(i == 0, j == 0))
        def _():
            dgain_ref[...] = jnp.zeros_like(dgain_ref)

        @pl.when(j == nj - 1)
        def _():
            g = g_ref[...]
            r, n, y = _rms_fwd(h_ref[...], g)
            dx, dgain = _rms_bwd(acc[...], r, n, g)
            dgain_ref[...] += dgain
            rows = _row_ids(TR, i, (TR, D))
            o_ref[...] = jnp.where(rows >= PAD_ROWS, dh_ref[...] + dx, 0.0)
            hn_ref[...] = y.astype(hn_ref.dtype)

    def row(w):
        return pl.BlockSpec((TR, w), lambda i, j: (i, 0))

    def whole(w):
        return pl.BlockSpec((D, w), lambda i, j: (0, 0))

    def gcol(i, j):
        return jnp.maximum(j - 1, 0)

    return _pcall(
        body,
        grid=(ni, nj),
        in_specs=[
            row(TC), row(ATT_W), row(ATT_W), row(ATT_W),
            pl.BlockSpec((TR, TC), lambda i, j: (i, gcol(i, j))),
            whole(TC), whole(ATT_W), whole(ATT_W), whole(ATT_W),
            pl.BlockSpec((D, TC), lambda i, j: (0, gcol(i, j))),
            row(D), row(D),
            pl.BlockSpec((1, D), lambda i, j: (0, 0)),
        ],
        out_specs=[row(D), row(D), pl.BlockSpec((1, D), lambda i, j: (0, 0))],
        out_shape=[jax.ShapeDtypeStruct((R, D), F32), jax.ShapeDtypeStruct((R, D), _MXU), jax.ShapeDtypeStruct((1, D), F32)],
        scratch_shapes=[pltpu.VMEM((TR, D), F32)],
        compiler_params=_cp(("arbitrary", "arbitrary")),
        name=name,
    )(dfront, dq, dk, dv, dgates, w_f, w_q, w_k, w_v, w_g.astype(_MXU), h, dh2, gain)


def _pool_blockdiag(w_grp):
    eye = jnp.eye(len(POOL_WINDOWS), dtype=w_grp.dtype)
    return jnp.einsum("gcd,gh->gchd", w_grp, eye).reshape(POOL_CH, D)


def _pool_blockdiag_grad(dw_bd):
    d4 = dw_bd.reshape(len(POOL_WINDOWS), POOL_GC, len(POOL_WINDOWS), D // len(POOL_WINDOWS))
    return jnp.stack([d4[g, :, g, :] for g in range(len(POOL_WINDOWS))])


def _local_step(x, target, p):
    row = lambda a: a.reshape(1, -1)
    h = jnp.concatenate([jnp.zeros((PAD_ROWS, D), F32), p["meta"], x], axis=0)
    saved = []
    for l in range(N_LAYERS):
        w_a, w_g = p["w_in"][l][:, :D_A], p["w_in"][l][:, D_A:]
        conv_w = jnp.concatenate([p["conv_dw_w"][l], jnp.zeros((1, CONV_CH), F32)], axis=0)
        cw3 = p["ffn_dw_w"][l].reshape(FFN_K, 2, D_FF)
        cb2 = p["ffn_dw_b"][l].reshape(2, D_FF)
        mw = (row(p["conv_ln_g"][l]), row(p["conv_ln_b"][l]), p["w_conv_out"][l].astype(_MXU), row(p["b_conv_out"][l]),
              _pool_blockdiag(p["w_pool_grp"][l]).astype(_MXU), row(p["pool_scale"][l]),
              p["w_attn_out"][l].astype(_MXU), p["w_o"][l].astype(_MXU))
        proj_a = _rms_matmul(h, row(p["norm1"][l]), w_a, f"proj_a{l}")
        gates = _rms_matmul(h, row(p["norm1"][l]), w_g, f"proj_g{l}")
        qkv = _qkv_cast(proj_a, f"qkv_cast{l}")
        c, pooled = _branch_pre(proj_a, conv_w, row(p["conv_dw_b"][l]), f"branch_pre{l}")
        att = _attn_fwd(qkv, f"attn_fwd{l}")
        h2 = _mix_fwd(h, gates, c, pooled, att, mw, f"mix_fwd{l}")
        u3 = _rms_matmul(h2, row(p["norm2"][l]), p["w_up"][l], f"ffn_up{l}", split_out=True)
        h3 = _ffn_fwd(u3, h2, cw3, cb2, p["w_down"][l], f"ffn_fwd{l}")
        saved.append(dict(h=h, w_a=w_a, w_g=w_g, conv_w=conv_w, cw3=cw3, cb2=cb2, mw=mw, proj_a=proj_a, gates=gates,
                          qkv=qkv, c=c, pooled=pooled, att=att, h2=h2, u3=u3))
        h = h3

    dh, loss, d_final = _loss_bwd(h, target, row(p["final_norm"]), "loss_bwd")

    g = {k: [None] * N_LAYERS for k in ("norm1", "w_in", "conv_dw_w", "conv_dw_b", "conv_ln_g", "conv_ln_b", "w_conv_out",
                                        "b_conv_out", "w_pool_grp", "pool_scale", "w_attn_out", "w_o", "norm2", "w_up",
                                        "ffn_dw_w", "ffn_dw_b", "w_down")}
    for l in reversed(range(N_LAYERS)):
        s = saved[l]
        dc3, act, dwb = _ffn_bwd1(dh, s["u3"], s["cw3"], s["cb2"], p["w_down"][l], f"ffn_bwd1_{l}")
        g["w_down"][l] = _tdot_acc(act, dh, f"dw_down{l}")
        g["ffn_dw_w"][l] = jnp.transpose(dwb[:, :FFN_K, :], (1, 0, 2)).reshape(FFN_K, 2 * D_FF)
        g["ffn_dw_b"][l] = dwb[:, FFN_K, :].reshape(2 * D_FF)
        du3, dh2, xn2, dg2 = _ffn_bwd2(dc3, s["cw3"], p["w_up"][l], s["h2"], dh, row(p["norm2"][l]), f"ffn_bwd2_{l}")
        g["norm2"][l] = dg2[0]
        g["w_up"][l] = _tdot_acc(xn2, du3, f"dw_up{l}")
        (dgates, dc, dpooled, datt, mixed, s_act, dya, dybs, dyc, acc_d, acc_c) = _mix_bwd(
            dh2, s["gates"], s["c"], s["pooled"], s["att"], s["mw"], f"mix_bwd{l}")
        g["w_o"][l] = _tdot_acc(mixed, dh2, f"dw_o{l}")
        g["w_conv_out"][l] = _tdot_acc(s_act, dya, f"dw_conv_out{l}")
        g["w_pool_grp"][l] = _pool_blockdiag_grad(_tdot_acc(s["pooled"], dybs, f"dw_pool{l}"))
        g["w_attn_out"][l] = _tdot_acc(s["att"], dyc, f"dw_attn_out{l}")
        g["b_conv_out"][l] = acc_d[0]
        g["pool_scale"][l] = acc_d[1]
        g["conv_ln_g"][l] = acc_c[0]
        g["conv_ln_b"][l] = acc_c[1]
        g["conv_dw_b"][l] = acc_c[2]
        dq, dk, dv = _attn_bwd(s["qkv"], s["att"], datt, f"attn_bwd{l}")
        dfront, dcw = _branch_bwd(dc, dpooled, s["proj_a"], s["conv_w"], f"branch_bwd{l}")
        g["conv_dw_w"][l] = dcw[:CONV_K]
        dh, hn, dg1 = _in_bwd(dfront, dq, dk, dv, dgates, s["w_a"], s["w_g"], s["h"], dh2, row(p["norm1"][l]),
                              f"in_bwd{l}")
        g["norm1"][l] = dg1[0]
        g["w_in"][l] = jnp.concatenate(
            [_tdot_acc(hn, part, f"dw_in{l}_{k}") for k, part in enumerate((dfront, dq, dk, dv, dgates))], axis=1)

    grads = {k: jnp.stack(v) for k, v in g.items()}
    grads["final_norm"] = d_final[0]
    grads["meta"] = dh[PAD_ROWS:BLK]
    return loss, dh[BLK:], grads


def _axes():
    return lax.axis_index("x"), lax.axis_index("y"), lax.axis_index("c")


def _allgather_xy(blobs, name):
    nt = len(blobs)
    flips = ((1, 0), (0, 1), (1, 1))

    def body(*refs):
        srcs, dsts = refs[:nt], refs[nt : 2 * nt]
        send_sems, recv_sems, local_sems = refs[2 * nt :]
        x, y, c = _axes()
        mine = 2 * x + y
        started = []
        for t in range(nt):
            own = pltpu.make_async_copy(srcs[t], dsts[t].at[mine], local_sems.at[t])
            own.start()
            started.append(own)
        for t in range(nt):
            for f, (fx, fy) in enumerate(flips):
                px = 1 - x if fx else x
                py = 1 - y if fy else y
                out = pltpu.make_async_remote_copy(srcs[t], dsts[t].at[mine], send_sems.at[t, f], recv_sems.at[t, f],
                                                   device_id=(px, py, c), device_id_type=MESH)
                out.start()
                arriving = pltpu.make_async_remote_copy(srcs[t], dsts[t].at[2 * px + py], send_sems.at[t, f],
                                                        recv_sems.at[t, f], device_id=(px, py, c), device_id_type=MESH)
                started.append((out, arriving))
        for cp in started:
            if isinstance(cp, tuple):
                cp[0].wait_send()
                cp[1].wait_recv()
            else:
                cp.wait()

    return _pcall(
        body,
        in_specs=[ANY] * nt,
        out_specs=[ANY] * nt,
        out_shape=[jax.ShapeDtypeStruct((4,) + b.shape, b.dtype) for b in blobs],
        scratch_shapes=[pltpu.SemaphoreType.DMA((nt, 3)), pltpu.SemaphoreType.DMA((nt, 3)), pltpu.SemaphoreType.DMA((nt,))],
        compiler_params=pltpu.CompilerParams(has_side_effects=True),
        name=name,
    )(*blobs)


def _grad_exchange(gblob, name):
    shape = gblob.shape[1:]

    def body(src, dst, send_sems, recv_sems, local_sem):
        x, y, c = _axes()
        sibling = (x, y, 1 - c)
        chips = [(1 - x, y), (x, 1 - y), (1 - x, 1 - y)]

        def slot(px, py, pc):
            return dst.at[4 * px + 2 * py + pc]

        def copy(k, source, target_slot, to):
            return pltpu.make_async_remote_copy(source, target_slot, send_sems.at[k], recv_sems.at[k], device_id=to,
                                                device_id_type=MESH)

        mine = src.at[2 * x + y]
        own = pltpu.make_async_copy(mine, slot(x, y, c), local_sem)
        own.start()
        first = [copy(0, mine, slot(x, y, c), sibling)]
        first += [copy(1 + j, src.at[2 * px + py], slot(x, y, c), (px, py, c)) for j, (px, py) in enumerate(chips)]
        for cp in first:
            cp.start()
        passed = [copy(4 + j, slot(px, py, c), slot(px, py, c), sibling) for j, (px, py) in enumerate(chips)]
        for j, (px, py) in enumerate(chips):
            copy(1 + j, mine, slot(px, py, c), (px, py, c)).wait_recv()
            passed[j].start()
        copy(0, mine, slot(x, y, 1 - c), sibling).wait_recv()
        for j, (px, py) in enumerate(chips):
            copy(4 + j, mine, slot(px, py, 1 - c), sibling).wait_recv()
        for cp in first + passed:
            cp.wait_send()
        own.wait()

    return _pcall(
        body,
        in_specs=[ANY],
        out_specs=ANY,
        out_shape=jax.ShapeDtypeStruct((8,) + shape, gblob.dtype),
        scratch_shapes=[pltpu.SemaphoreType.DMA((7,)), pltpu.SemaphoreType.DMA((7,)), pltpu.SemaphoreType.DMA],
        compiler_params=pltpu.CompilerParams(has_side_effects=True),
        name=name,
    )(gblob)


def _adamw(w, m, v, gslots, name):
    rows, lanes = w.shape

    def body(w_ref, m_ref, v_ref, gs_ref, g_ref, d_ref, nm_ref, nv_ref):
        g = gs_ref[0]
        for k in range(1, 8):
            g = g + gs_ref[k]
        w_, m_, v_ = w_ref[...], m_ref[...], v_ref[...]
        m_new = ADAM_B1 * m_ + (1.0 - ADAM_B1) * g
        v_new = ADAM_B2 * v_ + (1.0 - ADAM_B2) * (g * g)
        m_hat = m_new / (1.0 - ADAM_B1 ** ADAM_STEP)
        v_hat = v_new / (1.0 - ADAM_B2 ** ADAM_STEP)
        g_ref[...] = g
        d_ref[...] = -ADAM_LR * (m_hat / (jnp.sqrt(v_hat) + ADAM_EPS) + ADAM_WD * w_)
        nm_ref[...] = m_new
        nv_ref[...] = v_new

    blk = pl.BlockSpec((ADAM_ROWS, lanes), lambda i: (i, 0))
    return _pcall(
        body,
        grid=(rows // ADAM_ROWS,),
        in_specs=[blk, blk, blk, pl.BlockSpec((8, ADAM_ROWS, lanes), lambda i: (0, i, 0))],
        out_specs=[blk] * 4,
        out_shape=[jax.ShapeDtypeStruct((rows, lanes), F32)] * 4,
        compiler_params=_cp(("parallel",)),
        name=name,
    )(w, m, v, gslots)


_PARAMS = (
    ("meta", (N_META, D), 1),
    ("norm1", (N_LAYERS, D), None),
    ("w_in", (N_LAYERS, D, D_A + D_G), 2),
    ("conv_dw_w", (N_LAYERS, CONV_K, CONV_CH), 2),
    ("conv_dw_b", (N_LAYERS, CONV_CH), None),
    ("conv_ln_g", (N_LAYERS, CONV_CH), None),
    ("conv_ln_b", (N_LAYERS, CONV_CH), None),
    ("w_conv_out", (N_LAYERS, CONV_CH, D), 2),
    ("b_conv_out", (N_LAYERS, D), None),
    ("w_pool_grp", (N_LAYERS, len(POOL_WINDOWS), POOL_GC, D // len(POOL_WINDOWS)), 3),
    ("pool_scale", (N_LAYERS, D), None),
    ("w_attn_out", (N_LAYERS, ATT_W, D), 2),
    ("w_o", (N_LAYERS, D, D), 1),
    ("norm2", (N_LAYERS, D), None),
    ("w_up", (N_LAYERS, D, 2 * D_FF), 2),
    ("ffn_dw_w", (N_LAYERS, FFN_K, 2 * D_FF), 2),
    ("ffn_dw_b", (N_LAYERS, 2 * D_FF), None),
    ("w_down", (N_LAYERS, D_FF, D), 1),
    ("final_norm", (D,), None),
)
_BIG = ("w_in", "w_conv_out", "w_pool_grp", "w_attn_out", "w_o", "w_up", "w_down")
_SMALL_SHARDED = ("meta", "conv_dw_w", "ffn_dw_w")
_SHARD_AXIS = {n: ax for n, _, ax in _PARAMS}


def _size(shape):
    n = 1
    for d in shape:
        n *= d
    return n


def _pack(parts, lanes, row_multiple):
    flat = jnp.concatenate([a.reshape(-1) for a in parts])
    rows = -(-flat.shape[0] // lanes)
    rows = -(-rows // row_multiple) * row_multiple
    flat = jnp.pad(flat, (0, rows * lanes - flat.shape[0]))
    return flat.reshape(rows, lanes)


def _unpack(blob, shapes):
    flat = blob.reshape(-1)
    out, off = [], 0
    for s in shapes:
        n = _size(s)
        out.append(flat[off : off + n].reshape(s))
        off += n
    return out


def _shard(a, ax, s):
    n = a.shape[ax] // 4
    return lax.slice_in_dim(a, s * n, (s + 1) * n, axis=ax)


def kernel(x, meta, norm1, w_in, conv_dw_w, conv_dw_b, conv_ln_g, conv_ln_b, w_conv_out, b_conv_out, w_pool_grp, pool_scale, w_attn_out, w_o, norm2, w_up, ffn_dw_w, ffn_dw_b, w_down, final_norm, loss_target, m_meta, m_norm1, m_w_in, m_conv_dw_w, m_conv_dw_b, m_conv_ln_g, m_conv_ln_b, m_w_conv_out, m_b_conv_out, m_w_pool_grp, m_pool_scale, m_w_attn_out, m_w_o, m_norm2, m_w_up, m_ffn_dw_w, m_ffn_dw_b, m_w_down, m_final_norm, v_meta, v_norm1, v_w_in, v_conv_dw_w, v_conv_dw_b, v_conv_ln_g, v_conv_ln_b, v_w_conv_out, v_b_conv_out, v_w_pool_grp, v_pool_scale, v_w_attn_out, v_w_o, v_norm2, v_w_up, v_ffn_dw_w, v_ffn_dw_b, v_w_down, v_final_norm):
    names = [n for n, _, _ in _PARAMS]
    w_loc = dict(zip(names, (meta, norm1, w_in, conv_dw_w, conv_dw_b, conv_ln_g, conv_ln_b, w_conv_out, b_conv_out, w_pool_grp, pool_scale, w_attn_out, w_o, norm2, w_up, ffn_dw_w, ffn_dw_b, w_down, final_norm)))
    m_loc = dict(zip(names, (m_meta, m_norm1, m_w_in, m_conv_dw_w, m_conv_dw_b, m_conv_ln_g, m_conv_ln_b, m_w_conv_out, m_b_conv_out, m_w_pool_grp, m_pool_scale, m_w_attn_out, m_w_o, m_norm2, m_w_up, m_ffn_dw_w, m_ffn_dw_b, m_w_down, m_final_norm)))
    v_loc = dict(zip(names, (v_meta, v_norm1, v_w_in, v_conv_dw_w, v_conv_dw_b, v_conv_ln_g, v_conv_ln_b, v_w_conv_out, v_b_conv_out, v_w_pool_grp, v_pool_scale, v_w_attn_out, v_w_o, v_norm2, v_w_up, v_ffn_dw_w, v_ffn_dw_b, v_w_down, v_final_norm)))

    big = _pack([w_loc[n].astype(_MXU) for n in _BIG], BLOB_LANES, 16)
    small = _pack([w_loc[n] for n in _SMALL_SHARDED], BLK, 8)
    big4, small4 = _allgather_xy([big, small], "allgather_weights")
    full = {n: w_loc[n] for n, _, ax in _PARAMS if ax is None}
    for group, blob4 in ((_BIG, big4), (_SMALL_SHARDED, small4)):
        per_chip = [_unpack(blob4[s], [w_loc[n].shape for n in group]) for s in range(4)]
        for k, n in enumerate(group):
            full[n] = jnp.concatenate([per_chip[s][k] for s in range(4)], axis=_SHARD_AXIS[n])

    loss, grad_x, grads = _local_step(x[0], loss_target[0], full)

    def owner_blob(src, s):
        return _pack([src[n] if ax is None else _shard(src[n], ax, s) for n, _, ax in _PARAMS], BLOB_LANES, ADAM_ROWS)

    gblob = jnp.stack([owner_blob(grads, s) for s in range(4)])
    gslots = _grad_exchange(gblob, "grad_exchange")
    local_shapes = [w_loc[n].shape for n in names]
    pk = lambda d: _pack([d[n] for n in names], BLOB_LANES, ADAM_ROWS)
    g_b, d_b, nm_b, nv_b = _adamw(pk(w_loc), pk(m_loc), pk(v_loc), gslots, "adamw")
    loss = lax.psum(loss[0, 0], ("x", "y", "c"))
    outs = [loss, grad_x[None]]
    for blob in (g_b, d_b, nm_b, nv_b):
        outs.extend(_unpack(blob, local_shapes))
    return tuple(outs)
```

```python
import functools

import jax
import jax.numpy as jnp
from jax import lax
from jax.experimental import pallas as pl
from jax.experimental.pallas import tpu as pltpu

F32 = jnp.float32
_MXU = jnp.bfloat16

D = 1024
N_META = 16
BLK = 128
CH = 2 * BLK
ATTN_UNROLLS = (8, 2, 1)
PAD_ROWS = BLK - N_META
N_LAYERS = 2
CONV_CH = 256
CONV_K = 31
POOL_CH = 256
POOL_WINDOWS = (2, 4, 8, 16)
POOL_GC = 64
ATT_W = 512
NH = 4
HD = 128
D_A = 2 * CONV_CH + POOL_CH + 3 * ATT_W
D_G = 3 * D
D_FF = 3 * D
FFN_K = 3
EPS = 1e-6
Q_SCALE = HD ** -0.5
LOG2E = 1.4426950408889634
LN2 = 0.6931471805599453

TR = 384
TB = 128
HALO = 32
HALO8 = 8
TC = 768
TF = 512
VMEM_LIMIT = 48 * 1024 * 1024
DW_STEPS = 12

ADAM_LR = 0.001
ADAM_B1 = 0.9
ADAM_B2 = 0.999
ADAM_EPS = 1e-08
ADAM_WD = 0.01
ADAM_STEP = 10

GELU_C0 = 0.7978845608028654
GELU_C1 = 0.044715

MESH = pl.DeviceIdType.MESH
ANY = pl.BlockSpec(memory_space=pl.ANY)

BLOB_LANES = 1024


def _pcall(body, **kw):
    return pl.pallas_call(body, **kw)


def _cp(sem):
    return pltpu.CompilerParams(dimension_semantics=sem, vmem_limit_bytes=VMEM_LIMIT)


def _dot(a, b):
    return jnp.dot(a.astype(_MXU), b.astype(_MXU), preferred_element_type=F32)


def _dot_t(a, b):
    return lax.dot_general(a.astype(_MXU), b.astype(_MXU), (((1,), (1,)), ((), ())), preferred_element_type=F32)


def _tdot(a, b):
    return lax.dot_general(a.astype(_MXU), b.astype(_MXU), (((0,), (0,)), ((), ())), preferred_element_type=F32)


def _sig(x):
    return 1.0 / (1.0 + jnp.exp(-x))


def _rms_fwd(x, g):
    r = lax.rsqrt(jnp.mean(x * x, axis=-1, keepdims=True) + EPS)
    n = x * r
    return r, n, n * g


def _rms_bwd(dy, r, n, g):
    dgain = jnp.sum(dy * n, axis=0, keepdims=True)
    dn = dy * g
    dx = r * (dn - n * jnp.mean(dn * n, axis=-1, keepdims=True))
    return dx, dgain


def _row_ids(tile_rows, i, shape):
    return i * tile_rows + lax.broadcasted_iota(jnp.int32, shape, 0)


def _rms_matmul(h, gain, w, name, split_out=False):
    R, N = h.shape[0], w.shape[1]
    nj = N // TC
    half = nj // 2

    def body(h_ref, g_ref, w_ref, o_ref, xn_ref):
        @pl.when(pl.program_id(1) == 0)
        def _():
            _, _, y = _rms_fwd(h_ref[...], g_ref[...])
            xn_ref[...] = y.astype(xn_ref.dtype)

        o_ref[...] = jnp.dot(xn_ref[...], w_ref[...], preferred_element_type=F32)

    if split_out:
        out_shape = jax.ShapeDtypeStruct((2, R, N // 2), F32)
        out_spec = pl.BlockSpec((None, TR, TC), lambda i, j: (j // half, i, j % half))
    else:
        out_shape = jax.ShapeDtypeStruct((R, N), F32)
        out_spec = pl.BlockSpec((TR, TC), lambda i, j: (i, j))
    return _pcall(
        body,
        grid=(R // TR, nj),
        in_specs=[
            pl.BlockSpec((TR, D), lambda i, j: (i, 0)),
            pl.BlockSpec((1, D), lambda i, j: (0, 0)),
            pl.BlockSpec((D, TC), lambda i, j: (0, j)),
        ],
        out_specs=out_spec,
        out_shape=out_shape,
        scratch_shapes=[pltpu.VMEM((TR, D), _MXU)],
        compiler_params=_cp(("parallel", "arbitrary")),
        name=name,
    )(h, gain, w.astype(_MXU))


def _qkv_cast(proj_a, name):
    R = proj_a.shape[0]

    def body(p_ref, o_ref):
        col = lax.broadcasted_iota(jnp.int32, (1, TC), 1) + pl.program_id(1) * TC
        sc = jnp.where(col < ATT_W, Q_SCALE * LOG2E, 1.0).astype(F32)
        o_ref[...] = (p_ref[...] * sc).astype(o_ref.dtype)

    return _pcall(
        body,
        grid=(R // TR, 2),
        in_specs=[pl.BlockSpec((TR, TC), lambda i, j: (i, j + 1))],
        out_specs=pl.BlockSpec((TR, TC), lambda i, j: (i, j)),
        out_shape=jax.ShapeDtypeStruct((R, 3 * ATT_W), _MXU),
        compiler_params=_cp(("parallel", "parallel")),
        name=name,
    )(proj_a)


def _sum_mat(w):
    rowi = lax.broadcasted_iota(jnp.int32, (w, w), 0)
    coli = lax.broadcasted_iota(jnp.int32, (w, w), 1)
    return (rowi > coli).astype(_MXU)


def _hi_lo_rows(x):
    hi = x.astype(_MXU)
    lo = (x - hi.astype(F32)).astype(_MXU)
    return jnp.concatenate([hi, lo], axis=0)


def _fold_sums(r):
    return r[:BLK] + r[BLK:]


def _attn_logs(z2, valid):
    m = jnp.minimum(z2, 0.0)
    d = m - z2
    t = jnp.log2(1.0 + jnp.exp2(m + d))
    lk = d - t
    if valid is not None:
        lk = jnp.where(valid, lk, 0.0)
    return lk, m - t


def _attn_weights(log_beta, r, valid, s_after):
    log_a = log_beta + r + s_after
    if valid is not None:
        log_a = jnp.where(valid, log_a, -1e30)
    return jnp.exp2(log_a)


def _attn_walk(i, phases, n_free, carry):
    ci = jnp.maximum(i - 1, 0) // 2

    def advance(states, stages, cr):
        for ph in stages:
            for u in range(len(states)):
                states[u], cr = ph(states[u], cr)
        return states, cr

    def at(c):
        return pl.multiple_of(BLK + c * CH, BLK)

    ends = [phases[0](at(ci), CH, True), phases[0](0, BLK, True)]
    ends, _ = advance(ends, phases[1 : 1 + n_free], None)
    _, carry = advance(ends[:1], phases[1 + n_free :], carry)
    left = ci
    for unroll in ATTN_UNROLLS:
        def group(t, cr, unroll=unroll, left=left):
            states = [phases[0](at(left - 1 - unroll * t - u), CH, False) for u in range(unroll)]
            return advance(states, phases[1:], cr)[1]

        carry = lax.fori_loop(0, left // unroll, group, carry)
        left = left % unroll
    return advance(ends[1:], phases[1 + n_free :], carry)[1]


def _attn_mask(i, off, w):
    qpos = i * BLK + lax.broadcasted_iota(jnp.int32, (BLK, w), 0)
    kpos = off + lax.broadcasted_iota(jnp.int32, (BLK, w), 1)
    return jnp.logical_and(kpos < qpos, kpos >= PAD_ROWS)


def _attn_fwd(qkv, name):
    R = qkv.shape[0]
    nb = R // BLK

    def body(q_ref, k_ref, v_ref, o_ref):
        i = pl.program_id(1)
        q = q_ref[...]
        sms = {CH: _sum_mat(CH), BLK: _sum_mat(BLK)}

        def scores(off, w, masked):
            valid = _attn_mask(i, off, w) if masked else None
            return off, w, valid, _dot_t(q, k_ref[pl.ds(off, w), :])

        def sums(state, carry):
            off, w, valid, z = state
            lk, log_beta = _attn_logs(z, valid)
            return (off, w, valid, log_beta, _dot(lk, sms[w]), jnp.sum(lk, axis=1, keepdims=True)), carry

        def output(state, carry):
            off, w, valid, log_beta, r, total = state
            acc, s_after = carry
            a = _attn_weights(log_beta, r, valid, s_after)
            return None, (acc + _dot(a, v_ref[pl.ds(off, w), :]), s_after + total)

        acc, _ = _attn_walk(i, (scores, sums, output), 1, (jnp.zeros((BLK, HD), F32), jnp.zeros((BLK, 1), F32)))
        o_ref[...] = acc

    return _pcall(
        body,
        grid=(NH, nb),
        in_specs=[
            pl.BlockSpec((BLK, HD), lambda h, i: (i, h)),
            pl.BlockSpec((R, HD), lambda h, i: (0, NH + h)),
            pl.BlockSpec((R, HD), lambda h, i: (0, 2 * NH + h)),
        ],
        out_specs=pl.BlockSpec((BLK, HD), lambda h, i: (i, h)),
        out_shape=jax.ShapeDtypeStruct((R, ATT_W), F32),
        compiler_params=_cp(("parallel", "arbitrary")),
        name=name,
    )(qkv, qkv, qkv)


def _pool_consts(i):
    lane = lax.broadcasted_iota(jnp.int32, (1, POOL_CH), 1)
    wsize = jnp.where(lane < POOL_GC, 2.0, jnp.where(lane < 2 * POOL_GC, 4.0, jnp.where(lane < 3 * POOL_GC, 8.0, 16.0)))
    return lane, wsize


def _pool_div(rows, wsize):
    pos1 = (rows - (PAD_ROWS - 1)).astype(F32)
    return jnp.clip(pos1, 1.0, wsize)


def _lane_select(lane, s2, s4, s8, s16):
    return jnp.where(lane < POOL_GC, s2, jnp.where(lane < 2 * POOL_GC, s4, jnp.where(lane < 3 * POOL_GC, s8, s16)))


def _branch_pre(proj_a, conv_w, conv_b, name):
    R = proj_a.shape[0]
    nh = TR // HALO

    def body(t_ref, h_ref, w_ref, b_ref, c_ref, p_ref, gext, pext):
        i = pl.program_id(0)
        t = t_ref[...]
        hl = jnp.where(i == 0, 0.0, h_ref[...])
        gext[pl.ds(0, HALO), :] = hl[:, :CONV_CH] * _sig(hl[:, CONV_CH : 2 * CONV_CH])
        gext[pl.ds(HALO, TR), :] = t[:, :CONV_CH] * _sig(t[:, CONV_CH : 2 * CONV_CH])
        acc = jnp.zeros((TR, CONV_CH), F32) + b_ref[...]
        for k in range(CONV_K):
            acc = acc + w_ref[pl.ds(k, 1), :] * gext[pl.ds(HALO - (CONV_K - 1) + k, TR), :]
        c_ref[...] = acc

        p = t[:, 2 * CONV_CH :]
        pext[pl.ds(0, HALO), :] = hl[:, 2 * CONV_CH :]
        pext[pl.ds(HALO, TR), :] = p

        def back(k):
            return pext[pl.ds(HALO - k, TR), :]

        s2 = p + back(1)
        s4 = s2 + back(2) + back(3)
        s8 = s4 + back(4) + back(5) + back(6) + back(7)
        s16 = s8
        for k in range(8, 16):
            s16 = s16 + back(k)
        lane, wsize = _pool_consts(i)
        div = _pool_div(_row_ids(TR, i, (TR, POOL_CH)), wsize)
        p_ref[...] = (_lane_select(lane, s2, s4, s8, s16) / div - p).astype(p_ref.dtype)

    return _pcall(
        body,
        grid=(R // TR,),
        in_specs=[
            pl.BlockSpec((TR, TC), lambda i: (i, 0)),
            pl.BlockSpec((HALO, TC), lambda i: (jnp.maximum(i * nh - 1, 0), 0)),
            pl.BlockSpec((HALO, CONV_CH), lambda i: (0, 0)),
            pl.BlockSpec((1, CONV_CH), lambda i: (0, 0)),
        ],
        out_specs=[pl.BlockSpec((TR, CONV_CH), lambda i: (i, 0)), pl.BlockSpec((TR, POOL_CH), lambda i: (i, 0))],
        out_shape=[jax.ShapeDtypeStruct((R, CONV_CH), F32), jax.ShapeDtypeStruct((R, POOL_CH), _MXU)],
        scratch_shapes=[pltpu.VMEM((TR + HALO, CONV_CH), F32), pltpu.VMEM((TR + HALO, POOL_CH), F32)],
        compiler_params=_cp(("parallel",)),
        name=name,
    )(proj_a, proj_a, conv_w, conv_b)


def _mix_values(c, pooled, att, gates, ln_g, ln_b, w_co, b_co, w_bd, p_scale, w_ao):
    mu = jnp.mean(c, axis=-1, keepdims=True)
    xc = c - mu
    rstd = lax.rsqrt(jnp.mean(xc * xc, axis=-1, keepdims=True) + EPS)
    nl = xc * rstd
    ln = nl * ln_g + ln_b
    sg = _sig(ln)
    s = ln * sg
    ya = _dot(s, w_co) + b_co
    ybr = _dot(pooled, w_bd)
    yb = ybr * p_scale
    yc = _dot(att, w_ao)
    g = _sig(gates)
    g0, g1, g2 = g[:, :D], g[:, D : 2 * D], g[:, 2 * D :]
    mixed = g0 * ya + g1 * yb + g2 * yc
    return dict(rstd=rstd, nl=nl, ln=ln, sg=sg, s=s, ya=ya, ybr=ybr, yb=yb, yc=yc, g0=g0, g1=g1, g2=g2, mixed=mixed)


_MIX_W_SPECS = [
    pl.BlockSpec((1, CONV_CH), lambda i: (0, 0)),
    pl.BlockSpec((1, CONV_CH), lambda i: (0, 0)),
    pl.BlockSpec((CONV_CH, D), lambda i: (0, 0)),
    pl.BlockSpec((1, D), lambda i: (0, 0)),
    pl.BlockSpec((POOL_CH, D), lambda i: (0, 0)),
    pl.BlockSpec((1, D), lambda i: (0, 0)),
    pl.BlockSpec((ATT_W, D), lambda i: (0, 0)),
    pl.BlockSpec((D, D), lambda i: (0, 0)),
]


def _mix_act_specs(t):
    return [
        pl.BlockSpec((t, D_G), lambda i: (i, 0)),
        pl.BlockSpec((t, CONV_CH), lambda i: (i, 0)),
        pl.BlockSpec((t, POOL_CH), lambda i: (i, 0)),
        pl.BlockSpec((t, ATT_W), lambda i: (i, 0)),
    ]


def _mix_fwd(h, gates, c, pooled, att, mw, name):
    R = h.shape[0]

    def body(h_ref, g_ref, c_ref, p_ref, a_ref, lg, lb, wco, bco, wbd, ps, wao, wo, o_ref):
        v = _mix_values(c_ref[...], p_ref[...], a_ref[...], g_ref[...], lg[...], lb[...], wco[...], bco[...],
                        wbd[...], ps[...], wao[...])
        out = h_ref[...] + _dot(v["mixed"], wo[...])
        rows = _row_ids(TB, pl.program_id(0), (TB, D))
        o_ref[...] = jnp.where(rows >= PAD_ROWS, out, 0.0)

    return _pcall(
        body,
        grid=(R // TB,),
        in_specs=[pl.BlockSpec((TB, D), lambda i: (i, 0))] + _mix_act_specs(TB) + _MIX_W_SPECS,
        out_specs=pl.BlockSpec((TB, D), lambda i: (i, 0)),
        out_shape=jax.ShapeDtypeStruct((R, D), F32),
        compiler_params=_cp(("parallel",)),
        name=name,
    )(h, gates, c, pooled, att, *mw)


def _ffn_conv(ut, uh, cw_ref, cb_ref, ext):
    ext[:, pl.ds(0, HALO8), :] = uh
    ext[:, pl.ds(HALO8, TR), :] = ut
    um1 = ext[:, pl.ds(HALO8 - 1, TR), :]
    um2 = ext[:, pl.ds(HALO8 - 2, TR), :]
    cw = cw_ref[...]
    conv = cw[0][:, None, :] * um2 + cw[1][:, None, :] * um1 + cw[2][:, None, :] * ut + cb_ref[...][:, None, :]
    return conv, um1, um2


def _gelu_parts(x):
    th = jnp.tanh(GELU_C0 * (x + GELU_C1 * x * x * x))
    return th, 0.5 * x * (1.0 + th)


def _ffn_in_specs(nrow8, order):
    n8 = TR // HALO8
    return [
        pl.BlockSpec((2, TR, TF), lambda *g: (0, order(*g)[0], order(*g)[1])),
        pl.BlockSpec((2, HALO8, TF), lambda *g: (0, jnp.maximum(order(*g)[0] * n8 - 1, 0), order(*g)[1])),
        pl.BlockSpec((FFN_K, 2, TF), lambda *g: (0, 0, order(*g)[1])),
        pl.BlockSpec((2, TF), lambda *g: (0, order(*g)[1])),
    ]


def _ffn_fwd(u3, h2, cw, cb, w_down, name):
    R = h2.shape[0]
    nj = D_FF // TF

    def body(u_ref, uh_ref, cw_ref, cb_ref, wd_ref, h_ref, o_ref, ext, acc):
        i, j = pl.program_id(0), pl.program_id(1)
        uh = jnp.where(i == 0, 0.0, uh_ref[...])
        conv, _, _ = _ffn_conv(u_ref[...], uh, cw_ref, cb_ref, ext)
        _, a = _gelu_parts(conv[0])
        part = _dot(a * conv[1], wd_ref[...])

        @pl.when(j == 0)
        def _():
            acc[...] = part

        @pl.when(j > 0)
        def _():
            acc[...] += part

        @pl.when(j == nj - 1)
        def _():
            rows = _row_ids(TR, i, (TR, D))
            o_ref[...] = jnp.where(rows >= PAD_ROWS, h_ref[...] + acc[...], 0.0)

    return _pcall(
        body,
        grid=(R // TR, nj),
        in_specs=_ffn_in_specs(R // HALO8, lambda i, j: (i, j))
        + [pl.BlockSpec((TF, D), lambda i, j: (j, 0)), pl.BlockSpec((TR, D), lambda i, j: (i, 0))],
        out_specs=pl.BlockSpec((TR, D), lambda i, j: (i, 0)),
        out_shape=jax.ShapeDtypeStruct((R, D), F32),
        scratch_shapes=[pltpu.VMEM((2, TR + HALO8, TF), F32), pltpu.VMEM((TR, D), F32)],
        compiler_params=_cp(("parallel", "arbitrary")),
        name=name,
    )(u3, u3, cw, cb, w_down.astype(_MXU), h2)


def _loss_bwd(h, target, gain, name):
    R = h.shape[0]

    def body(h_ref, t_ref, g_ref, dh_ref, loss_ref, dg_ref):
        i = pl.program_id(0)

        @pl.when(i == 0)
        def _():
            loss_ref[...] = jnp.zeros_like(loss_ref)
            dg_ref[...] = jnp.zeros_like(dg_ref)
            dh_ref[...] = jnp.zeros_like(dh_ref)

        @pl.when(i > 0)
        def _():
            g = g_ref[...]
            r, n, y = _rms_fwd(h_ref[...], g)
            e = y - t_ref[...]
            loss_ref[...] += (0.5 / D) * jnp.sum(jnp.sum(e * e, axis=1, keepdims=True), axis=0, keepdims=True)
            dx, dgain = _rms_bwd(e * (1.0 / D), r, n, g)
            dg_ref[...] += dgain
            dh_ref[...] = dx

    return _pcall(
        body,
        grid=(R // BLK,),
        in_specs=[
            pl.BlockSpec((BLK, D), lambda i: (i, 0)),
            pl.BlockSpec((BLK, D), lambda i: (jnp.maximum(i - 1, 0), 0)),
            pl.BlockSpec((1, D), lambda i: (0, 0)),
        ],
        out_specs=[
            pl.BlockSpec((BLK, D), lambda i: (i, 0)),
            pl.BlockSpec((1, 1), lambda i: (0, 0)),
            pl.BlockSpec((1, D), lambda i: (0, 0)),
        ],
        out_shape=[
            jax.ShapeDtypeStruct((R, D), F32),
            jax.ShapeDtypeStruct((1, 1), F32),
            jax.ShapeDtypeStruct((1, D), F32),
        ],
        compiler_params=_cp(("arbitrary",)),
        name=name,
    )(h, target, gain)


DW_BLOCK_BYTES = 6 * 1024 * 1024


def _tdot_acc(a, b, name):
    R, M = a.shape
    split = b.ndim == 3
    width = b.shape[-1]
    N = 2 * width if split else width
    tm = min(M, 512)
    tk = R // DW_STEPS
    tn = max(t for t in range(BLK, width + 1, BLK) if width % t == 0 and tk * t * b.dtype.itemsize <= DW_BLOCK_BYTES)
    per = width // tn

    def body(a_ref, b_ref, o_ref):
        part = _tdot(a_ref[...], b_ref[...])

        @pl.when(pl.program_id(2) == 0)
        def _():
            o_ref[...] = part

        @pl.when(pl.program_id(2) > 0)
        def _():
            o_ref[...] += part

    if split:
        b_spec = pl.BlockSpec((None, tk, tn), lambda m, n, k: (n // per, k, n % per))
    else:
        b_spec = pl.BlockSpec((tk, tn), lambda m, n, k: (k, n))
    return _pcall(
        body,
        grid=(M // tm, N // tn, R // tk),
        in_specs=[pl.BlockSpec((tk, tm), lambda m, n, k: (k, m)), b_spec],
        out_specs=pl.BlockSpec((tm, tn), lambda m, n, k: (m, n)),
        out_shape=jax.ShapeDtypeStruct((M, N), F32),
        compiler_params=_cp(("parallel", "parallel", "arbitrary")),
        name=name,
    )(a, b)


def _ffn_bwd1(dh3, u3, cw, cb, w_down, name):
    R = dh3.shape[0]
    nj = D_FF // TF

    def body(u_ref, uh_ref, cw_ref, cb_ref, wd_ref, dh_ref, dc_ref, dwd_ref, dwb_ref, ext):
        j, i = pl.program_id(0), pl.program_id(1)
        ut = u_ref[...]
        uh = jnp.where(i == 0, 0.0, uh_ref[...])
        conv, um1, um2 = _ffn_conv(ut, uh, cw_ref, cb_ref, ext)
        gt, val = conv[0], conv[1]
        th, a = _gelu_parts(gt)
        dh = dh_ref[...].astype(_MXU)
        dact = _dot_t(dh, wd_ref[...])
        dgelu = 0.5 * (1.0 + th) + 0.5 * gt * (1.0 - th * th) * (GELU_C0 * (1.0 + 3.0 * GELU_C1 * gt * gt))
        dgt = dact * val * dgelu
        dval = dact * a
        dc_ref[0] = dgt
        dc_ref[1] = dval

        @pl.when(i == 0)
        def _():
            dwb_ref[...] = jnp.zeros_like(dwb_ref)
            dwd_ref[...] = jnp.zeros_like(dwd_ref)

        dwd_ref[...] += _tdot(a * val, dh)

        for half, dcv in ((0, dgt), (1, dval)):
            dwb_ref[half, pl.ds(0, 1), :] += jnp.sum(um2[half] * dcv, axis=0, keepdims=True)
            dwb_ref[half, pl.ds(1, 1), :] += jnp.sum(um1[half] * dcv, axis=0, keepdims=True)
            dwb_ref[half, pl.ds(2, 1), :] += jnp.sum(ut[half] * dcv, axis=0, keepdims=True)
            dwb_ref[half, pl.ds(3, 1), :] += jnp.sum(dcv, axis=0, keepdims=True)

    return _pcall(
        body,
        grid=(nj, R // TR),
        in_specs=_ffn_in_specs(R // HALO8, lambda j, i: (i, j))
        + [pl.BlockSpec((TF, D), lambda j, i: (j, 0)), pl.BlockSpec((TR, D), lambda j, i: (i, 0))],
        out_specs=[
            pl.BlockSpec((2, TR, TF), lambda j, i: (0, i, j)),
            pl.BlockSpec((TF, D), lambda j, i: (j, 0)),
            pl.BlockSpec((2, 8, TF), lambda j, i: (0, 0, j)),
        ],
        out_shape=[
            jax.ShapeDtypeStruct((2, R, D_FF), F32),
            jax.ShapeDtypeStruct((D_FF, D), F32),
            jax.ShapeDtypeStruct((2, 8, D_FF), F32),
        ],
        scratch_shapes=[pltpu.VMEM((2, TR + HALO8, TF), F32)],
        compiler_params=_cp(("parallel", "arbitrary")),
        name=name,
    )(u3, u3, cw, cb, w_down.astype(_MXU), dh3)


def _ffn_bwd2(dc3, cw, w_up, h2, dh3, gain, name):
    R = h2.shape[0]
    nj = D_FF // TF
    n8 = TR // HALO8
    last8 = R // HALO8 - 1
    ni = R // TR

    def body(dc_ref, dn_ref, cw_ref, wg_ref, wv_ref, h_ref, dh_ref, g_ref, du_ref, o_ref, xn_ref, dg_ref, ext, acc):
        i, j = pl.program_id(0), pl.program_id(1)
        dc = dc_ref[...]
        ext[:, pl.ds(0, TR), :] = dc
        ext[:, pl.ds(TR, HALO8), :] = jnp.where(i == ni - 1, 0.0, dn_ref[...])
        cw = cw_ref[...]
        du = (cw[2][:, None, :] * dc + cw[1][:, None, :] * ext[:, pl.ds(1, TR), :]
              + cw[0][:, None, :] * ext[:, pl.ds(2, TR), :])
        du_ref[...] = du.astype(du_ref.dtype)
        part = _dot_t(du[0], wg_ref[...]) + _dot_t(du[1], wv_ref[...])

        @pl.when(j == 0)
        def _():
            acc[...] = part

        @pl.when(j > 0)
        def _():
            acc[...] += part

        @pl.when(jnp.logical_and(i == 0, j == 0))
        def _():
            dg_ref[...] = jnp.zeros_like(dg_ref)

        @pl.when(j == nj - 1)
        def _():
            g = g_ref[...]
            r, n, y = _rms_fwd(h_ref[...], g)
            dx, dgain = _rms_bwd(acc[...], r, n, g)
            dg_ref[...] += dgain
            rows = _row_ids(TR, i, (TR, D))
            o_ref[...] = jnp.where(rows >= PAD_ROWS, dh_ref[...] + dx, 0.0)
            xn_ref[...] = y.astype(xn_ref.dtype)

    w_up = w_up.astype(_MXU)
    return _pcall(
        body,
        grid=(ni, nj),
        in_specs=[
            pl.BlockSpec((2, TR, TF), lambda i, j: (0, i, j)),
            pl.BlockSpec((2, HALO8, TF), lambda i, j: (0, jnp.minimum((i + 1) * n8, last8), j)),
            pl.BlockSpec((FFN_K, 2, TF), lambda i, j: (0, 0, j)),
            pl.BlockSpec((D, TF), lambda i, j: (0, j)),
            pl.BlockSpec((D, TF), lambda i, j: (0, nj + j)),
            pl.BlockSpec((TR, D), lambda i, j: (i, 0)),
            pl.BlockSpec((TR, D), lambda i, j: (i, 0)),
            pl.BlockSpec((1, D), lambda i, j: (0, 0)),
        ],
        out_specs=[
            pl.BlockSpec((2, TR, TF), lambda i, j: (0, i, j)),
            pl.BlockSpec((TR, D), lambda i, j: (i, 0)),
            pl.BlockSpec((TR, D), lambda i, j: (i, 0)),
            pl.BlockSpec((1, D), lambda i, j: (0, 0)),
        ],
        out_shape=[
            jax.ShapeDtypeStruct((2, R, D_FF), _MXU),
            jax.ShapeDtypeStruct((R, D), F32),
            jax.ShapeDtypeStruct((R, D), _MXU),
            jax.ShapeDtypeStruct((1, D), F32),
        ],
        scratch_shapes=[pltpu.VMEM((2, TR + HALO8, TF), F32), pltpu.VMEM((TR, D), F32)],
        compiler_params=_cp(("arbitrary", "arbitrary")),
        name=name,
    )(dc3, dc3, cw, w_up, w_up, h2, dh3, gain)


def _mix_bwd(dh2, gates, c, pooled, att, mw, name):
    R = dh2.shape[0]

    def body(dh_ref, g_ref, c_ref, p_ref, a_ref, lg, lb, wco, bco, wbd, ps, wao, wo,
             dg_ref, dc_ref, dp_ref, da_ref, mx_ref, s_ref, dya_ref, dyb_ref, dyc_ref, accd_ref, accc_ref):
        v = _mix_values(c_ref[...], p_ref[...], a_ref[...], g_ref[...], lg[...], lb[...], wco[...], bco[...],
                        wbd[...], ps[...], wao[...])
        dmix = _dot_t(dh_ref[...], wo[...])
        for k, (gk, yk) in enumerate(((v["g0"], v["ya"]), (v["g1"], v["yb"]), (v["g2"], v["yc"]))):
            dg_ref[:, k * D : (k + 1) * D] = dmix * yk * gk * (1.0 - gk)
        dya = dmix * v["g0"]
        dyb = dmix * v["g1"]
        dyc = dmix * v["g2"]
        ds = _dot_t(dya, wco[...])
        ln, sg, nl = v["ln"], v["sg"], v["nl"]
        dln = ds * (sg * (1.0 + ln * (1.0 - sg)))
        dn = dln * lg[...]
        dc = v["rstd"] * (dn - jnp.mean(dn, axis=-1, keepdims=True) - nl * jnp.mean(dn * nl, axis=-1, keepdims=True))
        dybs = dyb * ps[...]
        dc_ref[...] = dc
        dp_ref[...] = _dot_t(dybs, wbd[...])
        da_ref[...] = _dot_t(dyc, wao[...])
        mx_ref[...] = v["mixed"].astype(mx_ref.dtype)
        s_ref[...] = v["s"].astype(s_ref.dtype)
        dya_ref[...] = dya.astype(dya_ref.dtype)
        dyb_ref[...] = dybs.astype(dyb_ref.dtype)
        dyc_ref[...] = dyc.astype(dyc_ref.dtype)

        @pl.when(pl.program_id(0) == 0)
        def _():
            accd_ref[...] = jnp.zeros_like(accd_ref)
            accc_ref[...] = jnp.zeros_like(accc_ref)

        accd_ref[pl.ds(0, 1), :] += jnp.sum(dya, axis=0, keepdims=True)
        accd_ref[pl.ds(1, 1), :] += jnp.sum(dyb * v["ybr"], axis=0, keepdims=True)
        accc_ref[pl.ds(0, 1), :] += jnp.sum(dln * nl, axis=0, keepdims=True)
        accc_ref[pl.ds(1, 1), :] += jnp.sum(dln, axis=0, keepdims=True)
        accc_ref[pl.ds(2, 1), :] += jnp.sum(dc, axis=0, keepdims=True)

    def row(w):
        return pl.BlockSpec((TB, w), lambda i: (i, 0))

    return _pcall(
        body,
        grid=(R // TB,),
        in_specs=[row(D)] + _mix_act_specs(TB) + _MIX_W_SPECS,
        out_specs=[row(D_G), row(CONV_CH), row(POOL_CH), row(ATT_W), row(D), row(CONV_CH), row(D), row(D), row(D),
                   pl.BlockSpec((8, D), lambda i: (0, 0)), pl.BlockSpec((8, CONV_CH), lambda i: (0, 0))],
        out_shape=[
            jax.ShapeDtypeStruct((R, D_G), F32),
            jax.ShapeDtypeStruct((R, CONV_CH), F32),
            jax.ShapeDtypeStruct((R, POOL_CH), F32),
            jax.ShapeDtypeStruct((R, ATT_W), F32),
            jax.ShapeDtypeStruct((R, D), _MXU),
            jax.ShapeDtypeStruct((R, CONV_CH), _MXU),
            jax.ShapeDtypeStruct((R, D), _MXU),
            jax.ShapeDtypeStruct((R, D), _MXU),
            jax.ShapeDtypeStruct((R, D), _MXU),
            jax.ShapeDtypeStruct((8, D), F32),
            jax.ShapeDtypeStruct((8, CONV_CH), F32),
        ],
        compiler_params=_cp(("arbitrary",)),
        name=name,
    )(dh2, gates, c, pooled, att, *mw)


def _attn_bwd(qkv, att, datt, name):
    R = qkv.shape[0]
    nb = R // BLK

    def body(q_ref, k_ref, v_ref, o_ref, do_ref, dq_ref, dk_ref, dv_ref):
        i = pl.program_id(1)

        @pl.when(i == 0)
        def _():
            dk_ref[...] = jnp.zeros_like(dk_ref)
            dv_ref[...] = jnp.zeros_like(dv_ref)

        q = q_ref[...]
        dob = do_ref[...].astype(_MXU)
        dof = dob.astype(F32)
        sms = {CH: _sum_mat(CH), BLK: _sum_mat(BLK)}
        e_all = jnp.sum(dof * o_ref[...], axis=1, keepdims=True)

        def scores(off, w, masked):
            valid = _attn_mask(i, off, w) if masked else None
            z = _dot_t(q, k_ref[pl.ds(off, w), :])
            return off, w, valid, z, _dot_t(dob, v_ref[pl.ds(off, w), :])

        def sums(state, carry):
            off, w, valid, z, da = state
            lk, log_beta = _attn_logs(z, valid)
            return (off, w, valid, log_beta, _dot(lk, sms[w]), jnp.sum(lk, axis=1, keepdims=True), da), carry

        def weights(state, carry):
            off, w, valid, log_beta, r, total, da = state
            dq, s_after, e_done = carry
            ab = _attn_weights(log_beta, r, valid, s_after).astype(_MXU)
            e = ab.astype(F32) * da
            re = jnp.dot(_hi_lo_rows(e), sms[w], preferred_element_type=F32)
            return (off, w, valid, jnp.exp2(log_beta), ab, e, re), (dq, s_after + total, e_done)

        def grads(state, carry):
            off, w, valid, beta, ab, e, re = state
            dq, s_after, e_done = carry
            dz = e - beta * ((e_all - e_done) - _fold_sums(re))
            if valid is not None:
                dz = jnp.where(valid, dz, 0.0)
            dzb = (dz * LN2).astype(_MXU)
            dk_ref[pl.ds(off, w), :] += _tdot(dzb, q)
            dv_ref[pl.ds(off, w), :] += _tdot(ab, dob)
            dq = dq + jnp.dot(dzb, k_ref[pl.ds(off, w), :], preferred_element_type=F32)
            return None, (dq, s_after, e_done + jnp.sum(e, axis=1, keepdims=True))

        zero = jnp.zeros((BLK, 1), F32)
        dq, _, _ = _attn_walk(i, (scores, sums, weights, grads), 1, (jnp.zeros((BLK, HD), F32), zero, zero))
        dq_ref[...] = dq * (Q_SCALE * LOG2E)

    blk = pl.BlockSpec((BLK, HD), lambda h, i: (i, h))
    col = pl.BlockSpec((R, HD), lambda h, i: (0, h))
    return _pcall(
        body,
        grid=(NH, nb),
        in_specs=[
            blk,
            pl.BlockSpec((R, HD), lambda h, i: (0, NH + h)),
            pl.BlockSpec((R, HD), lambda h, i: (0, 2 * NH + h)),
            blk,
            blk,
        ],
        out_specs=[blk, col, col],
        out_shape=[jax.ShapeDtypeStruct((R, ATT_W), F32)] * 3,
        compiler_params=pltpu.CompilerParams(dimension_semantics=("arbitrary", "arbitrary"),
                                             vmem_limit_bytes=56 * 1024 * 1024),
        name=name,
    )(qkv, qkv, qkv, att, datt)


def _branch_bwd(dc, dpooled, proj_a, conv_w, name):
    R = dc.shape[0]
    nh = TR // HALO
    last = R // HALO - 1
    ni = R // TR

    def body(dc_ref, dcn_ref, dp_ref, dpn_ref, t_ref, h_ref, w_ref, o_ref, dcw_ref, gext, dcext, eext):
        i = pl.program_id(0)
        is_last = i == ni - 1
        t = t_ref[...]
        hl = jnp.where(i == 0, 0.0, h_ref[...])
        a = t[:, :CONV_CH]
        sg = _sig(t[:, CONV_CH : 2 * CONV_CH])
        gext[pl.ds(0, HALO), :] = hl[:, :CONV_CH] * _sig(hl[:, CONV_CH : 2 * CONV_CH])
        gext[pl.ds(HALO, TR), :] = a * sg
        dct = dc_ref[...]
        dcext[pl.ds(0, TR), :] = dct
        dcext[pl.ds(TR, HALO), :] = jnp.where(is_last, 0.0, dcn_ref[...])

        @pl.when(i == 0)
        def _():
            dcw_ref[...] = jnp.zeros_like(dcw_ref)

        dglu = jnp.zeros((TR, CONV_CH), F32)
        for k in range(CONV_K):
            dglu = dglu + w_ref[pl.ds(k, 1), :] * dcext[pl.ds(CONV_K - 1 - k, TR), :]
            dcw_ref[pl.ds(k, 1), :] += jnp.sum(gext[pl.ds(HALO - (CONV_K - 1) + k, TR), :] * dct, axis=0, keepdims=True)
        o_ref[:, :CONV_CH] = dglu * sg
        o_ref[:, CONV_CH : 2 * CONV_CH] = dglu * a * sg * (1.0 - sg)

        lane, wsize = _pool_consts(i)
        dpt = dp_ref[...]
        eext[pl.ds(0, TR), :] = dpt / _pool_div(_row_ids(TR, i, (TR, POOL_CH)), wsize)
        nxt = dpn_ref[...] / _pool_div(_row_ids(TR, i + 1, (HALO, POOL_CH)), wsize)
        eext[pl.ds(TR, HALO), :] = jnp.where(is_last, 0.0, nxt)

        def fwd(k):
            return eext[pl.ds(k, TR), :]

        s2 = fwd(0) + fwd(1)
        s4 = s2 + fwd(2) + fwd(3)
        s8 = s4 + fwd(4) + fwd(5) + fwd(6) + fwd(7)
        s16 = s8
        for k in range(8, 16):
            s16 = s16 + fwd(k)
        o_ref[:, 2 * CONV_CH :] = _lane_select(lane, s2, s4, s8, s16) - dpt

    def nxt_spec(w):
        return pl.BlockSpec((HALO, w), lambda i: (jnp.minimum((i + 1) * nh, last), 0))

    return _pcall(
        body,
        grid=(ni,),
        in_specs=[
            pl.BlockSpec((TR, CONV_CH), lambda i: (i, 0)),
            nxt_spec(CONV_CH),
            pl.BlockSpec((TR, POOL_CH), lambda i: (i, 0)),
            nxt_spec(POOL_CH),
            pl.BlockSpec((TR, TC), lambda i: (i, 0)),
            pl.BlockSpec((HALO, TC), lambda i: (jnp.maximum(i * nh - 1, 0), 0)),
            pl.BlockSpec((HALO, CONV_CH), lambda i: (0, 0)),
        ],
        out_specs=[pl.BlockSpec((TR, TC), lambda i: (i, 0)), pl.BlockSpec((HALO, CONV_CH), lambda i: (0, 0))],
        out_shape=[jax.ShapeDtypeStruct((R, TC), F32), jax.ShapeDtypeStruct((HALO, CONV_CH), F32)],
        scratch_shapes=[
            pltpu.VMEM((TR + HALO, CONV_CH), F32),
            pltpu.VMEM((TR + HALO, CONV_CH), F32),
            pltpu.VMEM((TR + HALO, POOL_CH), F32),
        ],
        compiler_params=_cp(("arbitrary",)),
        name=name,
    )(dc, dc, dpooled, dpooled, proj_a, proj_a, conv_w)


def _in_bwd(dfront, dq, dk, dv, dgates, w_a, w_g, h, dh2, gain, name):
    R = h.shape[0]
    nj = 1 + D_G // TC
    ni = R // TR
    w_a = w_a.astype(_MXU)
    w_f, w_q = w_a[:, :TC], w_a[:, TC : TC + ATT_W]
    w_k, w_v = w_a[:, TC + ATT_W : TC + 2 * ATT_W], w_a[:, TC + 2 * ATT_W :]

    def body(df_ref, dq_ref, dk_ref, dv_ref, dg_ref, wf_ref, wq_ref, wk_ref, wv_ref, wg_ref, h_ref, dh_ref, g_ref,
             o_ref, hn_ref, dgain_ref, acc):
        i, j = pl.program_id(0), pl.program_id(1)

        @pl.when(j == 0)
        def _():
            acc[...] = (_dot_t(df_ref[...], wf_ref[...]) + _dot_t(dq_ref[...], wq_ref[...])
                        + _dot_t(dk_ref[...], wk_ref[...]) + _dot_t(dv_ref[...], wv_ref[...]))

        @pl.when(j > 0)
        def _():
            acc[...] += _dot_t(dg_ref[...], wg_ref[...])

        @pl.when(jnp.logical_and(i == 0, j == 0))
        def _():
            dgain_ref[...] = jnp.zeros_like(dgain_ref)

        @pl.when(j == nj - 1)
        def _():
            g = g_ref[...]
            r, n, y = _rms_fwd(h_ref[...], g)
            dx, dgain = _rms_bwd(acc[...], r, n, g)
            dgain_ref[...] += dgain
            rows = _row_ids(TR, i, (TR, D))
            o_ref[...] = jnp.where(rows >= PAD_ROWS, dh_ref[...] + dx, 0.0)
            hn_ref[...] = y.astype(hn_ref.dtype)

    def row(w):
        return pl.BlockSpec((TR, w), lambda i, j: (i, 0))

    def whole(w):
        return pl.BlockSpec((D, w), lambda i, j: (0, 0))

    def gcol(i, j):
        return jnp.maximum(j - 1, 0)

    return _pcall(
        body,
        grid=(ni, nj),
        in_specs=[
            row(TC), row(ATT_W), row(ATT_W), row(ATT_W),
            pl.BlockSpec((TR, TC), lambda i, j: (i, gcol(i, j))),
            whole(TC), whole(ATT_W), whole(ATT_W), whole(ATT_W),
            pl.BlockSpec((D, TC), lambda i, j: (0, gcol(i, j))),
            row(D), row(D),
            pl.BlockSpec((1, D), lambda i, j: (0, 0)),
        ],
        out_specs=[row(D), row(D), pl.BlockSpec((1, D), lambda i, j: (0, 0))],
        out_shape=[jax.ShapeDtypeStruct((R, D), F32), jax.ShapeDtypeStruct((R, D), _MXU), jax.ShapeDtypeStruct((1, D), F32)],
        scratch_shapes=[pltpu.VMEM((TR, D), F32)],
        compiler_params=_cp(("arbitrary", "arbitrary")),
        name=name,
    )(dfront, dq, dk, dv, dgates, w_f, w_q, w_k, w_v, w_g.astype(_MXU), h, dh2, gain)


def _pool_blockdiag(w_grp):
    eye = jnp.eye(len(POOL_WINDOWS), dtype=w_grp.dtype)
    return jnp.einsum("gcd,gh->gchd", w_grp, eye).reshape(POOL_CH, D)


def _pool_blockdiag_grad(dw_bd):
    d4 = dw_bd.reshape(len(POOL_WINDOWS), POOL_GC, len(POOL_WINDOWS), D // len(POOL_WINDOWS))
    return jnp.stack([d4[g, :, g, :] for g in range(len(POOL_WINDOWS))])


def _local_step(x, target, p):
    row = lambda a: a.reshape(1, -1)
    h = jnp.concatenate([jnp.zeros((PAD_ROWS, D), F32), p["meta"], x], axis=0)
    saved = []
    for l in range(N_LAYERS):
        w_a, w_g = p["w_in"][l][:, :D_A], p["w_in"][l][:, D_A:]
        conv_w = jnp.concatenate([p["conv_dw_w"][l], jnp.zeros((1, CONV_CH), F32)], axis=0)
        cw3 = p["ffn_dw_w"][l].reshape(FFN_K, 2, D_FF)
        cb2 = p["ffn_dw_b"][l].reshape(2, D_FF)
        mw = (row(p["conv_ln_g"][l]), row(p["conv_ln_b"][l]), p["w_conv_out"][l].astype(_MXU), row(p["b_conv_out"][l]),
              _pool_blockdiag(p["w_pool_grp"][l]).astype(_MXU), row(p["pool_scale"][l]),
              p["w_attn_out"][l].astype(_MXU), p["w_o"][l].astype(_MXU))
        proj_a = _rms_matmul(h, row(p["norm1"][l]), w_a, f"proj_a{l}")
        gates = _rms_matmul(h, row(p["norm1"][l]), w_g, f"proj_g{l}")
        qkv = _qkv_cast(proj_a, f"qkv_cast{l}")
        c, pooled = _branch_pre(proj_a, conv_w, row(p["conv_dw_b"][l]), f"branch_pre{l}")
        att = _attn_fwd(qkv, f"attn_fwd{l}")
        h2 = _mix_fwd(h, gates, c, pooled, att, mw, f"mix_fwd{l}")
        u3 = _rms_matmul(h2, row(p["norm2"][l]), p["w_up"][l], f"ffn_up{l}", split_out=True)
        h3 = _ffn_fwd(u3, h2, cw3, cb2, p["w_down"][l], f"ffn_fwd{l}")
        saved.append(dict(h=h, w_a=w_a, w_g=w_g, conv_w=conv_w, cw3=cw3, cb2=cb2, mw=mw, proj_a=proj_a, gates=gates,
                          qkv=qkv, c=c, pooled=pooled, att=att, h2=h2, u3=u3))
        h = h3

    dh, loss, d_final = _loss_bwd(h, target, row(p["final_norm"]), "loss_bwd")

    g = {k: [None] * N_LAYERS for k in ("norm1", "w_in", "conv_dw_w", "conv_dw_b", "conv_ln_g", "conv_ln_b", "w_conv_out",
                                        "b_conv_out", "w_pool_grp", "pool_scale", "w_attn_out", "w_o", "norm2", "w_up",
                                        "ffn_dw_w", "ffn_dw_b", "w_down")}
    for l in reversed(range(N_LAYERS)):
        s = saved[l]
        dc3, g["w_down"][l], dwb = _ffn_bwd1(dh, s["u3"], s["cw3"], s["cb2"], p["w_down"][l], f"ffn_bwd1_{l}")
        g["ffn_dw_w"][l] = jnp.transpose(dwb[:, :FFN_K, :], (1, 0, 2)).reshape(FFN_K, 2 * D_FF)
        g["ffn_dw_b"][l] = dwb[:, FFN_K, :].reshape(2 * D_FF)
        du3, dh2, xn2, dg2 = _ffn_bwd2(dc3, s["cw3"], p["w_up"][l], s["h2"], dh, row(p["norm2"][l]), f"ffn_bwd2_{l}")
        g["norm2"][l] = dg2[0]
        g["w_up"][l] = _tdot_acc(xn2, du3, f"dw_up{l}")
        (dgates, dc, dpooled, datt, mixed, s_act, dya, dybs, dyc, acc_d, acc_c) = _mix_bwd(
            dh2, s["gates"], s["c"], s["pooled"], s["att"], s["mw"], f"mix_bwd{l}")
        g["w_o"][l] = _tdot_acc(mixed, dh2, f"dw_o{l}")
        g["w_conv_out"][l] = _tdot_acc(s_act, dya, f"dw_conv_out{l}")
        g["w_pool_grp"][l] = _pool_blockdiag_grad(_tdot_acc(s["pooled"], dybs, f"dw_pool{l}"))
        g["w_attn_out"][l] = _tdot_acc(s["att"], dyc, f"dw_attn_out{l}")
        g["b_conv_out"][l] = acc_d[0]
        g["pool_scale"][l] = acc_d[1]
        g["conv_ln_g"][l] = acc_c[0]
        g["conv_ln_b"][l] = acc_c[1]
        g["conv_dw_b"][l] = acc_c[2]
        dq, dk, dv = _attn_bwd(s["qkv"], s["att"], datt, f"attn_bwd{l}")
        dfront, dcw = _branch_bwd(dc, dpooled, s["proj_a"], s["conv_w"], f"branch_bwd{l}")
        g["conv_dw_w"][l] = dcw[:CONV_K]
        dh, hn, dg1 = _in_bwd(dfront, dq, dk, dv, dgates, s["w_a"], s["w_g"], s["h"], dh2, row(p["norm1"][l]),
                              f"in_bwd{l}")
        g["norm1"][l] = dg1[0]
        g["w_in"][l] = jnp.concatenate(
            [_tdot_acc(hn, part, f"dw_in{l}_{k}") for k, part in enumerate((dfront, dq, dk, dv, dgates))], axis=1)

    grads = {k: jnp.stack(v) for k, v in g.items()}
    grads["final_norm"] = d_final[0]
    grads["meta"] = dh[PAD_ROWS:BLK]
    return loss, dh[BLK:], grads


def _axes():
    return lax.axis_index("x"), lax.axis_index("y"), lax.axis_index("c")


def _allgather_xy(blobs, name):
    nt = len(blobs)
    flips = ((1, 0), (0, 1), (1, 1))

    def body(*refs):
        srcs, dsts = refs[:nt], refs[nt : 2 * nt]
        send_sems, recv_sems, local_sems = refs[2 * nt :]
        x, y, c = _axes()
        mine = 2 * x + y
        started = []
        for t in range(nt):
            own = pltpu.make_async_copy(srcs[t], dsts[t].at[mine], local_sems.at[t])
            own.start()
            started.append(own)
        for t in range(nt):
            for f, (fx, fy) in enumerate(flips):
                px = 1 - x if fx else x
                py = 1 - y if fy else y
                out = pltpu.make_async_remote_copy(srcs[t], dsts[t].at[mine], send_sems.at[t, f], recv_sems.at[t, f],
                                                   device_id=(px, py, c), device_id_type=MESH)
                out.start()
                arriving = pltpu.make_async_remote_copy(srcs[t], dsts[t].at[2 * px + py], send_sems.at[t, f],
                                                        recv_sems.at[t, f], device_id=(px, py, c), device_id_type=MESH)
                started.append((out, arriving))
        for cp in started:
            if isinstance(cp, tuple):
                cp[0].wait_send()
                cp[1].wait_recv()
            else:
                cp.wait()

    return _pcall(
        body,
        in_specs=[ANY] * nt,
        out_specs=[ANY] * nt,
        out_shape=[jax.ShapeDtypeStruct((4,) + b.shape, b.dtype) for b in blobs],
        scratch_shapes=[pltpu.SemaphoreType.DMA((nt, 3)), pltpu.SemaphoreType.DMA((nt, 3)), pltpu.SemaphoreType.DMA((nt,))],
        compiler_params=pltpu.CompilerParams(has_side_effects=True),
        name=name,
    )(*blobs)


def _grad_exchange(contribs, name):
    nt = len(contribs)

    def body(*refs):
        srcs, dsts = refs[:nt], refs[nt : 2 * nt]
        send_sems, recv_sems, local_sems = refs[2 * nt :]
        x, y, c = _axes()
        sibling = (x, y, 1 - c)
        chips = [(1 - x, y), (x, 1 - y), (1 - x, 1 - y)]

        def slot(t, px, py, pc):
            return dsts[t].at[4 * px + 2 * py + pc]

        def copy(t, k, source, target_slot, to):
            return pltpu.make_async_remote_copy(source, target_slot, send_sems.at[t, k], recv_sems.at[t, k],
                                                device_id=to, device_id_type=MESH)

        def mine(t):
            return srcs[t].at[2 * x + y]

        own = [pltpu.make_async_copy(mine(t), slot(t, x, y, c), local_sems.at[t]) for t in range(nt)]
        first = []
        for t in range(nt):
            first.append(copy(t, 0, mine(t), slot(t, x, y, c), sibling))
            first += [copy(t, 1 + j, srcs[t].at[2 * px + py], slot(t, x, y, c), (px, py, c))
                      for j, (px, py) in enumerate(chips)]
        for cp in own + first:
            cp.start()
        passed = []
        for j, (px, py) in enumerate(chips):
            for t in range(nt):
                copy(t, 1 + j, mine(t), slot(t, px, py, c), (px, py, c)).wait_recv()
                fwd = copy(t, 4 + j, slot(t, px, py, c), slot(t, px, py, c), sibling)
                fwd.start()
                passed.append(fwd)
        for t in range(nt):
            copy(t, 0, mine(t), slot(t, x, y, 1 - c), sibling).wait_recv()
            for j, (px, py) in enumerate(chips):
                copy(t, 4 + j, mine(t), slot(t, px, py, 1 - c), sibling).wait_recv()
        for cp in first + passed:
            cp.wait_send()
        for cp in own:
            cp.wait()

    return _pcall(
        body,
        in_specs=[ANY] * nt,
        out_specs=[ANY] * nt,
        out_shape=[jax.ShapeDtypeStruct((8,) + a.shape[1:], a.dtype) for a in contribs],
        scratch_shapes=[pltpu.SemaphoreType.DMA((nt, 7)), pltpu.SemaphoreType.DMA((nt, 7)),
                        pltpu.SemaphoreType.DMA((nt,))],
        compiler_params=pltpu.CompilerParams(has_side_effects=True),
        name=name,
    )(*contribs)


ADAM_BLOCK_BYTES = 768 * 1024


def _adamw(w, m, v, gslots, name):
    shape = w.shape
    cols = shape[-1]
    rows = _size(shape) // cols
    blk = max(r for r in range(8, rows + 1, 8) if rows % r == 0 and r * cols * 4 <= ADAM_BLOCK_BYTES) \
        if rows % 8 == 0 else rows

    def body(w_ref, m_ref, v_ref, gs_ref, g_ref, d_ref, nm_ref, nv_ref):
        g = gs_ref[0]
        for k in range(1, 8):
            g = g + gs_ref[k]
        w_, m_, v_ = w_ref[...], m_ref[...], v_ref[...]
        m_new = ADAM_B1 * m_ + (1.0 - ADAM_B1) * g
        v_new = ADAM_B2 * v_ + (1.0 - ADAM_B2) * (g * g)
        m_hat = m_new / (1.0 - ADAM_B1 ** ADAM_STEP)
        v_hat = v_new / (1.0 - ADAM_B2 ** ADAM_STEP)
        g_ref[...] = g
        d_ref[...] = -ADAM_LR * (m_hat / (jnp.sqrt(v_hat) + ADAM_EPS) + ADAM_WD * w_)
        nm_ref[...] = m_new
        nv_ref[...] = v_new

    spec = pl.BlockSpec((blk, cols), lambda i: (i, 0))
    outs = _pcall(
        body,
        grid=(rows // blk,),
        in_specs=[spec, spec, spec, pl.BlockSpec((8, blk, cols), lambda i: (0, i, 0))],
        out_specs=[spec] * 4,
        out_shape=[jax.ShapeDtypeStruct((rows, cols), F32)] * 4,
        compiler_params=_cp(("parallel",)),
        name=name,
    )(w.reshape(rows, cols), m.reshape(rows, cols), v.reshape(rows, cols), gslots.reshape(8, rows, cols))
    return [o.reshape(shape) for o in outs]


_PARAMS = (
    ("meta", (N_META, D), 1),
    ("norm1", (N_LAYERS, D), None),
    ("w_in", (N_LAYERS, D, D_A + D_G), 2),
    ("conv_dw_w", (N_LAYERS, CONV_K, CONV_CH), 2),
    ("conv_dw_b", (N_LAYERS, CONV_CH), None),
    ("conv_ln_g", (N_LAYERS, CONV_CH), None),
    ("conv_ln_b", (N_LAYERS, CONV_CH), None),
    ("w_conv_out", (N_LAYERS, CONV_CH, D), 2),
    ("b_conv_out", (N_LAYERS, D), None),
    ("w_pool_grp", (N_LAYERS, len(POOL_WINDOWS), POOL_GC, D // len(POOL_WINDOWS)), 3),
    ("pool_scale", (N_LAYERS, D), None),
    ("w_attn_out", (N_LAYERS, ATT_W, D), 2),
    ("w_o", (N_LAYERS, D, D), 1),
    ("norm2", (N_LAYERS, D), None),
    ("w_up", (N_LAYERS, D, 2 * D_FF), 2),
    ("ffn_dw_w", (N_LAYERS, FFN_K, 2 * D_FF), 2),
    ("ffn_dw_b", (N_LAYERS, 2 * D_FF), None),
    ("w_down", (N_LAYERS, D_FF, D), 1),
    ("final_norm", (D,), None),
)
_BIG = ("w_in", "w_conv_out", "w_pool_grp", "w_attn_out", "w_o", "w_up", "w_down")
_SMALL_SHARDED = ("meta", "conv_dw_w", "ffn_dw_w")
_SHARD_AXIS = {n: ax for n, _, ax in _PARAMS}


def _size(shape):
    n = 1
    for d in shape:
        n *= d
    return n


def _pack(parts, lanes, row_multiple):
    flat = jnp.concatenate([a.reshape(-1) for a in parts])
    rows = -(-flat.shape[0] // lanes)
    rows = -(-rows // row_multiple) * row_multiple
    flat = jnp.pad(flat, (0, rows * lanes - flat.shape[0]))
    return flat.reshape(rows, lanes)


def _unpack(blob, shapes):
    flat = blob.reshape(-1)
    out, off = [], 0
    for s in shapes:
        n = _size(s)
        out.append(flat[off : off + n].reshape(s))
        off += n
    return out


def _shard(a, ax, s):
    n = a.shape[ax] // 4
    return lax.slice_in_dim(a, s * n, (s + 1) * n, axis=ax)


def kernel(x, meta, norm1, w_in, conv_dw_w, conv_dw_b, conv_ln_g, conv_ln_b, w_conv_out, b_conv_out, w_pool_grp, pool_scale, w_attn_out, w_o, norm2, w_up, ffn_dw_w, ffn_dw_b, w_down, final_norm, loss_target, m_meta, m_norm1, m_w_in, m_conv_dw_w, m_conv_dw_b, m_conv_ln_g, m_conv_ln_b, m_w_conv_out, m_b_conv_out, m_w_pool_grp, m_pool_scale, m_w_attn_out, m_w_o, m_norm2, m_w_up, m_ffn_dw_w, m_ffn_dw_b, m_w_down, m_final_norm, v_meta, v_norm1, v_w_in, v_conv_dw_w, v_conv_dw_b, v_conv_ln_g, v_conv_ln_b, v_w_conv_out, v_b_conv_out, v_w_pool_grp, v_pool_scale, v_w_attn_out, v_w_o, v_norm2, v_w_up, v_ffn_dw_w, v_ffn_dw_b, v_w_down, v_final_norm):
    names = [n for n, _, _ in _PARAMS]
    w_loc = dict(zip(names, (meta, norm1, w_in, conv_dw_w, conv_dw_b, conv_ln_g, conv_ln_b, w_conv_out, b_conv_out, w_pool_grp, pool_scale, w_attn_out, w_o, norm2, w_up, ffn_dw_w, ffn_dw_b, w_down, final_norm)))
    m_loc = dict(zip(names, (m_meta, m_norm1, m_w_in, m_conv_dw_w, m_conv_dw_b, m_conv_ln_g, m_conv_ln_b, m_w_conv_out, m_b_conv_out, m_w_pool_grp, m_pool_scale, m_w_attn_out, m_w_o, m_norm2, m_w_up, m_ffn_dw_w, m_ffn_dw_b, m_w_down, m_final_norm)))
    v_loc = dict(zip(names, (v_meta, v_norm1, v_w_in, v_conv_dw_w, v_conv_dw_b, v_conv_ln_g, v_conv_ln_b, v_w_conv_out, v_b_conv_out, v_w_pool_grp, v_pool_scale, v_w_attn_out, v_w_o, v_norm2, v_w_up, v_ffn_dw_w, v_ffn_dw_b, v_w_down, v_final_norm)))

    small = _pack([w_loc[n] for n in _SMALL_SHARDED], BLK, 8)
    gathered = _allgather_xy([w_loc[n].astype(_MXU) for n in _BIG] + [small], "allgather_weights")
    full = {n: w_loc[n] for n, _, ax in _PARAMS if ax is None}
    for n, g4 in zip(_BIG, gathered):
        full[n] = jnp.concatenate([g4[s] for s in range(4)], axis=_SHARD_AXIS[n])
    per_chip = [_unpack(gathered[-1][s], [w_loc[n].shape for n in _SMALL_SHARDED]) for s in range(4)]
    for k, n in enumerate(_SMALL_SHARDED):
        full[n] = jnp.concatenate([per_chip[s][k] for s in range(4)], axis=_SHARD_AXIS[n])

    loss, grad_x, grads = _local_step(x[0], loss_target[0], full)

    small_names = [n for n in names if n not in _BIG]

    def small_blob(src, s=None):
        return _pack([src[n] if s is None or _SHARD_AXIS[n] is None else _shard(src[n], _SHARD_AXIS[n], s)
                      for n in small_names], BLOB_LANES, 8)

    contribs = [jnp.stack([_shard(grads[n], _SHARD_AXIS[n], s) for s in range(4)]) for n in _BIG]
    contribs.append(jnp.stack([small_blob(grads, s) for s in range(4)]))
    slots = _grad_exchange(contribs, "grad_exchange")
    results = {n: _adamw(w_loc[n], m_loc[n], v_loc[n], gs, f"adamw_{n}") for n, gs in zip(_BIG, slots)}
    small_out = _adamw(small_blob(w_loc), small_blob(m_loc), small_blob(v_loc), slots[-1], "adamw_small")
    small_shapes = [w_loc[n].shape for n in small_names]
    for k, blob in enumerate(small_out):
        for n, a in zip(small_names, _unpack(blob, small_shapes)):
            results.setdefault(n, [None] * 4)[k] = a
    loss = lax.psum(loss[0, 0], ("x", "y", "c"))
    outs = [loss, grad_x[None]]
    for k in range(4):
        outs.extend(results[n][k] for n in names)
    return tuple(outs)
```

```python
import functools
from typing import Callable, NamedTuple

import jax
import jax.numpy as jnp
from jax import lax
from jax.experimental import pallas as pl
from jax.experimental.pallas import tpu as pltpu

F32 = jnp.float32
_MXU = jnp.bfloat16

D = 1024
N_META = 16
BLK = 128
CH = 2 * BLK
ATTN_UNROLLS = (8, 2, 1)
PAD_ROWS = BLK - N_META
N_LAYERS = 2
CONV_CH = 256
CONV_K = 31
POOL_CH = 256
POOL_WINDOWS = (2, 4, 8, 16)
POOL_GC = 64
ATT_W = 512
NH = 4
HD = 128
D_A = 2 * CONV_CH + POOL_CH + 3 * ATT_W
D_G = 3 * D
D_FF = 3 * D
FFN_K = 3
EPS = 1e-6
Q_SCALE = HD ** -0.5
LOG2E = 1.4426950408889634
LN2 = 0.6931471805599453

TR = 384
TB = 128
HALO = 32
HALO8 = 8
TC = 768
TF = 512
VMEM_LIMIT = 48 * 1024 * 1024
DW_STEPS = 12

ADAM_LR = 0.001
ADAM_B1 = 0.9
ADAM_B2 = 0.999
ADAM_EPS = 1e-08
ADAM_WD = 0.01
ADAM_STEP = 10

GELU_C0 = 0.7978845608028654
GELU_C1 = 0.044715

MESH = pl.DeviceIdType.MESH
ANY = pl.BlockSpec(memory_space=pl.ANY)

BLOB_LANES = 1024


def _pcall(body, **kw):
    return pl.pallas_call(body, **kw)


def _cp(sem):
    return pltpu.CompilerParams(dimension_semantics=sem, vmem_limit_bytes=VMEM_LIMIT)


def _dot(a, b):
    return jnp.dot(a.astype(_MXU), b.astype(_MXU), preferred_element_type=F32)


def _dot_t(a, b):
    return lax.dot_general(a.astype(_MXU), b.astype(_MXU), (((1,), (1,)), ((), ())), preferred_element_type=F32)


def _tdot(a, b):
    return lax.dot_general(a.astype(_MXU), b.astype(_MXU), (((0,), (0,)), ((), ())), preferred_element_type=F32)


def _sig(x):
    return 1.0 / (1.0 + jnp.exp(-x))


def _rms_fwd(x, g):
    r = lax.rsqrt(jnp.mean(x * x, axis=-1, keepdims=True) + EPS)
    n = x * r
    return r, n, n * g


def _rms_bwd(dy, r, n, g):
    dgain = jnp.sum(dy * n, axis=0, keepdims=True)
    dn = dy * g
    dx = r * (dn - n * jnp.mean(dn * n, axis=-1, keepdims=True))
    return dx, dgain


def _row_ids(tile_rows, i, shape):
    return i * tile_rows + lax.broadcasted_iota(jnp.int32, shape, 0)


class _Side(NamedTuple):
    arrays: tuple
    out_shape: tuple
    scratch: tuple
    start: Callable
    finish: Callable


def _hosted_call(body, side, grid, in_specs, out_specs, out_shape, scratch_shapes, name, operands):
    n_in, n_out, n_scr = len(in_specs), len(out_specs), len(scratch_shapes)
    if side is None:
        return _pcall(body, grid=grid, in_specs=in_specs, out_specs=out_specs, out_shape=out_shape,
                      scratch_shapes=scratch_shapes, compiler_params=_cp(("arbitrary",) * len(grid)), name=name)(*operands)
    s_in, s_out = len(side.arrays), len(side.out_shape)

    def hosted(*refs):
        ins, refs = refs[:n_in], refs[n_in:]
        side_ins, refs = refs[:s_in], refs[s_in:]
        outs, refs = refs[:n_out], refs[n_out:]
        side_outs, refs = refs[:s_out], refs[s_out:]
        scratch, side_scratch = refs[:n_scr], refs[n_scr:]
        ids = [pl.program_id(a) for a in range(len(grid))]
        first = functools.reduce(jnp.logical_and, [p == 0 for p in ids])
        last = functools.reduce(jnp.logical_and, [p == n - 1 for p, n in zip(ids, grid)])

        @pl.when(first)
        def _():
            side.start(side_ins, side_outs, side_scratch)

        body(*ins, *outs, *scratch)

        @pl.when(last)
        def _():
            side.finish(side_ins, side_outs, side_scratch)

    return _pcall(
        hosted,
        grid=grid,
        in_specs=list(in_specs) + [ANY] * s_in,
        out_specs=list(out_specs) + [ANY] * s_out,
        out_shape=list(out_shape) + list(side.out_shape),
        scratch_shapes=list(scratch_shapes) + list(side.scratch),
        compiler_params=_cp(("arbitrary",) * len(grid)),
        name=name,
    )(*operands, *side.arrays)


def _rms_matmul(h, gain, w, name, split_out=False, side=None):
    R, N = h.shape[0], w.shape[1]
    nj = N // TC
    tr = R // DW_STEPS
    half = nj // 2

    def body(h_ref, g_ref, w_ref, o_ref, xn_ref):
        @pl.when(pl.program_id(1) == 0)
        def _():
            _, _, y = _rms_fwd(h_ref[...], g_ref[...])
            xn_ref[...] = y.astype(xn_ref.dtype)

        o_ref[...] = jnp.dot(xn_ref[...], w_ref[...], preferred_element_type=F32)

    if split_out:
        out_shape = jax.ShapeDtypeStruct((2, R, N // 2), F32)
        out_spec = pl.BlockSpec((None, tr, TC), lambda i, j: (j // half, i, j % half))
    else:
        out_shape = jax.ShapeDtypeStruct((R, N), F32)
        out_spec = pl.BlockSpec((tr, TC), lambda i, j: (i, j))
    res = _hosted_call(
        body, side,
        grid=(R // tr, nj),
        in_specs=[
            pl.BlockSpec((tr, D), lambda i, j: (i, 0)),
            pl.BlockSpec((1, D), lambda i, j: (0, 0)),
            pl.BlockSpec((D, TC), lambda i, j: (0, j)),
        ],
        out_specs=[out_spec],
        out_shape=[out_shape],
        scratch_shapes=[pltpu.VMEM((tr, D), _MXU)],
        name=name,
        operands=(h, gain, w.astype(_MXU)),
    )
    return res[0] if side is None else res


def _qkv_cast(proj_a, name):
    R = proj_a.shape[0]

    def body(p_ref, o_ref):
        col = lax.broadcasted_iota(jnp.int32, (1, TC), 1) + pl.program_id(1) * TC
        sc = jnp.where(col < ATT_W, Q_SCALE * LOG2E, 1.0).astype(F32)
        o_ref[...] = (p_ref[...] * sc).astype(o_ref.dtype)

    return _pcall(
        body,
        grid=(R // TR, 2),
        in_specs=[pl.BlockSpec((TR, TC), lambda i, j: (i, j + 1))],
        out_specs=pl.BlockSpec((TR, TC), lambda i, j: (i, j)),
        out_shape=jax.ShapeDtypeStruct((R, 3 * ATT_W), _MXU),
        compiler_params=_cp(("parallel", "parallel")),
        name=name,
    )(proj_a)


def _sum_mat(w):
    rowi = lax.broadcasted_iota(jnp.int32, (w, w), 0)
    coli = lax.broadcasted_iota(jnp.int32, (w, w), 1)
    return (rowi > coli).astype(_MXU)


def _hi_lo_rows(x):
    hi = x.astype(_MXU)
    lo = (x - hi.astype(F32)).astype(_MXU)
    return jnp.concatenate([hi, lo], axis=0)


def _fold_sums(r):
    return r[:BLK] + r[BLK:]


def _attn_logs(z2, valid):
    m = jnp.minimum(z2, 0.0)
    d = m - z2
    t = jnp.log2(1.0 + jnp.exp2(m + d))
    lk = d - t
    if valid is not None:
        lk = jnp.where(valid, lk, 0.0)
    return lk, m - t


def _attn_weights(log_beta, r, valid, s_after):
    log_a = log_beta + r + s_after
    if valid is not None:
        log_a = jnp.where(valid, log_a, -1e30)
    return jnp.exp2(log_a)


def _attn_walk(i, phases, n_free, carry):
    ci = jnp.maximum(i - 1, 0) // 2

    def advance(states, stages, cr):
        for ph in stages:
            for u in range(len(states)):
                states[u], cr = ph(states[u], cr)
        return states, cr

    def at(c):
        return pl.multiple_of(BLK + c * CH, BLK)

    ends = [phases[0](at(ci), CH, True), phases[0](0, BLK, True)]
    ends, _ = advance(ends, phases[1 : 1 + n_free], None)
    _, carry = advance(ends[:1], phases[1 + n_free :], carry)
    left = ci
    for unroll in ATTN_UNROLLS:
        def group(t, cr, unroll=unroll, left=left):
            states = [phases[0](at(left - 1 - unroll * t - u), CH, False) for u in range(unroll)]
            return advance(states, phases[1:], cr)[1]

        carry = lax.fori_loop(0, left // unroll, group, carry)
        left = left % unroll
    return advance(ends[1:], phases[1 + n_free :], carry)[1]


def _attn_mask(i, off, w):
    qpos = i * BLK + lax.broadcasted_iota(jnp.int32, (BLK, w), 0)
    kpos = off + lax.broadcasted_iota(jnp.int32, (BLK, w), 1)
    return jnp.logical_and(kpos < qpos, kpos >= PAD_ROWS)


def _attn_fwd(qkv, name):
    R = qkv.shape[0]
    nb = R // BLK

    def body(q_ref, k_ref, v_ref, o_ref):
        i = pl.program_id(1)
        q = q_ref[...]
        sms = {CH: _sum_mat(CH), BLK: _sum_mat(BLK)}

        def scores(off, w, masked):
            valid = _attn_mask(i, off, w) if masked else None
            return off, w, valid, _dot_t(q, k_ref[pl.ds(off, w), :])

        def sums(state, carry):
            off, w, valid, z = state
            lk, log_beta = _attn_logs(z, valid)
            return (off, w, valid, log_beta, _dot(lk, sms[w]), jnp.sum(lk, axis=1, keepdims=True)), carry

        def output(state, carry):
            off, w, valid, log_beta, r, total = state
            acc, s_after = carry
            a = _attn_weights(log_beta, r, valid, s_after)
            return None, (acc + _dot(a, v_ref[pl.ds(off, w), :]), s_after + total)

        acc, _ = _attn_walk(i, (scores, sums, output), 1, (jnp.zeros((BLK, HD), F32), jnp.zeros((BLK, 1), F32)))
        o_ref[...] = acc

    return _pcall(
        body,
        grid=(NH, nb),
        in_specs=[
            pl.BlockSpec((BLK, HD), lambda h, i: (i, h)),
            pl.BlockSpec((R, HD), lambda h, i: (0, NH + h)),
            pl.BlockSpec((R, HD), lambda h, i: (0, 2 * NH + h)),
        ],
        out_specs=pl.BlockSpec((BLK, HD), lambda h, i: (i, h)),
        out_shape=jax.ShapeDtypeStruct((R, ATT_W), F32),
        compiler_params=_cp(("parallel", "arbitrary")),
        name=name,
    )(qkv, qkv, qkv)


def _pool_consts(i):
    lane = lax.broadcasted_iota(jnp.int32, (1, POOL_CH), 1)
    wsize = jnp.where(lane < POOL_GC, 2.0, jnp.where(lane < 2 * POOL_GC, 4.0, jnp.where(lane < 3 * POOL_GC, 8.0, 16.0)))
    return lane, wsize


def _pool_div(rows, wsize):
    pos1 = (rows - (PAD_ROWS - 1)).astype(F32)
    return jnp.clip(pos1, 1.0, wsize)


def _lane_select(lane, s2, s4, s8, s16):
    return jnp.where(lane < POOL_GC, s2, jnp.where(lane < 2 * POOL_GC, s4, jnp.where(lane < 3 * POOL_GC, s8, s16)))


def _branch_pre(proj_a, conv_w, conv_b, name):
    R = proj_a.shape[0]
    nh = TR // HALO

    def body(t_ref, h_ref, w_ref, b_ref, c_ref, p_ref, gext, pext):
        i = pl.program_id(0)
        t = t_ref[...]
        hl = jnp.where(i == 0, 0.0, h_ref[...])
        gext[pl.ds(0, HALO), :] = hl[:, :CONV_CH] * _sig(hl[:, CONV_CH : 2 * CONV_CH])
        gext[pl.ds(HALO, TR), :] = t[:, :CONV_CH] * _sig(t[:, CONV_CH : 2 * CONV_CH])
        acc = jnp.zeros((TR, CONV_CH), F32) + b_ref[...]
        for k in range(CONV_K):
            acc = acc + w_ref[pl.ds(k, 1), :] * gext[pl.ds(HALO - (CONV_K - 1) + k, TR), :]
        c_ref[...] = acc

        p = t[:, 2 * CONV_CH :]
        pext[pl.ds(0, HALO), :] = hl[:, 2 * CONV_CH :]
        pext[pl.ds(HALO, TR), :] = p

        def back(k):
            return pext[pl.ds(HALO - k, TR), :]

        s2 = p + back(1)
        s4 = s2 + back(2) + back(3)
        s8 = s4 + back(4) + back(5) + back(6) + back(7)
        s16 = s8
        for k in range(8, 16):
            s16 = s16 + back(k)
        lane, wsize = _pool_consts(i)
        div = _pool_div(_row_ids(TR, i, (TR, POOL_CH)), wsize)
        p_ref[...] = (_lane_select(lane, s2, s4, s8, s16) / div - p).astype(p_ref.dtype)

    return _pcall(
        body,
        grid=(R // TR,),
        in_specs=[
            pl.BlockSpec((TR, TC), lambda i: (i, 0)),
            pl.BlockSpec((HALO, TC), lambda i: (jnp.maximum(i * nh - 1, 0), 0)),
            pl.BlockSpec((HALO, CONV_CH), lambda i: (0, 0)),
            pl.BlockSpec((1, CONV_CH), lambda i: (0, 0)),
        ],
        out_specs=[pl.BlockSpec((TR, CONV_CH), lambda i: (i, 0)), pl.BlockSpec((TR, POOL_CH), lambda i: (i, 0))],
        out_shape=[jax.ShapeDtypeStruct((R, CONV_CH), F32), jax.ShapeDtypeStruct((R, POOL_CH), _MXU)],
        scratch_shapes=[pltpu.VMEM((TR + HALO, CONV_CH), F32), pltpu.VMEM((TR + HALO, POOL_CH), F32)],
        compiler_params=_cp(("parallel",)),
        name=name,
    )(proj_a, proj_a, conv_w, conv_b)


def _mix_values(c, pooled, att, gates, ln_g, ln_b, w_co, b_co, w_bd, p_scale, w_ao):
    mu = jnp.mean(c, axis=-1, keepdims=True)
    xc = c - mu
    rstd = lax.rsqrt(jnp.mean(xc * xc, axis=-1, keepdims=True) + EPS)
    nl = xc * rstd
    ln = nl * ln_g + ln_b
    sg = _sig(ln)
    s = ln * sg
    ya = _dot(s, w_co) + b_co
    ybr = _dot(pooled, w_bd)
    yb = ybr * p_scale
    yc = _dot(att, w_ao)
    g = _sig(gates)
    g0, g1, g2 = g[:, :D], g[:, D : 2 * D], g[:, 2 * D :]
    mixed = g0 * ya + g1 * yb + g2 * yc
    return dict(rstd=rstd, nl=nl, ln=ln, sg=sg, s=s, ya=ya, ybr=ybr, yb=yb, yc=yc, g0=g0, g1=g1, g2=g2, mixed=mixed)


_MIX_W_SPECS = [
    pl.BlockSpec((1, CONV_CH), lambda i: (0, 0)),
    pl.BlockSpec((1, CONV_CH), lambda i: (0, 0)),
    pl.BlockSpec((CONV_CH, D), lambda i: (0, 0)),
    pl.BlockSpec((1, D), lambda i: (0, 0)),
    pl.BlockSpec((POOL_CH, D), lambda i: (0, 0)),
    pl.BlockSpec((1, D), lambda i: (0, 0)),
    pl.BlockSpec((ATT_W, D), lambda i: (0, 0)),
    pl.BlockSpec((D, D), lambda i: (0, 0)),
]


def _mix_act_specs(t):
    return [
        pl.BlockSpec((t, D_G), lambda i: (i, 0)),
        pl.BlockSpec((t, CONV_CH), lambda i: (i, 0)),
        pl.BlockSpec((t, POOL_CH), lambda i: (i, 0)),
        pl.BlockSpec((t, ATT_W), lambda i: (i, 0)),
    ]


def _mix_fwd(h, gates, c, pooled, att, mw, name):
    R = h.shape[0]

    def body(h_ref, g_ref, c_ref, p_ref, a_ref, lg, lb, wco, bco, wbd, ps, wao, wo, o_ref):
        v = _mix_values(c_ref[...], p_ref[...], a_ref[...], g_ref[...], lg[...], lb[...], wco[...], bco[...],
                        wbd[...], ps[...], wao[...])
        out = h_ref[...] + _dot(v["mixed"], wo[...])
        rows = _row_ids(TB, pl.program_id(0), (TB, D))
        o_ref[...] = jnp.where(rows >= PAD_ROWS, out, 0.0)

    return _pcall(
        body,
        grid=(R // TB,),
        in_specs=[pl.BlockSpec((TB, D), lambda i: (i, 0))] + _mix_act_specs(TB) + _MIX_W_SPECS,
        out_specs=pl.BlockSpec((TB, D), lambda i: (i, 0)),
        out_shape=jax.ShapeDtypeStruct((R, D), F32),
        compiler_params=_cp(("parallel",)),
        name=name,
    )(h, gates, c, pooled, att, *mw)


def _ffn_conv(ut, uh, cw_ref, cb_ref, ext):
    ext[:, pl.ds(0, HALO8), :] = uh
    ext[:, pl.ds(HALO8, TR), :] = ut
    um1 = ext[:, pl.ds(HALO8 - 1, TR), :]
    um2 = ext[:, pl.ds(HALO8 - 2, TR), :]
    cw = cw_ref[...]
    conv = cw[0][:, None, :] * um2 + cw[1][:, None, :] * um1 + cw[2][:, None, :] * ut + cb_ref[...][:, None, :]
    return conv, um1, um2


def _gelu_parts(x):
    th = jnp.tanh(GELU_C0 * (x + GELU_C1 * x * x * x))
    return th, 0.5 * x * (1.0 + th)


def _ffn_in_specs(nrow8, order):
    n8 = TR // HALO8
    return [
        pl.BlockSpec((2, TR, TF), lambda *g: (0, order(*g)[0], order(*g)[1])),
        pl.BlockSpec((2, HALO8, TF), lambda *g: (0, jnp.maximum(order(*g)[0] * n8 - 1, 0), order(*g)[1])),
        pl.BlockSpec((FFN_K, 2, TF), lambda *g: (0, 0, order(*g)[1])),
        pl.BlockSpec((2, TF), lambda *g: (0, order(*g)[1])),
    ]


def _ffn_fwd(u3, h2, cw, cb, w_down, name):
    R = h2.shape[0]
    nj = D_FF // TF

    def body(u_ref, uh_ref, cw_ref, cb_ref, wd_ref, h_ref, o_ref, ext, acc):
        i, j = pl.program_id(0), pl.program_id(1)
        uh = jnp.where(i == 0, 0.0, uh_ref[...])
        conv, _, _ = _ffn_conv(u_ref[...], uh, cw_ref, cb_ref, ext)
        _, a = _gelu_parts(conv[0])
        part = _dot(a * conv[1], wd_ref[...])

        @pl.when(j == 0)
        def _():
            acc[...] = part

        @pl.when(j > 0)
        def _():
            acc[...] += part

        @pl.when(j == nj - 1)
        def _():
            rows = _row_ids(TR, i, (TR, D))
            o_ref[...] = jnp.where(rows >= PAD_ROWS, h_ref[...] + acc[...], 0.0)

    return _pcall(
        body,
        grid=(R // TR, nj),
        in_specs=_ffn_in_specs(R // HALO8, lambda i, j: (i, j))
        + [pl.BlockSpec((TF, D), lambda i, j: (j, 0)), pl.BlockSpec((TR, D), lambda i, j: (i, 0))],
        out_specs=pl.BlockSpec((TR, D), lambda i, j: (i, 0)),
        out_shape=jax.ShapeDtypeStruct((R, D), F32),
        scratch_shapes=[pltpu.VMEM((2, TR + HALO8, TF), F32), pltpu.VMEM((TR, D), F32)],
        compiler_params=_cp(("parallel", "arbitrary")),
        name=name,
    )(u3, u3, cw, cb, w_down.astype(_MXU), h2)


def _loss_bwd(h, target, gain, name):
    R = h.shape[0]

    def body(h_ref, t_ref, g_ref, dh_ref, loss_ref, dg_ref):
        i = pl.program_id(0)

        @pl.when(i == 0)
        def _():
            loss_ref[...] = jnp.zeros_like(loss_ref)
            dg_ref[...] = jnp.zeros_like(dg_ref)
            dh_ref[...] = jnp.zeros_like(dh_ref)

        @pl.when(i > 0)
        def _():
            g = g_ref[...]
            r, n, y = _rms_fwd(h_ref[...], g)
            e = y - t_ref[...]
            loss_ref[...] += (0.5 / D) * jnp.sum(jnp.sum(e * e, axis=1, keepdims=True), axis=0, keepdims=True)
            dx, dgain = _rms_bwd(e * (1.0 / D), r, n, g)
            dg_ref[...] += dgain
            dh_ref[...] = dx

    return _pcall(
        body,
        grid=(R // BLK,),
        in_specs=[
            pl.BlockSpec((BLK, D), lambda i: (i, 0)),
            pl.BlockSpec((BLK, D), lambda i: (jnp.maximum(i - 1, 0), 0)),
            pl.BlockSpec((1, D), lambda i: (0, 0)),
        ],
        out_specs=[
            pl.BlockSpec((BLK, D), lambda i: (i, 0)),
            pl.BlockSpec((1, 1), lambda i: (0, 0)),
            pl.BlockSpec((1, D), lambda i: (0, 0)),
        ],
        out_shape=[
            jax.ShapeDtypeStruct((R, D), F32),
            jax.ShapeDtypeStruct((1, 1), F32),
            jax.ShapeDtypeStruct((1, D), F32),
        ],
        compiler_params=_cp(("arbitrary",)),
        name=name,
    )(h, target, gain)


DW_BLOCK_BYTES = 6 * 1024 * 1024


def _tdot_acc(a, b, name):
    R, M = a.shape
    split = b.ndim == 3
    width = b.shape[-1]
    N = 2 * width if split else width
    tm = min(M, 512)
    tk = R // DW_STEPS
    tn = max(t for t in range(BLK, width + 1, BLK) if width % t == 0 and tk * t * b.dtype.itemsize <= DW_BLOCK_BYTES)
    per = width // tn

    def body(a_ref, b_ref, o_ref):
        part = _tdot(a_ref[...], b_ref[...])

        @pl.when(pl.program_id(2) == 0)
        def _():
            o_ref[...] = part

        @pl.when(pl.program_id(2) > 0)
        def _():
            o_ref[...] += part

    if split:
        b_spec = pl.BlockSpec((None, tk, tn), lambda m, n, k: (n // per, k, n % per))
    else:
        b_spec = pl.BlockSpec((tk, tn), lambda m, n, k: (k, n))
    return _pcall(
        body,
        grid=(M // tm, N // tn, R // tk),
        in_specs=[pl.BlockSpec((tk, tm), lambda m, n, k: (k, m)), b_spec],
        out_specs=pl.BlockSpec((tm, tn), lambda m, n, k: (m, n)),
        out_shape=jax.ShapeDtypeStruct((M, N), F32),
        compiler_params=_cp(("parallel", "parallel", "arbitrary")),
        name=name,
    )(a, b)


def _ffn_bwd1(dh3, u3, cw, cb, w_down, name, side=None):
    R = dh3.shape[0]
    nj = D_FF // TF

    def body(u_ref, uh_ref, cw_ref, cb_ref, wd_ref, dh_ref, dc_ref, dwd_ref, dwb_ref, ext):
        j, i = pl.program_id(0), pl.program_id(1)
        ut = u_ref[...]
        uh = jnp.where(i == 0, 0.0, uh_ref[...])
        conv, um1, um2 = _ffn_conv(ut, uh, cw_ref, cb_ref, ext)
        gt, val = conv[0], conv[1]
        th, a = _gelu_parts(gt)
        dh = dh_ref[...].astype(_MXU)
        dact = _dot_t(dh, wd_ref[...])
        dgelu = 0.5 * (1.0 + th) + 0.5 * gt * (1.0 - th * th) * (GELU_C0 * (1.0 + 3.0 * GELU_C1 * gt * gt))
        dgt = dact * val * dgelu
        dval = dact * a
        dc_ref[0] = dgt
        dc_ref[1] = dval

        @pl.when(i == 0)
        def _():
            dwb_ref[...] = jnp.zeros_like(dwb_ref)
            dwd_ref[...] = jnp.zeros_like(dwd_ref)

        dwd_ref[...] += _tdot(a * val, dh)

        for half, dcv in ((0, dgt), (1, dval)):
            dwb_ref[half, pl.ds(0, 1), :] += jnp.sum(um2[half] * dcv, axis=0, keepdims=True)
            dwb_ref[half, pl.ds(1, 1), :] += jnp.sum(um1[half] * dcv, axis=0, keepdims=True)
            dwb_ref[half, pl.ds(2, 1), :] += jnp.sum(ut[half] * dcv, axis=0, keepdims=True)
            dwb_ref[half, pl.ds(3, 1), :] += jnp.sum(dcv, axis=0, keepdims=True)

    return _hosted_call(
        body, side,
        grid=(nj, R // TR),
        in_specs=_ffn_in_specs(R // HALO8, lambda j, i: (i, j))
        + [pl.BlockSpec((TF, D), lambda j, i: (j, 0)), pl.BlockSpec((TR, D), lambda j, i: (i, 0))],
        out_specs=[
            pl.BlockSpec((2, TR, TF), lambda j, i: (0, i, j)),
            pl.BlockSpec((TF, D), lambda j, i: (j, 0)),
            pl.BlockSpec((2, 8, TF), lambda j, i: (0, 0, j)),
        ],
        out_shape=[
            jax.ShapeDtypeStruct((2, R, D_FF), F32),
            jax.ShapeDtypeStruct((D_FF, D), F32),
            jax.ShapeDtypeStruct((2, 8, D_FF), F32),
        ],
        scratch_shapes=[pltpu.VMEM((2, TR + HALO8, TF), F32)],
        name=name,
        operands=(u3, u3, cw, cb, w_down.astype(_MXU), dh3),
    )


def _ffn_bwd2(dc3, cw, w_up, h2, dh3, gain, name):
    R = h2.shape[0]
    nj = D_FF // TF
    n8 = TR // HALO8
    last8 = R // HALO8 - 1
    ni = R // TR

    def body(dc_ref, dn_ref, cw_ref, wg_ref, wv_ref, h_ref, dh_ref, g_ref, du_ref, o_ref, xn_ref, dg_ref, ext, acc):
        i, j = pl.program_id(0), pl.program_id(1)
        dc = dc_ref[...]
        ext[:, pl.ds(0, TR), :] = dc
        ext[:, pl.ds(TR, HALO8), :] = jnp.where(i == ni - 1, 0.0, dn_ref[...])
        cw = cw_ref[...]
        du = (cw[2][:, None, :] * dc + cw[1][:, None, :] * ext[:, pl.ds(1, TR), :]
              + cw[0][:, None, :] * ext[:, pl.ds(2, TR), :])
        du_ref[...] = du.astype(du_ref.dtype)
        part = _dot_t(du[0], wg_ref[...]) + _dot_t(du[1], wv_ref[...])

        @pl.when(j == 0)
        def _():
            acc[...] = part

        @pl.when(j > 0)
        def _():
            acc[...] += part

        @pl.when(jnp.logical_and(i == 0, j == 0))
        def _():
            dg_ref[...] = jnp.zeros_like(dg_ref)

        @pl.when(j == nj - 1)
        def _():
            g = g_ref[...]
            r, n, y = _rms_fwd(h_ref[...], g)
            dx, dgain = _rms_bwd(acc[...], r, n, g)
            dg_ref[...] += dgain
            rows = _row_ids(TR, i, (TR, D))
            o_ref[...] = jnp.where(rows >= PAD_ROWS, dh_ref[...] + dx, 0.0)
            xn_ref[...] = y.astype(xn_ref.dtype)

    w_up = w_up.astype(_MXU)
    return _pcall(
        body,
        grid=(ni, nj),
        in_specs=[
            pl.BlockSpec((2, TR, TF), lambda i, j: (0, i, j)),
            pl.BlockSpec((2, HALO8, TF), lambda i, j: (0, jnp.minimum((i + 1) * n8, last8), j)),
            pl.BlockSpec((FFN_K, 2, TF), lambda i, j: (0, 0, j)),
            pl.BlockSpec((D, TF), lambda i, j: (0, j)),
            pl.BlockSpec((D, TF), lambda i, j: (0, nj + j)),
            pl.BlockSpec((TR, D), lambda i, j: (i, 0)),
            pl.BlockSpec((TR, D), lambda i, j: (i, 0)),
            pl.BlockSpec((1, D), lambda i, j: (0, 0)),
        ],
        out_specs=[
            pl.BlockSpec((2, TR, TF), lambda i, j: (0, i, j)),
            pl.BlockSpec((TR, D), lambda i, j: (i, 0)),
            pl.BlockSpec((TR, D), lambda i, j: (i, 0)),
            pl.BlockSpec((1, D), lambda i, j: (0, 0)),
        ],
        out_shape=[
            jax.ShapeDtypeStruct((2, R, D_FF), _MXU),
            jax.ShapeDtypeStruct((R, D), F32),
            jax.ShapeDtypeStruct((R, D), _MXU),
            jax.ShapeDtypeStruct((1, D), F32),
        ],
        scratch_shapes=[pltpu.VMEM((2, TR + HALO8, TF), F32), pltpu.VMEM((TR, D), F32)],
        compiler_params=_cp(("arbitrary", "arbitrary")),
        name=name,
    )(dc3, dc3, cw, w_up, w_up, h2, dh3, gain)


def _mix_bwd(dh2, gates, c, pooled, att, mw, name):
    R = dh2.shape[0]

    def body(dh_ref, g_ref, c_ref, p_ref, a_ref, lg, lb, wco, bco, wbd, ps, wao, wo,
             dg_ref, dc_ref, dp_ref, da_ref, mx_ref, s_ref, dya_ref, dyb_ref, dyc_ref, accd_ref, accc_ref):
        v = _mix_values(c_ref[...], p_ref[...], a_ref[...], g_ref[...], lg[...], lb[...], wco[...], bco[...],
                        wbd[...], ps[...], wao[...])
        dmix = _dot_t(dh_ref[...], wo[...])
        for k, (gk, yk) in enumerate(((v["g0"], v["ya"]), (v["g1"], v["yb"]), (v["g2"], v["yc"]))):
            dg_ref[:, k * D : (k + 1) * D] = dmix * yk * gk * (1.0 - gk)
        dya = dmix * v["g0"]
        dyb = dmix * v["g1"]
        dyc = dmix * v["g2"]
        ds = _dot_t(dya, wco[...])
        ln, sg, nl = v["ln"], v["sg"], v["nl"]
        dln = ds * (sg * (1.0 + ln * (1.0 - sg)))
        dn = dln * lg[...]
        dc = v["rstd"] * (dn - jnp.mean(dn, axis=-1, keepdims=True) - nl * jnp.mean(dn * nl, axis=-1, keepdims=True))
        dybs = dyb * ps[...]
        dc_ref[...] = dc
        dp_ref[...] = _dot_t(dybs, wbd[...])
        da_ref[...] = _dot_t(dyc, wao[...])
        mx_ref[...] = v["mixed"].astype(mx_ref.dtype)
        s_ref[...] = v["s"].astype(s_ref.dtype)
        dya_ref[...] = dya.astype(dya_ref.dtype)
        dyb_ref[...] = dybs.astype(dyb_ref.dtype)
        dyc_ref[...] = dyc.astype(dyc_ref.dtype)

        @pl.when(pl.program_id(0) == 0)
        def _():
            accd_ref[...] = jnp.zeros_like(accd_ref)
            accc_ref[...] = jnp.zeros_like(accc_ref)

        accd_ref[pl.ds(0, 1), :] += jnp.sum(dya, axis=0, keepdims=True)
        accd_ref[pl.ds(1, 1), :] += jnp.sum(dyb * v["ybr"], axis=0, keepdims=True)
        accc_ref[pl.ds(0, 1), :] += jnp.sum(dln * nl, axis=0, keepdims=True)
        accc_ref[pl.ds(1, 1), :] += jnp.sum(dln, axis=0, keepdims=True)
        accc_ref[pl.ds(2, 1), :] += jnp.sum(dc, axis=0, keepdims=True)

    def row(w):
        return pl.BlockSpec((TB, w), lambda i: (i, 0))

    return _pcall(
        body,
        grid=(R // TB,),
        in_specs=[row(D)] + _mix_act_specs(TB) + _MIX_W_SPECS,
        out_specs=[row(D_G), row(CONV_CH), row(POOL_CH), row(ATT_W), row(D), row(CONV_CH), row(D), row(D), row(D),
                   pl.BlockSpec((8, D), lambda i: (0, 0)), pl.BlockSpec((8, CONV_CH), lambda i: (0, 0))],
        out_shape=[
            jax.ShapeDtypeStruct((R, D_G), F32),
            jax.ShapeDtypeStruct((R, CONV_CH), F32),
            jax.ShapeDtypeStruct((R, POOL_CH), F32),
            jax.ShapeDtypeStruct((R, ATT_W), F32),
            jax.ShapeDtypeStruct((R, D), _MXU),
            jax.ShapeDtypeStruct((R, CONV_CH), _MXU),
            jax.ShapeDtypeStruct((R, D), _MXU),
            jax.ShapeDtypeStruct((R, D), _MXU),
            jax.ShapeDtypeStruct((R, D), _MXU),
            jax.ShapeDtypeStruct((8, D), F32),
            jax.ShapeDtypeStruct((8, CONV_CH), F32),
        ],
        compiler_params=_cp(("arbitrary",)),
        name=name,
    )(dh2, gates, c, pooled, att, *mw)


def _attn_bwd(qkv, att, datt, name):
    R = qkv.shape[0]
    nb = R // BLK

    def body(q_ref, k_ref, v_ref, o_ref, do_ref, dq_ref, dk_ref, dv_ref):
        i = pl.program_id(1)

        @pl.when(i == 0)
        def _():
            dk_ref[...] = jnp.zeros_like(dk_ref)
            dv_ref[...] = jnp.zeros_like(dv_ref)

        q = q_ref[...]
        dob = do_ref[...].astype(_MXU)
        dof = dob.astype(F32)
        sms = {CH: _sum_mat(CH), BLK: _sum_mat(BLK)}
        e_all = jnp.sum(dof * o_ref[...], axis=1, keepdims=True)

        def scores(off, w, masked):
            valid = _attn_mask(i, off, w) if masked else None
            z = _dot_t(q, k_ref[pl.ds(off, w), :])
            return off, w, valid, z, _dot_t(dob, v_ref[pl.ds(off, w), :])

        def sums(state, carry):
            off, w, valid, z, da = state
            lk, log_beta = _attn_logs(z, valid)
            return (off, w, valid, log_beta, _dot(lk, sms[w]), jnp.sum(lk, axis=1, keepdims=True), da), carry

        def weights(state, carry):
            off, w, valid, log_beta, r, total, da = state
            dq, s_after, e_done = carry
            ab = _attn_weights(log_beta, r, valid, s_after).astype(_MXU)
            e = ab.astype(F32) * da
            re = jnp.dot(_hi_lo_rows(e), sms[w], preferred_element_type=F32)
            return (off, w, valid, jnp.exp2(log_beta), ab, e, re), (dq, s_after + total, e_done)

        def grads(state, carry):
            off, w, valid, beta, ab, e, re = state
            dq, s_after, e_done = carry
            dz = e - beta * ((e_all - e_done) - _fold_sums(re))
            if valid is not None:
                dz = jnp.where(valid, dz, 0.0)
            dzb = (dz * LN2).astype(_MXU)
            dk_ref[pl.ds(off, w), :] += _tdot(dzb, q)
            dv_ref[pl.ds(off, w), :] += _tdot(ab, dob)
            dq = dq + jnp.dot(dzb, k_ref[pl.ds(off, w), :], preferred_element_type=F32)
            return None, (dq, s_after, e_done + jnp.sum(e, axis=1, keepdims=True))

        zero = jnp.zeros((BLK, 1), F32)
        dq, _, _ = _attn_walk(i, (scores, sums, weights, grads), 1, (jnp.zeros((BLK, HD), F32), zero, zero))
        dq_ref[...] = dq * (Q_SCALE * LOG2E)

    blk = pl.BlockSpec((BLK, HD), lambda h, i: (i, h))
    col = pl.BlockSpec((R, HD), lambda h, i: (0, h))
    return _pcall(
        body,
        grid=(NH, nb),
        in_specs=[
            blk,
            pl.BlockSpec((R, HD), lambda h, i: (0, NH + h)),
            pl.BlockSpec((R, HD), lambda h, i: (0, 2 * NH + h)),
            blk,
            blk,
        ],
        out_specs=[blk, col, col],
        out_shape=[jax.ShapeDtypeStruct((R, ATT_W), F32)] * 3,
        compiler_params=pltpu.CompilerParams(dimension_semantics=("arbitrary", "arbitrary"),
                                             vmem_limit_bytes=56 * 1024 * 1024),
        name=name,
    )(qkv, qkv, qkv, att, datt)


def _branch_bwd(dc, dpooled, proj_a, conv_w, name):
    R = dc.shape[0]
    nh = TR // HALO
    last = R // HALO - 1
    ni = R // TR

    def body(dc_ref, dcn_ref, dp_ref, dpn_ref, t_ref, h_ref, w_ref, o_ref, dcw_ref, gext, dcext, eext):
        i = pl.program_id(0)
        is_last = i == ni - 1
        t = t_ref[...]
        hl = jnp.where(i == 0, 0.0, h_ref[...])
        a = t[:, :CONV_CH]
        sg = _sig(t[:, CONV_CH : 2 * CONV_CH])
        gext[pl.ds(0, HALO), :] = hl[:, :CONV_CH] * _sig(hl[:, CONV_CH : 2 * CONV_CH])
        gext[pl.ds(HALO, TR), :] = a * sg
        dct = dc_ref[...]
        dcext[pl.ds(0, TR), :] = dct
        dcext[pl.ds(TR, HALO), :] = jnp.where(is_last, 0.0, dcn_ref[...])

        @pl.when(i == 0)
        def _():
            dcw_ref[...] = jnp.zeros_like(dcw_ref)

        dglu = jnp.zeros((TR, CONV_CH), F32)
        for k in range(CONV_K):
            dglu = dglu + w_ref[pl.ds(k, 1), :] * dcext[pl.ds(CONV_K - 1 - k, TR), :]
            dcw_ref[pl.ds(k, 1), :] += jnp.sum(gext[pl.ds(HALO - (CONV_K - 1) + k, TR), :] * dct, axis=0, keepdims=True)
        o_ref[:, :CONV_CH] = dglu * sg
        o_ref[:, CONV_CH : 2 * CONV_CH] = dglu * a * sg * (1.0 - sg)

        lane, wsize = _pool_consts(i)
        dpt = dp_ref[...]
        eext[pl.ds(0, TR), :] = dpt / _pool_div(_row_ids(TR, i, (TR, POOL_CH)), wsize)
        nxt = dpn_ref[...] / _pool_div(_row_ids(TR, i + 1, (HALO, POOL_CH)), wsize)
        eext[pl.ds(TR, HALO), :] = jnp.where(is_last, 0.0, nxt)

        def fwd(k):
            return eext[pl.ds(k, TR), :]

        s2 = fwd(0) + fwd(1)
        s4 = s2 + fwd(2) + fwd(3)
        s8 = s4 + fwd(4) + fwd(5) + fwd(6) + fwd(7)
        s16 = s8
        for k in range(8, 16):
            s16 = s16 + fwd(k)
        o_ref[:, 2 * CONV_CH :] = _lane_select(lane, s2, s4, s8, s16) - dpt

    def nxt_spec(w):
        return pl.BlockSpec((HALO, w), lambda i: (jnp.minimum((i + 1) * nh, last), 0))

    return _pcall(
        body,
        grid=(ni,),
        in_specs=[
            pl.BlockSpec((TR, CONV_CH), lambda i: (i, 0)),
            nxt_spec(CONV_CH),
            pl.BlockSpec((TR, POOL_CH), lambda i: (i, 0)),
            nxt_spec(POOL_CH),
            pl.BlockSpec((TR, TC), lambda i: (i, 0)),
            pl.BlockSpec((HALO, TC), lambda i: (jnp.maximum(i * nh - 1, 0), 0)),
            pl.BlockSpec((HALO, CONV_CH), lambda i: (0, 0)),
        ],
        out_specs=[pl.BlockSpec((TR, TC), lambda i: (i, 0)), pl.BlockSpec((HALO, CONV_CH), lambda i: (0, 0))],
        out_shape=[jax.ShapeDtypeStruct((R, TC), F32), jax.ShapeDtypeStruct((HALO, CONV_CH), F32)],
        scratch_shapes=[
            pltpu.VMEM((TR + HALO, CONV_CH), F32),
            pltpu.VMEM((TR + HALO, CONV_CH), F32),
            pltpu.VMEM((TR + HALO, POOL_CH), F32),
        ],
        compiler_params=_cp(("arbitrary",)),
        name=name,
    )(dc, dc, dpooled, dpooled, proj_a, proj_a, conv_w)


def _in_bwd(dfront, dq, dk, dv, dgates, w_a, w_g, h, dh2, gain, name):
    R = h.shape[0]
    nj = 1 + D_G // TC
    ni = R // TR
    w_a = w_a.astype(_MXU)
    w_f, w_q = w_a[:, :TC], w_a[:, TC : TC + ATT_W]
    w_k, w_v = w_a[:, TC + ATT_W : TC + 2 * ATT_W], w_a[:, TC + 2 * ATT_W :]

    def body(df_ref, dq_ref, dk_ref, dv_ref, dg_ref, wf_ref, wq_ref, wk_ref, wv_ref, wg_ref, h_ref, dh_ref, g_ref,
             o_ref, hn_ref, dgain_ref, acc):
        i, j = pl.program_id(0), pl.program_id(1)

        @pl.when(j == 0)
        def _():
            acc[...] = (_dot_t(df_ref[...], wf_ref[...]) + _dot_t(dq_ref[...], wq_ref[...])
                        + _dot_t(dk_ref[...], wk_ref[...]) + _dot_t(dv_ref[...], wv_ref[...]))

        @pl.when(j > 0)
        def _():
            acc[...] += _dot_t(dg_ref[...], wg_ref[...])

        @pl.when(jnp.logical_and(i == 0, j == 0))
        def _():
            dgain_ref[...] = jnp.zeros_like(dgain_ref)

        @pl.when(j == nj - 1)
        def _():
            g = g_ref[...]
            r, n, y = _rms_fwd(h_ref[...], g)
            dx, dgain = _rms_bwd(acc[...], r, n, g)
            dgain_ref[...] += dgain
            rows = _row_ids(TR, i, (TR, D))
            o_ref[...] = jnp.where(rows >= PAD_ROWS, dh_ref[...] + dx, 0.0)
            hn_ref[...] = y.astype(hn_ref.dtype)

    def row(w):
        return pl.BlockSpec((TR, w), lambda i, j: (i, 0))

    def whole(w):
        return pl.BlockSpec((D, w), lambda i, j: (0, 0))

    def gcol(i, j):
        return jnp.maximum(j - 1, 0)

    return _pcall(
        body,
        grid=(ni, nj),
        in_specs=[
            row(TC), row(ATT_W), row(ATT_W), row(ATT_W),
            pl.BlockSpec((TR, TC), lambda i, j: (i, gcol(i, j))),
            whole(TC), whole(ATT_W), whole(ATT_W), whole(ATT_W),
            pl.BlockSpec((D, TC), lambda i, j: (0, gcol(i, j))),
            row(D), row(D),
            pl.BlockSpec((1, D), lambda i, j: (0, 0)),
        ],
        out_specs=[row(D), row(D), pl.BlockSpec((1, D), lambda i, j: (0, 0))],
        out_shape=[jax.ShapeDtypeStruct((R, D), F32), jax.ShapeDtypeStruct((R, D), _MXU), jax.ShapeDtypeStruct((1, D), F32)],
        scratch_shapes=[pltpu.VMEM((TR, D), F32)],
        compiler_params=_cp(("arbitrary", "arbitrary")),
        name=name,
    )(dfront, dq, dk, dv, dgates, w_f, w_q, w_k, w_v, w_g.astype(_MXU), h, dh2, gain)


def _pool_blockdiag(w_grp):
    eye = jnp.eye(len(POOL_WINDOWS), dtype=w_grp.dtype)
    return jnp.einsum("gcd,gh->gchd", w_grp, eye).reshape(POOL_CH, D)


def _pool_blockdiag_grad(dw_bd):
    d4 = dw_bd.reshape(len(POOL_WINDOWS), POOL_GC, len(POOL_WINDOWS), D // len(POOL_WINDOWS))
    return jnp.stack([d4[g, :, g, :] for g in range(len(POOL_WINDOWS))])


def _local_step(x, target, p, gather_next=None, exchange_early=None):
    row = lambda a: a.reshape(1, -1)
    h = jnp.concatenate([jnp.zeros((PAD_ROWS, D), F32), p["meta"], x], axis=0)
    saved = []
    for l in range(N_LAYERS):
        w_a, w_g = p["w_in"][l][:, :D_A], p["w_in"][l][:, D_A:]
        conv_w = jnp.concatenate([p["conv_dw_w"][l], jnp.zeros((1, CONV_CH), F32)], axis=0)
        cw3 = p["ffn_dw_w"][l].reshape(FFN_K, 2, D_FF)
        cb2 = p["ffn_dw_b"][l].reshape(2, D_FF)
        mw = (row(p["conv_ln_g"][l]), row(p["conv_ln_b"][l]), p["w_conv_out"][l].astype(_MXU), row(p["b_conv_out"][l]),
              _pool_blockdiag(p["w_pool_grp"][l]).astype(_MXU), row(p["pool_scale"][l]),
              p["w_attn_out"][l].astype(_MXU), p["w_o"][l].astype(_MXU))
        proj_a = _rms_matmul(h, row(p["norm1"][l]), w_a, f"proj_a{l}")
        gates = _rms_matmul(h, row(p["norm1"][l]), w_g, f"proj_g{l}")
        qkv = _qkv_cast(proj_a, f"qkv_cast{l}")
        c, pooled = _branch_pre(proj_a, conv_w, row(p["conv_dw_b"][l]), f"branch_pre{l}")
        att = _attn_fwd(qkv, f"attn_fwd{l}")
        h2 = _mix_fwd(h, gates, c, pooled, att, mw, f"mix_fwd{l}")
        if l == 0 and gather_next is not None:
            u3, *landed = _rms_matmul(h2, row(p["norm2"][l]), p["w_up"][l], f"ffn_up{l}", split_out=True,
                                      side=gather_next[0])
            gather_next[1](landed)
        else:
            u3 = _rms_matmul(h2, row(p["norm2"][l]), p["w_up"][l], f"ffn_up{l}", split_out=True)
        h3 = _ffn_fwd(u3, h2, cw3, cb2, p["w_down"][l], f"ffn_fwd{l}")
        saved.append(dict(h=h, w_a=w_a, w_g=w_g, conv_w=conv_w, cw3=cw3, cb2=cb2, mw=mw, proj_a=proj_a, gates=gates,
                          qkv=qkv, c=c, pooled=pooled, att=att, h2=h2, u3=u3))
        h = h3

    dh, loss, d_final = _loss_bwd(h, target, row(p["final_norm"]), "loss_bwd")

    g = {k: [None] * N_LAYERS for k in ("norm1", "w_in", "conv_dw_w", "conv_dw_b", "conv_ln_g", "conv_ln_b", "w_conv_out",
                                        "b_conv_out", "w_pool_grp", "pool_scale", "w_attn_out", "w_o", "norm2", "w_up",
                                        "ffn_dw_w", "ffn_dw_b", "w_down")}
    early = []
    for l in reversed(range(N_LAYERS)):
        s = saved[l]
        side = exchange_early({n: g[n][1] for n in _BIG}) if l == 0 and exchange_early is not None else None
        dc3, g["w_down"][l], dwb, *early = _ffn_bwd1(dh, s["u3"], s["cw3"], s["cb2"], p["w_down"][l], f"ffn_bwd1_{l}",
                                                     side=side)
        g["ffn_dw_w"][l] = jnp.transpose(dwb[:, :FFN_K, :], (1, 0, 2)).reshape(FFN_K, 2 * D_FF)
        g["ffn_dw_b"][l] = dwb[:, FFN_K, :].reshape(2 * D_FF)
        du3, dh2, xn2, dg2 = _ffn_bwd2(dc3, s["cw3"], p["w_up"][l], s["h2"], dh, row(p["norm2"][l]), f"ffn_bwd2_{l}")
        g["norm2"][l] = dg2[0]
        g["w_up"][l] = _tdot_acc(xn2, du3, f"dw_up{l}")
        (dgates, dc, dpooled, datt, mixed, s_act, dya, dybs, dyc, acc_d, acc_c) = _mix_bwd(
            dh2, s["gates"], s["c"], s["pooled"], s["att"], s["mw"], f"mix_bwd{l}")
        g["w_o"][l] = _tdot_acc(mixed, dh2, f"dw_o{l}")
        g["w_conv_out"][l] = _tdot_acc(s_act, dya, f"dw_conv_out{l}")
        g["w_pool_grp"][l] = _pool_blockdiag_grad(_tdot_acc(s["pooled"], dybs, f"dw_pool{l}"))
        g["w_attn_out"][l] = _tdot_acc(s["att"], dyc, f"dw_attn_out{l}")
        g["b_conv_out"][l] = acc_d[0]
        g["pool_scale"][l] = acc_d[1]
        g["conv_ln_g"][l] = acc_c[0]
        g["conv_ln_b"][l] = acc_c[1]
        g["conv_dw_b"][l] = acc_c[2]
        dq, dk, dv = _attn_bwd(s["qkv"], s["att"], datt, f"attn_bwd{l}")
        dfront, dcw = _branch_bwd(dc, dpooled, s["proj_a"], s["conv_w"], f"branch_bwd{l}")
        g["conv_dw_w"][l] = dcw[:CONV_K]
        dh, hn, dg1 = _in_bwd(dfront, dq, dk, dv, dgates, s["w_a"], s["w_g"], s["h"], dh2, row(p["norm1"][l]),
                              f"in_bwd{l}")
        g["norm1"][l] = dg1[0]
        g["w_in"][l] = jnp.concatenate(
            [_tdot_acc(hn, part, f"dw_in{l}_{k}") for k, part in enumerate((dfront, dq, dk, dv, dgates))], axis=1)

    grads = dict(g, final_norm=d_final[0], meta=dh[PAD_ROWS:BLK])
    return loss, dh[BLK:], grads, early


def _axes():
    return lax.axis_index("x"), lax.axis_index("y"), lax.axis_index("c")


def _transfer(side, name):
    def body(*refs):
        s_in, s_out = len(side.arrays), len(side.out_shape)
        ins, outs, scratch = refs[:s_in], refs[s_in : s_in + s_out], refs[s_in + s_out :]
        side.start(ins, outs, scratch)
        side.finish(ins, outs, scratch)

    return _pcall(
        body,
        in_specs=[ANY] * len(side.arrays),
        out_specs=[ANY] * len(side.out_shape),
        out_shape=list(side.out_shape),
        scratch_shapes=list(side.scratch),
        compiler_params=pltpu.CompilerParams(has_side_effects=True),
        name=name,
    )(*side.arrays)


def _gather_side(blobs):
    nt = len(blobs)
    flips = ((1, 0), (0, 1), (1, 1))

    def local(srcs, dsts, sems):
        x, y, _ = _axes()
        return [pltpu.make_async_copy(srcs[t], dsts[t].at[2 * x + y], sems[2].at[t]) for t in range(nt)]

    def remote(srcs, dsts, sems, arriving):
        send_sems, recv_sems, _ = sems
        x, y, c = _axes()
        out = []
        for t in range(nt):
            for f, (fx, fy) in enumerate(flips):
                px = 1 - x if fx else x
                py = 1 - y if fy else y
                out.append(pltpu.make_async_remote_copy(
                    srcs[t], dsts[t].at[2 * px + py if arriving else 2 * x + y], send_sems.at[t, f], recv_sems.at[t, f],
                    device_id=(px, py, c), device_id_type=MESH))
        return out

    def start(srcs, dsts, sems):
        for cp in local(srcs, dsts, sems) + remote(srcs, dsts, sems, arriving=False):
            cp.start()

    def finish(srcs, dsts, sems):
        for cp in remote(srcs, dsts, sems, arriving=False):
            cp.wait_send()
        for cp in remote(srcs, dsts, sems, arriving=True):
            cp.wait_recv()
        for cp in local(srcs, dsts, sems):
            cp.wait()

    return _Side(
        arrays=tuple(blobs),
        out_shape=tuple(jax.ShapeDtypeStruct((4,) + b.shape, b.dtype) for b in blobs),
        scratch=(pltpu.SemaphoreType.DMA((nt, 3)), pltpu.SemaphoreType.DMA((nt, 3)), pltpu.SemaphoreType.DMA((nt,))),
        start=start,
        finish=finish,
    )


def _exchange_side(contribs):
    nt = len(contribs)

    def plan(srcs, dsts, sems):
        send_sems, recv_sems, local_sems = sems
        x, y, c = _axes()
        sibling = (x, y, 1 - c)
        chips = [(1 - x, y), (x, 1 - y), (1 - x, 1 - y)]

        def slot(t, px, py, pc):
            return dsts[t].at[4 * px + 2 * py + pc]

        def copy(t, k, source, target_slot, to):
            return pltpu.make_async_remote_copy(source, target_slot, send_sems.at[t, k], recv_sems.at[t, k],
                                                device_id=to, device_id_type=MESH)

        def mine(t):
            return srcs[t].at[2 * x + y]

        def own():
            return [pltpu.make_async_copy(mine(t), slot(t, x, y, c), local_sems.at[t]) for t in range(nt)]

        def first():
            out = []
            for t in range(nt):
                out.append(copy(t, 0, mine(t), slot(t, x, y, c), sibling))
                out += [copy(t, 1 + j, srcs[t].at[2 * px + py], slot(t, x, y, c), (px, py, c))
                        for j, (px, py) in enumerate(chips)]
            return out

        def relay():
            return [(copy(t, 1 + j, mine(t), slot(t, px, py, c), (px, py, c)),
                     copy(t, 4 + j, slot(t, px, py, c), slot(t, px, py, c), sibling))
                    for j, (px, py) in enumerate(chips) for t in range(nt)]

        def from_sibling():
            out = [copy(t, 0, mine(t), slot(t, x, y, 1 - c), sibling) for t in range(nt)]
            return out + [copy(t, 4 + j, mine(t), slot(t, px, py, 1 - c), sibling)
                          for t in range(nt) for j, (px, py) in enumerate(chips)]

        return own, first, relay, from_sibling

    def start(srcs, dsts, sems):
        own, first, _, _ = plan(srcs, dsts, sems)
        for cp in own() + first():
            cp.start()

    def finish(srcs, dsts, sems):
        own, first, relay, from_sibling = plan(srcs, dsts, sems)
        passed = []
        for arriving, onward in relay():
            arriving.wait_recv()
            onward.start()
            passed.append(onward)
        for cp in from_sibling():
            cp.wait_recv()
        for cp in first() + passed:
            cp.wait_send()
        for cp in own():
            cp.wait()

    return _Side(
        arrays=tuple(contribs),
        out_shape=tuple(jax.ShapeDtypeStruct((8,) + a.shape[1:], a.dtype) for a in contribs),
        scratch=(pltpu.SemaphoreType.DMA((nt, 7)), pltpu.SemaphoreType.DMA((nt, 7)), pltpu.SemaphoreType.DMA((nt,))),
        start=start,
        finish=finish,
    )


ADAM_BLOCK_BYTES = 768 * 1024


def _adamw(w, m, v, slot_list, name):
    shape = w.shape
    parts = len(slot_list)
    cols = shape[-1]
    rows = _size(shape) // cols // parts
    blk = max(r for r in range(8, rows + 1, 8) if rows % r == 0 and r * cols * 4 <= ADAM_BLOCK_BYTES) \
        if rows % 8 == 0 else rows

    def body(w_ref, m_ref, v_ref, *refs):
        slot_refs, (g_ref, d_ref, nm_ref, nv_ref) = refs[:parts], refs[parts:]
        g = None
        for k, gs_ref in enumerate(slot_refs):
            gk = gs_ref[0]
            for dev in range(1, 8):
                gk = gk + gs_ref[dev]
            g = gk if g is None else jnp.where(pl.program_id(0) == k, gk, g)
        w_, m_, v_ = w_ref[...], m_ref[...], v_ref[...]
        m_new = ADAM_B1 * m_ + (1.0 - ADAM_B1) * g
        v_new = ADAM_B2 * v_ + (1.0 - ADAM_B2) * (g * g)
        m_hat = m_new / (1.0 - ADAM_B1 ** ADAM_STEP)
        v_hat = v_new / (1.0 - ADAM_B2 ** ADAM_STEP)
        g_ref[...] = g
        d_ref[...] = -ADAM_LR * (m_hat / (jnp.sqrt(v_hat) + ADAM_EPS) + ADAM_WD * w_)
        nm_ref[...] = m_new
        nv_ref[...] = v_new

    spec = pl.BlockSpec((None, blk, cols), lambda p, i: (p, i, 0))
    slot_specs = [pl.BlockSpec((8, blk, cols), lambda p, i, k=k: (0, jnp.where(p == k, i, 0), 0)) for k in range(parts)]
    flat = (parts, rows, cols)
    outs = _pcall(
        body,
        grid=(parts, rows // blk),
        in_specs=[spec, spec, spec] + slot_specs,
        out_specs=[spec] * 4,
        out_shape=[jax.ShapeDtypeStruct(flat, F32)] * 4,
        compiler_params=_cp(("arbitrary", "arbitrary")),
        name=name,
    )(w.reshape(flat), m.reshape(flat), v.reshape(flat), *[s.reshape(8, rows, cols) for s in slot_list])
    return [o.reshape(shape) for o in outs]


_PARAMS = (
    ("meta", (N_META, D), 1),
    ("norm1", (N_LAYERS, D), None),
    ("w_in", (N_LAYERS, D, D_A + D_G), 2),
    ("conv_dw_w", (N_LAYERS, CONV_K, CONV_CH), 2),
    ("conv_dw_b", (N_LAYERS, CONV_CH), None),
    ("conv_ln_g", (N_LAYERS, CONV_CH), None),
    ("conv_ln_b", (N_LAYERS, CONV_CH), None),
    ("w_conv_out", (N_LAYERS, CONV_CH, D), 2),
    ("b_conv_out", (N_LAYERS, D), None),
    ("w_pool_grp", (N_LAYERS, len(POOL_WINDOWS), POOL_GC, D // len(POOL_WINDOWS)), 3),
    ("pool_scale", (N_LAYERS, D), None),
    ("w_attn_out", (N_LAYERS, ATT_W, D), 2),
    ("w_o", (N_LAYERS, D, D), 1),
    ("norm2", (N_LAYERS, D), None),
    ("w_up", (N_LAYERS, D, 2 * D_FF), 2),
    ("ffn_dw_w", (N_LAYERS, FFN_K, 2 * D_FF), 2),
    ("ffn_dw_b", (N_LAYERS, 2 * D_FF), None),
    ("w_down", (N_LAYERS, D_FF, D), 1),
    ("final_norm", (D,), None),
)
_BIG = ("w_in", "w_conv_out", "w_pool_grp", "w_attn_out", "w_o", "w_up", "w_down")
_SMALL_SHARDED = ("meta", "conv_dw_w", "ffn_dw_w")
_SHARD_AXIS = {n: ax for n, _, ax in _PARAMS}


def _size(shape):
    n = 1
    for d in shape:
        n *= d
    return n


def _pack(parts, lanes, row_multiple):
    flat = jnp.concatenate([a.reshape(-1) for a in parts])
    rows = -(-flat.shape[0] // lanes)
    rows = -(-rows // row_multiple) * row_multiple
    flat = jnp.pad(flat, (0, rows * lanes - flat.shape[0]))
    return flat.reshape(rows, lanes)


def _unpack(blob, shapes):
    flat = blob.reshape(-1)
    out, off = [], 0
    for s in shapes:
        n = _size(s)
        out.append(flat[off : off + n].reshape(s))
        off += n
    return out


def _shard(a, ax, s):
    n = a.shape[ax] // 4
    return lax.slice_in_dim(a, s * n, (s + 1) * n, axis=ax)


def kernel(x, meta, norm1, w_in, conv_dw_w, conv_dw_b, conv_ln_g, conv_ln_b, w_conv_out, b_conv_out, w_pool_grp, pool_scale, w_attn_out, w_o, norm2, w_up, ffn_dw_w, ffn_dw_b, w_down, final_norm, loss_target, m_meta, m_norm1, m_w_in, m_conv_dw_w, m_conv_dw_b, m_conv_ln_g, m_conv_ln_b, m_w_conv_out, m_b_conv_out, m_w_pool_grp, m_pool_scale, m_w_attn_out, m_w_o, m_norm2, m_w_up, m_ffn_dw_w, m_ffn_dw_b, m_w_down, m_final_norm, v_meta, v_norm1, v_w_in, v_conv_dw_w, v_conv_dw_b, v_conv_ln_g, v_conv_ln_b, v_w_conv_out, v_b_conv_out, v_w_pool_grp, v_pool_scale, v_w_attn_out, v_w_o, v_norm2, v_w_up, v_ffn_dw_w, v_ffn_dw_b, v_w_down, v_final_norm):
    names = [n for n, _, _ in _PARAMS]
    w_loc = dict(zip(names, (meta, norm1, w_in, conv_dw_w, conv_dw_b, conv_ln_g, conv_ln_b, w_conv_out, b_conv_out, w_pool_grp, pool_scale, w_attn_out, w_o, norm2, w_up, ffn_dw_w, ffn_dw_b, w_down, final_norm)))
    m_loc = dict(zip(names, (m_meta, m_norm1, m_w_in, m_conv_dw_w, m_conv_dw_b, m_conv_ln_g, m_conv_ln_b, m_w_conv_out, m_b_conv_out, m_w_pool_grp, m_pool_scale, m_w_attn_out, m_w_o, m_norm2, m_w_up, m_ffn_dw_w, m_ffn_dw_b, m_w_down, m_final_norm)))
    v_loc = dict(zip(names, (v_meta, v_norm1, v_w_in, v_conv_dw_w, v_conv_dw_b, v_conv_ln_g, v_conv_ln_b, v_w_conv_out, v_b_conv_out, v_w_pool_grp, v_pool_scale, v_w_attn_out, v_w_o, v_norm2, v_w_up, v_ffn_dw_w, v_ffn_dw_b, v_w_down, v_final_norm)))

    def operands(l):
        return [w_loc[n][l].astype(_MXU) for n in _BIG]

    def assemble(gathered):
        return {n: jnp.concatenate([g4[s] for s in range(4)], axis=_SHARD_AXIS[n] - 1) for n, g4 in zip(_BIG, gathered)}

    small = _pack([w_loc[n] for n in _SMALL_SHARDED], BLK, 8)
    *first, small4 = _transfer(_gather_side(operands(0) + [small]), "allgather_weights")
    full = {n: w_loc[n] for n, _, ax in _PARAMS if ax is None}
    full.update({n: [a, None] for n, a in assemble(first).items()})
    per_chip = [_unpack(small4[s], [w_loc[n].shape for n in _SMALL_SHARDED]) for s in range(4)]
    for k, n in enumerate(_SMALL_SHARDED):
        full[n] = jnp.concatenate([per_chip[s][k] for s in range(4)], axis=_SHARD_AXIS[n])

    def install(landed):
        for n, a in assemble(landed).items():
            full[n][1] = a

    def contributions(big):
        return [jnp.stack([_shard(big[n], _SHARD_AXIS[n] - 1, s) for s in range(4)]) for n in _BIG]

    loss, grad_x, grads, early = _local_step(
        x[0], loss_target[0], full, gather_next=(_gather_side(operands(1)), install),
        exchange_early=lambda big: _exchange_side(contributions(big)))

    small_names = [n for n in names if n not in _BIG]
    small_grads = {n: jnp.stack(grads[n]) if isinstance(grads[n], list) else grads[n] for n in small_names}

    def small_blob(src, s=None):
        return _pack([src[n] if s is None or _SHARD_AXIS[n] is None else _shard(src[n], _SHARD_AXIS[n], s)
                      for n in small_names], BLOB_LANES, 8)

    contribs = contributions({n: grads[n][0] for n in _BIG})
    contribs.append(jnp.stack([small_blob(small_grads, s) for s in range(4)]))
    *late, small_slots = _transfer(_exchange_side(contribs), "grad_exchange")
    results = {n: _adamw(w_loc[n], m_loc[n], v_loc[n], [late[k], early[k]], f"adamw_{n}") for k, n in enumerate(_BIG)}
    small_out = _adamw(small_blob(w_loc), small_blob(m_loc), small_blob(v_loc), [small_slots], "adamw_small")
    small_shapes = [w_loc[n].shape for n in small_names]
    for k, blob in enumerate(small_out):
        for n, a in zip(small_names, _unpack(blob, small_shapes)):
            results.setdefault(n, [None] * 4)[k] = a
    loss = lax.psum(loss[0, 0], ("x", "y", "c"))
    outs = [loss, grad_x[None]]
    for k in range(4):
        outs.extend(results[n][k] for n in names)
    return tuple(outs)
```

```python
import functools
from typing import Callable, NamedTuple

import jax
import jax.numpy as jnp
from jax import lax
from jax.experimental import pallas as pl
from jax.experimental.pallas import tpu as pltpu

F32 = jnp.float32
_MXU = jnp.bfloat16

D = 1024
N_META = 16
BLK = 128
CH = 2 * BLK
ATTN_UNROLLS = (8, 2, 1)
PAD_ROWS = BLK - N_META
N_LAYERS = 2
CONV_CH = 256
CONV_K = 31
POOL_CH = 256
POOL_WINDOWS = (2, 4, 8, 16)
POOL_GC = 64
ATT_W = 512
NH = 4
HD = 128
D_A = 2 * CONV_CH + POOL_CH + 3 * ATT_W
D_G = 3 * D
D_FF = 3 * D
FFN_K = 3
EPS = 1e-6
Q_SCALE = HD ** -0.5
LOG2E = 1.4426950408889634
LN2 = 0.6931471805599453

TR = 384
TB = 128
HALO = 32
HALO8 = 8
TC = 768
TF = 512
VMEM_LIMIT = 48 * 1024 * 1024
DW_STEPS = 12

ADAM_LR = 0.001
ADAM_B1 = 0.9
ADAM_B2 = 0.999
ADAM_EPS = 1e-08
ADAM_WD = 0.01
ADAM_STEP = 10

GELU_C0 = 0.7978845608028654
GELU_C1 = 0.044715

MESH = pl.DeviceIdType.MESH
ANY = pl.BlockSpec(memory_space=pl.ANY)

BLOB_LANES = 1024


def _pcall(body, **kw):
    return pl.pallas_call(body, **kw)


def _cp(sem):
    return pltpu.CompilerParams(dimension_semantics=sem, vmem_limit_bytes=VMEM_LIMIT)


def _dot(a, b):
    return jnp.dot(a.astype(_MXU), b.astype(_MXU), preferred_element_type=F32)


def _dot_t(a, b):
    return lax.dot_general(a.astype(_MXU), b.astype(_MXU), (((1,), (1,)), ((), ())), preferred_element_type=F32)


def _tdot(a, b):
    return lax.dot_general(a.astype(_MXU), b.astype(_MXU), (((0,), (0,)), ((), ())), preferred_element_type=F32)


def _sig(x):
    return 1.0 / (1.0 + jnp.exp(-x))


def _rms_fwd(x, g):
    r = lax.rsqrt(jnp.mean(x * x, axis=-1, keepdims=True) + EPS)
    n = x * r
    return r, n, n * g


def _rms_bwd(dy, r, n, g):
    dgain = jnp.sum(dy * n, axis=0, keepdims=True)
    dn = dy * g
    dx = r * (dn - n * jnp.mean(dn * n, axis=-1, keepdims=True))
    return dx, dgain


def _row_ids(tile_rows, i, shape):
    return i * tile_rows + lax.broadcasted_iota(jnp.int32, shape, 0)


class _Side(NamedTuple):
    arrays: tuple
    out_shape: tuple
    scratch: tuple
    start: Callable
    finish: Callable


def _hosted_call(body, side, grid, in_specs, out_specs, out_shape, scratch_shapes, name, operands):
    n_in, n_out, n_scr = len(in_specs), len(out_specs), len(scratch_shapes)
    if side is None:
        return _pcall(body, grid=grid, in_specs=in_specs, out_specs=out_specs, out_shape=out_shape,
                      scratch_shapes=scratch_shapes, compiler_params=_cp(("arbitrary",) * len(grid)), name=name)(*operands)
    s_in, s_out = len(side.arrays), len(side.out_shape)

    def hosted(*refs):
        ins, refs = refs[:n_in], refs[n_in:]
        side_ins, refs = refs[:s_in], refs[s_in:]
        outs, refs = refs[:n_out], refs[n_out:]
        side_outs, refs = refs[:s_out], refs[s_out:]
        scratch, side_scratch = refs[:n_scr], refs[n_scr:]
        ids = [pl.program_id(a) for a in range(len(grid))]
        first = functools.reduce(jnp.logical_and, [p == 0 for p in ids])
        last = functools.reduce(jnp.logical_and, [p == n - 1 for p, n in zip(ids, grid)])

        @pl.when(first)
        def _():
            side.start(side_ins, side_outs, side_scratch)

        body(*ins, *outs, *scratch)

        @pl.when(last)
        def _():
            side.finish(side_ins, side_outs, side_scratch)

    return _pcall(
        hosted,
        grid=grid,
        in_specs=list(in_specs) + [ANY] * s_in,
        out_specs=list(out_specs) + [ANY] * s_out,
        out_shape=list(out_shape) + list(side.out_shape),
        scratch_shapes=list(scratch_shapes) + list(side.scratch),
        compiler_params=_cp(("arbitrary",) * len(grid)),
        name=name,
    )(*operands, *side.arrays)


def _rms_matmul(h, gain, w, name, split_out=False, side=None):
    R, N = h.shape[0], w.shape[1]
    nj = N // TC
    tr = R // DW_STEPS
    half = nj // 2

    def body(h_ref, g_ref, w_ref, o_ref, xn_ref):
        @pl.when(pl.program_id(1) == 0)
        def _():
            _, _, y = _rms_fwd(h_ref[...], g_ref[...])
            xn_ref[...] = y.astype(xn_ref.dtype)

        o_ref[...] = jnp.dot(xn_ref[...], w_ref[...], preferred_element_type=F32)

    if split_out:
        out_shape = jax.ShapeDtypeStruct((2, R, N // 2), F32)
        out_spec = pl.BlockSpec((None, tr, TC), lambda i, j: (j // half, i, j % half))
    else:
        out_shape = jax.ShapeDtypeStruct((R, N), F32)
        out_spec = pl.BlockSpec((tr, TC), lambda i, j: (i, j))
    res = _hosted_call(
        body, side,
        grid=(R // tr, nj),
        in_specs=[
            pl.BlockSpec((tr, D), lambda i, j: (i, 0)),
            pl.BlockSpec((1, D), lambda i, j: (0, 0)),
            pl.BlockSpec((D, TC), lambda i, j: (0, j)),
        ],
        out_specs=[out_spec],
        out_shape=[out_shape],
        scratch_shapes=[pltpu.VMEM((tr, D), _MXU)],
        name=name,
        operands=(h, gain, w.astype(_MXU)),
    )
    return res[0] if side is None else res


def _qkv_cast(proj_a, name):
    R = proj_a.shape[0]

    def body(p_ref, o_ref):
        col = lax.broadcasted_iota(jnp.int32, (1, TC), 1) + pl.program_id(1) * TC
        sc = jnp.where(col < ATT_W, Q_SCALE * LOG2E, 1.0).astype(F32)
        o_ref[...] = (p_ref[...] * sc).astype(o_ref.dtype)

    return _pcall(
        body,
        grid=(R // TR, 2),
        in_specs=[pl.BlockSpec((TR, TC), lambda i, j: (i, j + 1))],
        out_specs=pl.BlockSpec((TR, TC), lambda i, j: (i, j)),
        out_shape=jax.ShapeDtypeStruct((R, 3 * ATT_W), _MXU),
        compiler_params=_cp(("parallel", "parallel")),
        name=name,
    )(proj_a)


def _sum_mat(w):
    rowi = lax.broadcasted_iota(jnp.int32, (w, w), 0)
    coli = lax.broadcasted_iota(jnp.int32, (w, w), 1)
    return (rowi > coli).astype(_MXU)


def _hi_lo_rows(x):
    hi = x.astype(_MXU)
    lo = (x - hi.astype(F32)).astype(_MXU)
    return jnp.concatenate([hi, lo], axis=0)


def _fold_sums(r):
    return r[:BLK] + r[BLK:]


def _attn_logs(z2, valid):
    m = jnp.minimum(z2, 0.0)
    d = m - z2
    t = jnp.log2(1.0 + jnp.exp2(m + d))
    lk = d - t
    if valid is not None:
        lk = jnp.where(valid, lk, 0.0)
    return lk, m - t


def _attn_weights(log_beta, r, valid, s_after):
    log_a = log_beta + r + s_after
    if valid is not None:
        log_a = jnp.where(valid, log_a, -1e30)
    return jnp.exp2(log_a)


def _attn_walk(i, phases, n_free, carry):
    ci = jnp.maximum(i - 1, 0) // 2

    def advance(states, stages, cr):
        for ph in stages:
            for u in range(len(states)):
                states[u], cr = ph(states[u], cr)
        return states, cr

    def at(c):
        return pl.multiple_of(BLK + c * CH, BLK)

    ends = [phases[0](at(ci), CH, True), phases[0](0, BLK, True)]
    ends, _ = advance(ends, phases[1 : 1 + n_free], None)
    _, carry = advance(ends[:1], phases[1 + n_free :], carry)
    left = ci
    for unroll in ATTN_UNROLLS:
        def group(t, cr, unroll=unroll, left=left):
            states = [phases[0](at(left - 1 - unroll * t - u), CH, False) for u in range(unroll)]
            return advance(states, phases[1:], cr)[1]

        carry = lax.fori_loop(0, left // unroll, group, carry)
        left = left % unroll
    return advance(ends[1:], phases[1 + n_free :], carry)[1]


def _attn_mask(i, off, w):
    qpos = i * BLK + lax.broadcasted_iota(jnp.int32, (BLK, w), 0)
    kpos = off + lax.broadcasted_iota(jnp.int32, (BLK, w), 1)
    return jnp.logical_and(kpos < qpos, kpos >= PAD_ROWS)


def _attn_fwd(qkv, name):
    R = qkv.shape[0]
    nb = R // BLK

    def body(q_ref, k_ref, v_ref, o_ref):
        i = pl.program_id(1)
        q = q_ref[...]
        sms = {CH: _sum_mat(CH), BLK: _sum_mat(BLK)}

        def scores(off, w, masked):
            valid = _attn_mask(i, off, w) if masked else None
            return off, w, valid, _dot_t(q, k_ref[pl.ds(off, w), :])

        def sums(state, carry):
            off, w, valid, z = state
            lk, log_beta = _attn_logs(z, valid)
            return (off, w, valid, log_beta, _dot(lk, sms[w]), jnp.sum(lk, axis=1, keepdims=True)), carry

        def output(state, carry):
            off, w, valid, log_beta, r, total = state
            acc, s_after = carry
            a = _attn_weights(log_beta, r, valid, s_after)
            return None, (acc + _dot(a, v_ref[pl.ds(off, w), :]), s_after + total)

        acc, _ = _attn_walk(i, (scores, sums, output), 1, (jnp.zeros((BLK, HD), F32), jnp.zeros((BLK, 1), F32)))
        o_ref[...] = acc

    return _pcall(
        body,
        grid=(NH, nb),
        in_specs=[
            pl.BlockSpec((BLK, HD), lambda h, i: (i, h)),
            pl.BlockSpec((R, HD), lambda h, i: (0, NH + h)),
            pl.BlockSpec((R, HD), lambda h, i: (0, 2 * NH + h)),
        ],
        out_specs=pl.BlockSpec((BLK, HD), lambda h, i: (i, h)),
        out_shape=jax.ShapeDtypeStruct((R, ATT_W), F32),
        compiler_params=_cp(("parallel", "arbitrary")),
        name=name,
    )(qkv, qkv, qkv)


def _pool_consts(i):
    lane = lax.broadcasted_iota(jnp.int32, (1, POOL_CH), 1)
    wsize = jnp.where(lane < POOL_GC, 2.0, jnp.where(lane < 2 * POOL_GC, 4.0, jnp.where(lane < 3 * POOL_GC, 8.0, 16.0)))
    return lane, wsize


def _pool_div(rows, wsize):
    pos1 = (rows - (PAD_ROWS - 1)).astype(F32)
    return jnp.clip(pos1, 1.0, wsize)


def _lane_select(lane, s2, s4, s8, s16):
    return jnp.where(lane < POOL_GC, s2, jnp.where(lane < 2 * POOL_GC, s4, jnp.where(lane < 3 * POOL_GC, s8, s16)))


def _branch_pre(proj_a, conv_w, conv_b, name):
    R = proj_a.shape[0]
    nh = TR // HALO

    def body(t_ref, h_ref, w_ref, b_ref, c_ref, p_ref, gext, pext):
        i = pl.program_id(0)
        t = t_ref[...]
        hl = jnp.where(i == 0, 0.0, h_ref[...])
        gext[pl.ds(0, HALO), :] = hl[:, :CONV_CH] * _sig(hl[:, CONV_CH : 2 * CONV_CH])
        gext[pl.ds(HALO, TR), :] = t[:, :CONV_CH] * _sig(t[:, CONV_CH : 2 * CONV_CH])
        acc = jnp.zeros((TR, CONV_CH), F32) + b_ref[...]
        for k in range(CONV_K):
            acc = acc + w_ref[pl.ds(k, 1), :] * gext[pl.ds(HALO - (CONV_K - 1) + k, TR), :]
        c_ref[...] = acc

        p = t[:, 2 * CONV_CH :]
        pext[pl.ds(0, HALO), :] = hl[:, 2 * CONV_CH :]
        pext[pl.ds(HALO, TR), :] = p

        def back(k):
            return pext[pl.ds(HALO - k, TR), :]

        s2 = p + back(1)
        s4 = s2 + back(2) + back(3)
        s8 = s4 + back(4) + back(5) + back(6) + back(7)
        s16 = s8
        for k in range(8, 16):
            s16 = s16 + back(k)
        lane, wsize = _pool_consts(i)
        div = _pool_div(_row_ids(TR, i, (TR, POOL_CH)), wsize)
        p_ref[...] = (_lane_select(lane, s2, s4, s8, s16) / div - p).astype(p_ref.dtype)

    return _pcall(
        body,
        grid=(R // TR,),
        in_specs=[
            pl.BlockSpec((TR, TC), lambda i: (i, 0)),
            pl.BlockSpec((HALO, TC), lambda i: (jnp.maximum(i * nh - 1, 0), 0)),
            pl.BlockSpec((HALO, CONV_CH), lambda i: (0, 0)),
            pl.BlockSpec((1, CONV_CH), lambda i: (0, 0)),
        ],
        out_specs=[pl.BlockSpec((TR, CONV_CH), lambda i: (i, 0)), pl.BlockSpec((TR, POOL_CH), lambda i: (i, 0))],
        out_shape=[jax.ShapeDtypeStruct((R, CONV_CH), F32), jax.ShapeDtypeStruct((R, POOL_CH), _MXU)],
        scratch_shapes=[pltpu.VMEM((TR + HALO, CONV_CH), F32), pltpu.VMEM((TR + HALO, POOL_CH), F32)],
        compiler_params=_cp(("parallel",)),
        name=name,
    )(proj_a, proj_a, conv_w, conv_b)


def _mix_values(c, pooled, att, gates, ln_g, ln_b, w_co, b_co, w_bd, p_scale, w_ao):
    mu = jnp.mean(c, axis=-1, keepdims=True)
    xc = c - mu
    rstd = lax.rsqrt(jnp.mean(xc * xc, axis=-1, keepdims=True) + EPS)
    nl = xc * rstd
    ln = nl * ln_g + ln_b
    sg = _sig(ln)
    s = ln * sg
    ya = _dot(s, w_co) + b_co
    ybr = _dot(pooled, w_bd)
    yb = ybr * p_scale
    yc = _dot(att, w_ao)
    g = _sig(gates)
    g0, g1, g2 = g[:, :D], g[:, D : 2 * D], g[:, 2 * D :]
    mixed = g0 * ya + g1 * yb + g2 * yc
    return dict(rstd=rstd, nl=nl, ln=ln, sg=sg, s=s, ya=ya, ybr=ybr, yb=yb, yc=yc, g0=g0, g1=g1, g2=g2, mixed=mixed)


_MIX_W_SPECS = [
    pl.BlockSpec((1, CONV_CH), lambda i: (0, 0)),
    pl.BlockSpec((1, CONV_CH), lambda i: (0, 0)),
    pl.BlockSpec((CONV_CH, D), lambda i: (0, 0)),
    pl.BlockSpec((1, D), lambda i: (0, 0)),
    pl.BlockSpec((POOL_CH, D), lambda i: (0, 0)),
    pl.BlockSpec((1, D), lambda i: (0, 0)),
    pl.BlockSpec((ATT_W, D), lambda i: (0, 0)),
    pl.BlockSpec((D, D), lambda i: (0, 0)),
]


def _mix_act_specs(t):
    return [
        pl.BlockSpec((t, D_G), lambda i: (i, 0)),
        pl.BlockSpec((t, CONV_CH), lambda i: (i, 0)),
        pl.BlockSpec((t, POOL_CH), lambda i: (i, 0)),
        pl.BlockSpec((t, ATT_W), lambda i: (i, 0)),
    ]


def _mix_fwd(h, gates, c, pooled, att, mw, name):
    R = h.shape[0]

    def body(h_ref, g_ref, c_ref, p_ref, a_ref, lg, lb, wco, bco, wbd, ps, wao, wo, o_ref):
        v = _mix_values(c_ref[...], p_ref[...], a_ref[...], g_ref[...], lg[...], lb[...], wco[...], bco[...],
                        wbd[...], ps[...], wao[...])
        out = h_ref[...] + _dot(v["mixed"], wo[...])
        rows = _row_ids(TB, pl.program_id(0), (TB, D))
        o_ref[...] = jnp.where(rows >= PAD_ROWS, out, 0.0)

    return _pcall(
        body,
        grid=(R // TB,),
        in_specs=[pl.BlockSpec((TB, D), lambda i: (i, 0))] + _mix_act_specs(TB) + _MIX_W_SPECS,
        out_specs=pl.BlockSpec((TB, D), lambda i: (i, 0)),
        out_shape=jax.ShapeDtypeStruct((R, D), F32),
        compiler_params=_cp(("parallel",)),
        name=name,
    )(h, gates, c, pooled, att, *mw)


def _ffn_conv(ut, uh, cw_ref, cb_ref, ext):
    ext[:, pl.ds(0, HALO8), :] = uh
    ext[:, pl.ds(HALO8, TR), :] = ut
    um1 = ext[:, pl.ds(HALO8 - 1, TR), :]
    um2 = ext[:, pl.ds(HALO8 - 2, TR), :]
    cw = cw_ref[...]
    conv = cw[0][:, None, :] * um2 + cw[1][:, None, :] * um1 + cw[2][:, None, :] * ut + cb_ref[...][:, None, :]
    return conv, um1, um2


def _gelu_parts(x):
    th = jnp.tanh(GELU_C0 * (x + GELU_C1 * x * x * x))
    return th, 0.5 * x * (1.0 + th)


def _ffn_in_specs(nrow8, order):
    n8 = TR // HALO8
    return [
        pl.BlockSpec((2, TR, TF), lambda *g: (0, order(*g)[0], order(*g)[1])),
        pl.BlockSpec((2, HALO8, TF), lambda *g: (0, jnp.maximum(order(*g)[0] * n8 - 1, 0), order(*g)[1])),
        pl.BlockSpec((FFN_K, 2, TF), lambda *g: (0, 0, order(*g)[1])),
        pl.BlockSpec((2, TF), lambda *g: (0, order(*g)[1])),
    ]


def _ffn_fwd(u3, h2, cw, cb, w_down, name):
    R = h2.shape[0]
    nj = D_FF // TF

    def body(u_ref, uh_ref, cw_ref, cb_ref, wd_ref, h_ref, o_ref, ext, acc):
        i, j = pl.program_id(0), pl.program_id(1)
        uh = jnp.where(i == 0, 0.0, uh_ref[...])
        conv, _, _ = _ffn_conv(u_ref[...], uh, cw_ref, cb_ref, ext)
        _, a = _gelu_parts(conv[0])
        part = _dot(a * conv[1], wd_ref[...])

        @pl.when(j == 0)
        def _():
            acc[...] = part

        @pl.when(j > 0)
        def _():
            acc[...] += part

        @pl.when(j == nj - 1)
        def _():
            rows = _row_ids(TR, i, (TR, D))
            o_ref[...] = jnp.where(rows >= PAD_ROWS, h_ref[...] + acc[...], 0.0)

    return _pcall(
        body,
        grid=(R // TR, nj),
        in_specs=_ffn_in_specs(R // HALO8, lambda i, j: (i, j))
        + [pl.BlockSpec((TF, D), lambda i, j: (j, 0)), pl.BlockSpec((TR, D), lambda i, j: (i, 0))],
        out_specs=pl.BlockSpec((TR, D), lambda i, j: (i, 0)),
        out_shape=jax.ShapeDtypeStruct((R, D), F32),
        scratch_shapes=[pltpu.VMEM((2, TR + HALO8, TF), F32), pltpu.VMEM((TR, D), F32)],
        compiler_params=_cp(("parallel", "arbitrary")),
        name=name,
    )(u3, u3, cw, cb, w_down.astype(_MXU), h2)


def _loss_bwd(h, target, gain, name):
    R = h.shape[0]

    def body(h_ref, t_ref, g_ref, dh_ref, loss_ref, dg_ref):
        i = pl.program_id(0)

        @pl.when(i == 0)
        def _():
            loss_ref[...] = jnp.zeros_like(loss_ref)
            dg_ref[...] = jnp.zeros_like(dg_ref)
            dh_ref[...] = jnp.zeros_like(dh_ref)

        @pl.when(i > 0)
        def _():
            g = g_ref[...]
            r, n, y = _rms_fwd(h_ref[...], g)
            e = y - t_ref[...]
            loss_ref[...] += (0.5 / D) * jnp.sum(jnp.sum(e * e, axis=1, keepdims=True), axis=0, keepdims=True)
            dx, dgain = _rms_bwd(e * (1.0 / D), r, n, g)
            dg_ref[...] += dgain
            dh_ref[...] = dx

    return _pcall(
        body,
        grid=(R // BLK,),
        in_specs=[
            pl.BlockSpec((BLK, D), lambda i: (i, 0)),
            pl.BlockSpec((BLK, D), lambda i: (jnp.maximum(i - 1, 0), 0)),
            pl.BlockSpec((1, D), lambda i: (0, 0)),
        ],
        out_specs=[
            pl.BlockSpec((BLK, D), lambda i: (i, 0)),
            pl.BlockSpec((1, 1), lambda i: (0, 0)),
            pl.BlockSpec((1, D), lambda i: (0, 0)),
        ],
        out_shape=[
            jax.ShapeDtypeStruct((R, D), F32),
            jax.ShapeDtypeStruct((1, 1), F32),
            jax.ShapeDtypeStruct((1, D), F32),
        ],
        compiler_params=_cp(("arbitrary",)),
        name=name,
    )(h, target, gain)


DW_BLOCK_BYTES = 6 * 1024 * 1024


def _tdot_acc(a, b, name):
    R, M = a.shape
    split = b.ndim == 3
    width = b.shape[-1]
    N = 2 * width if split else width
    tm = min(M, 512)
    tk = R // DW_STEPS
    tn = max(t for t in range(BLK, width + 1, BLK) if width % t == 0 and tk * t * b.dtype.itemsize <= DW_BLOCK_BYTES)
    per = width // tn

    def body(a_ref, b_ref, o_ref):
        part = _tdot(a_ref[...], b_ref[...])

        @pl.when(pl.program_id(2) == 0)
        def _():
            o_ref[...] = part

        @pl.when(pl.program_id(2) > 0)
        def _():
            o_ref[...] += part

    if split:
        b_spec = pl.BlockSpec((None, tk, tn), lambda m, n, k: (n // per, k, n % per))
    else:
        b_spec = pl.BlockSpec((tk, tn), lambda m, n, k: (k, n))
    return _pcall(
        body,
        grid=(M // tm, N // tn, R // tk),
        in_specs=[pl.BlockSpec((tk, tm), lambda m, n, k: (k, m)), b_spec],
        out_specs=pl.BlockSpec((tm, tn), lambda m, n, k: (m, n)),
        out_shape=jax.ShapeDtypeStruct((M, N), F32),
        compiler_params=_cp(("parallel", "parallel", "arbitrary")),
        name=name,
    )(a, b)


def _ffn_bwd1(dh3, u3, cw, cb, w_down, name, side=None):
    R = dh3.shape[0]
    nj = D_FF // TF

    def body(u_ref, uh_ref, cw_ref, cb_ref, wd_ref, dh_ref, dc_ref, dwd_ref, dwb_ref, ext):
        j, i = pl.program_id(0), pl.program_id(1)
        ut = u_ref[...]
        uh = jnp.where(i == 0, 0.0, uh_ref[...])
        conv, um1, um2 = _ffn_conv(ut, uh, cw_ref, cb_ref, ext)
        gt, val = conv[0], conv[1]
        th, a = _gelu_parts(gt)
        dh = dh_ref[...].astype(_MXU)
        dact = _dot_t(dh, wd_ref[...])
        dgelu = 0.5 * (1.0 + th) + 0.5 * gt * (1.0 - th * th) * (GELU_C0 * (1.0 + 3.0 * GELU_C1 * gt * gt))
        dgt = dact * val * dgelu
        dval = dact * a
        dc_ref[0] = dgt
        dc_ref[1] = dval

        @pl.when(i == 0)
        def _():
            dwb_ref[...] = jnp.zeros_like(dwb_ref)
            dwd_ref[...] = jnp.zeros_like(dwd_ref)

        dwd_ref[...] += _tdot(a * val, dh)

        for half, dcv in ((0, dgt), (1, dval)):
            dwb_ref[half, pl.ds(0, 1), :] += jnp.sum(um2[half] * dcv, axis=0, keepdims=True)
            dwb_ref[half, pl.ds(1, 1), :] += jnp.sum(um1[half] * dcv, axis=0, keepdims=True)
            dwb_ref[half, pl.ds(2, 1), :] += jnp.sum(ut[half] * dcv, axis=0, keepdims=True)
            dwb_ref[half, pl.ds(3, 1), :] += jnp.sum(dcv, axis=0, keepdims=True)

    return _hosted_call(
        body, side,
        grid=(nj, R // TR),
        in_specs=_ffn_in_specs(R // HALO8, lambda j, i: (i, j))
        + [pl.BlockSpec((TF, D), lambda j, i: (j, 0)), pl.BlockSpec((TR, D), lambda j, i: (i, 0))],
        out_specs=[
            pl.BlockSpec((2, TR, TF), lambda j, i: (0, i, j)),
            pl.BlockSpec((TF, D), lambda j, i: (j, 0)),
            pl.BlockSpec((2, 8, TF), lambda j, i: (0, 0, j)),
        ],
        out_shape=[
            jax.ShapeDtypeStruct((2, R, D_FF), F32),
            jax.ShapeDtypeStruct((D_FF, D), F32),
            jax.ShapeDtypeStruct((2, 8, D_FF), F32),
        ],
        scratch_shapes=[pltpu.VMEM((2, TR + HALO8, TF), F32)],
        name=name,
        operands=(u3, u3, cw, cb, w_down.astype(_MXU), dh3),
    )


def _ffn_bwd2(dc3, cw, w_up, h2, dh3, gain, name):
    R = h2.shape[0]
    nj = D_FF // TF
    n8 = TR // HALO8
    last8 = R // HALO8 - 1
    ni = R // TR

    def body(dc_ref, dn_ref, cw_ref, wg_ref, wv_ref, h_ref, dh_ref, g_ref, du_ref, o_ref, xn_ref, dg_ref, ext, acc):
        i, j = pl.program_id(0), pl.program_id(1)
        dc = dc_ref[...]
        ext[:, pl.ds(0, TR), :] = dc
        ext[:, pl.ds(TR, HALO8), :] = jnp.where(i == ni - 1, 0.0, dn_ref[...])
        cw = cw_ref[...]
        du = (cw[2][:, None, :] * dc + cw[1][:, None, :] * ext[:, pl.ds(1, TR), :]
              + cw[0][:, None, :] * ext[:, pl.ds(2, TR), :])
        du_ref[...] = du.astype(du_ref.dtype)
        part = _dot_t(du[0], wg_ref[...]) + _dot_t(du[1], wv_ref[...])

        @pl.when(j == 0)
        def _():
            acc[...] = part

        @pl.when(j > 0)
        def _():
            acc[...] += part

        @pl.when(jnp.logical_and(i == 0, j == 0))
        def _():
            dg_ref[...] = jnp.zeros_like(dg_ref)

        @pl.when(j == nj - 1)
        def _():
            g = g_ref[...]
            r, n, y = _rms_fwd(h_ref[...], g)
            dx, dgain = _rms_bwd(acc[...], r, n, g)
            dg_ref[...] += dgain
            rows = _row_ids(TR, i, (TR, D))
            o_ref[...] = jnp.where(rows >= PAD_ROWS, dh_ref[...] + dx, 0.0)
            xn_ref[...] = y.astype(xn_ref.dtype)

    w_up = w_up.astype(_MXU)
    return _pcall(
        body,
        grid=(ni, nj),
        in_specs=[
            pl.BlockSpec((2, TR, TF), lambda i, j: (0, i, j)),
            pl.BlockSpec((2, HALO8, TF), lambda i, j: (0, jnp.minimum((i + 1) * n8, last8), j)),
            pl.BlockSpec((FFN_K, 2, TF), lambda i, j: (0, 0, j)),
            pl.BlockSpec((D, TF), lambda i, j: (0, j)),
            pl.BlockSpec((D, TF), lambda i, j: (0, nj + j)),
            pl.BlockSpec((TR, D), lambda i, j: (i, 0)),
            pl.BlockSpec((TR, D), lambda i, j: (i, 0)),
            pl.BlockSpec((1, D), lambda i, j: (0, 0)),
        ],
        out_specs=[
            pl.BlockSpec((2, TR, TF), lambda i, j: (0, i, j)),
            pl.BlockSpec((TR, D), lambda i, j: (i, 0)),
            pl.BlockSpec((TR, D), lambda i, j: (i, 0)),
            pl.BlockSpec((1, D), lambda i, j: (0, 0)),
        ],
        out_shape=[
            jax.ShapeDtypeStruct((2, R, D_FF), _MXU),
            jax.ShapeDtypeStruct((R, D), F32),
            jax.ShapeDtypeStruct((R, D), _MXU),
            jax.ShapeDtypeStruct((1, D), F32),
        ],
        scratch_shapes=[pltpu.VMEM((2, TR + HALO8, TF), F32), pltpu.VMEM((TR, D), F32)],
        compiler_params=_cp(("arbitrary", "arbitrary")),
        name=name,
    )(dc3, dc3, cw, w_up, w_up, h2, dh3, gain)


def _mix_bwd(dh2, gates, c, pooled, att, mw, name):
    R = dh2.shape[0]

    def body(dh_ref, g_ref, c_ref, p_ref, a_ref, lg, lb, wco, bco, wbd, ps, wao, wo,
             dg_ref, dc_ref, dp_ref, da_ref, mx_ref, s_ref, dya_ref, dyb_ref, dyc_ref, accd_ref, accc_ref):
        v = _mix_values(c_ref[...], p_ref[...], a_ref[...], g_ref[...], lg[...], lb[...], wco[...], bco[...],
                        wbd[...], ps[...], wao[...])
        dmix = _dot_t(dh_ref[...], wo[...])
        for k, (gk, yk) in enumerate(((v["g0"], v["ya"]), (v["g1"], v["yb"]), (v["g2"], v["yc"]))):
            dg_ref[:, k * D : (k + 1) * D] = dmix * yk * gk * (1.0 - gk)
        dya = dmix * v["g0"]
        dyb = dmix * v["g1"]
        dyc = dmix * v["g2"]
        ds = _dot_t(dya, wco[...])
        ln, sg, nl = v["ln"], v["sg"], v["nl"]
        dln = ds * (sg * (1.0 + ln * (1.0 - sg)))
        dn = dln * lg[...]
        dc = v["rstd"] * (dn - jnp.mean(dn, axis=-1, keepdims=True) - nl * jnp.mean(dn * nl, axis=-1, keepdims=True))
        dybs = dyb * ps[...]
        dc_ref[...] = dc
        dp_ref[...] = _dot_t(dybs, wbd[...])
        da_ref[...] = _dot_t(dyc, wao[...])
        mx_ref[...] = v["mixed"].astype(mx_ref.dtype)
        s_ref[...] = v["s"].astype(s_ref.dtype)
        dya_ref[...] = dya.astype(dya_ref.dtype)
        dyb_ref[...] = dybs.astype(dyb_ref.dtype)
        dyc_ref[...] = dyc.astype(dyc_ref.dtype)

        @pl.when(pl.program_id(0) == 0)
        def _():
            accd_ref[...] = jnp.zeros_like(accd_ref)
            accc_ref[...] = jnp.zeros_like(accc_ref)

        accd_ref[pl.ds(0, 1), :] += jnp.sum(dya, axis=0, keepdims=True)
        accd_ref[pl.ds(1, 1), :] += jnp.sum(dyb * v["ybr"], axis=0, keepdims=True)
        accc_ref[pl.ds(0, 1), :] += jnp.sum(dln * nl, axis=0, keepdims=True)
        accc_ref[pl.ds(1, 1), :] += jnp.sum(dln, axis=0, keepdims=True)
        accc_ref[pl.ds(2, 1), :] += jnp.sum(dc, axis=0, keepdims=True)

    def row(w):
        return pl.BlockSpec((TB, w), lambda i: (i, 0))

    return _pcall(
        body,
        grid=(R // TB,),
        in_specs=[row(D)] + _mix_act_specs(TB) + _MIX_W_SPECS,
        out_specs=[row(D_G), row(CONV_CH), row(POOL_CH), row(ATT_W), row(D), row(CONV_CH), row(D), row(D), row(D),
                   pl.BlockSpec((8, D), lambda i: (0, 0)), pl.BlockSpec((8, CONV_CH), lambda i: (0, 0))],
        out_shape=[
            jax.ShapeDtypeStruct((R, D_G), F32),
            jax.ShapeDtypeStruct((R, CONV_CH), F32),
            jax.ShapeDtypeStruct((R, POOL_CH), F32),
            jax.ShapeDtypeStruct((R, ATT_W), F32),
            jax.ShapeDtypeStruct((R, D), _MXU),
            jax.ShapeDtypeStruct((R, CONV_CH), _MXU),
            jax.ShapeDtypeStruct((R, D), _MXU),
            jax.ShapeDtypeStruct((R, D), _MXU),
            jax.ShapeDtypeStruct((R, D), _MXU),
            jax.ShapeDtypeStruct((8, D), F32),
            jax.ShapeDtypeStruct((8, CONV_CH), F32),
        ],
        compiler_params=_cp(("arbitrary",)),
        name=name,
    )(dh2, gates, c, pooled, att, *mw)


def _attn_bwd(qkv, att, datt, name):
    R = qkv.shape[0]
    nb = R // BLK

    def body(q_ref, k_ref, v_ref, o_ref, do_ref, dq_ref, dk_ref, dv_ref):
        i = pl.program_id(1)

        @pl.when(i == 0)
        def _():
            dk_ref[...] = jnp.zeros_like(dk_ref)
            dv_ref[...] = jnp.zeros_like(dv_ref)

        q = q_ref[...]
        dob = do_ref[...].astype(_MXU)
        dof = dob.astype(F32)
        sms = {CH: _sum_mat(CH), BLK: _sum_mat(BLK)}
        e_all = jnp.sum(dof * o_ref[...], axis=1, keepdims=True)

        def scores(off, w, masked):
            valid = _attn_mask(i, off, w) if masked else None
            z = _dot_t(q, k_ref[pl.ds(off, w), :])
            return off, w, valid, z, _dot_t(dob, v_ref[pl.ds(off, w), :])

        def sums(state, carry):
            off, w, valid, z, da = state
            lk, log_beta = _attn_logs(z, valid)
            return (off, w, valid, log_beta, _dot(lk, sms[w]), jnp.sum(lk, axis=1, keepdims=True), da), carry

        def weights(state, carry):
            off, w, valid, log_beta, r, total, da = state
            dq, s_after, e_done = carry
            ab = _attn_weights(log_beta, r, valid, s_after).astype(_MXU)
            e = ab.astype(F32) * da
            re = jnp.dot(_hi_lo_rows(e), sms[w], preferred_element_type=F32)
            return (off, w, valid, jnp.exp2(log_beta), ab, e, re), (dq, s_after + total, e_done)

        def grads(state, carry):
            off, w, valid, beta, ab, e, re = state
            dq, s_after, e_done = carry
            dz = e - beta * ((e_all - e_done) - _fold_sums(re))
            if valid is not None:
                dz = jnp.where(valid, dz, 0.0)
            dzb = (dz * LN2).astype(_MXU)
            dk_ref[pl.ds(off, w), :] += _tdot(dzb, q)
            dv_ref[pl.ds(off, w), :] += _tdot(ab, dob)
            dq = dq + jnp.dot(dzb, k_ref[pl.ds(off, w), :], preferred_element_type=F32)
            return None, (dq, s_after, e_done + jnp.sum(e, axis=1, keepdims=True))

        zero = jnp.zeros((BLK, 1), F32)
        dq, _, _ = _attn_walk(i, (scores, sums, weights, grads), 1, (jnp.zeros((BLK, HD), F32), zero, zero))
        dq_ref[...] = dq * (Q_SCALE * LOG2E)

    blk = pl.BlockSpec((BLK, HD), lambda h, i: (i, h))
    col = pl.BlockSpec((R, HD), lambda h, i: (0, h))
    return _pcall(
        body,
        grid=(NH, nb),
        in_specs=[
            blk,
            pl.BlockSpec((R, HD), lambda h, i: (0, NH + h)),
            pl.BlockSpec((R, HD), lambda h, i: (0, 2 * NH + h)),
            blk,
            blk,
        ],
        out_specs=[blk, col, col],
        out_shape=[jax.ShapeDtypeStruct((R, ATT_W), F32)] * 3,
        compiler_params=pltpu.CompilerParams(dimension_semantics=("arbitrary", "arbitrary"),
                                             vmem_limit_bytes=56 * 1024 * 1024),
        name=name,
    )(qkv, qkv, qkv, att, datt)


def _branch_bwd(dc, dpooled, proj_a, conv_w, name):
    R = dc.shape[0]
    nh = TR // HALO
    last = R // HALO - 1
    ni = R // TR

    def body(dc_ref, dcn_ref, dp_ref, dpn_ref, t_ref, h_ref, w_ref, o_ref, dcw_ref, gext, dcext, eext):
        i = pl.program_id(0)
        is_last = i == ni - 1
        t = t_ref[...]
        hl = jnp.where(i == 0, 0.0, h_ref[...])
        a = t[:, :CONV_CH]
        sg = _sig(t[:, CONV_CH : 2 * CONV_CH])
        gext[pl.ds(0, HALO), :] = hl[:, :CONV_CH] * _sig(hl[:, CONV_CH : 2 * CONV_CH])
        gext[pl.ds(HALO, TR), :] = a * sg
        dct = dc_ref[...]
        dcext[pl.ds(0, TR), :] = dct
        dcext[pl.ds(TR, HALO), :] = jnp.where(is_last, 0.0, dcn_ref[...])

        @pl.when(i == 0)
        def _():
            dcw_ref[...] = jnp.zeros_like(dcw_ref)

        dglu = jnp.zeros((TR, CONV_CH), F32)
        for k in range(CONV_K):
            dglu = dglu + w_ref[pl.ds(k, 1), :] * dcext[pl.ds(CONV_K - 1 - k, TR), :]
            dcw_ref[pl.ds(k, 1), :] += jnp.sum(gext[pl.ds(HALO - (CONV_K - 1) + k, TR), :] * dct, axis=0, keepdims=True)
        o_ref[:, :CONV_CH] = dglu * sg
        o_ref[:, CONV_CH : 2 * CONV_CH] = dglu * a * sg * (1.0 - sg)

        lane, wsize = _pool_consts(i)
        dpt = dp_ref[...]
        eext[pl.ds(0, TR), :] = dpt / _pool_div(_row_ids(TR, i, (TR, POOL_CH)), wsize)
        nxt = dpn_ref[...] / _pool_div(_row_ids(TR, i + 1, (HALO, POOL_CH)), wsize)
        eext[pl.ds(TR, HALO), :] = jnp.where(is_last, 0.0, nxt)

        def fwd(k):
            return eext[pl.ds(k, TR), :]

        s2 = fwd(0) + fwd(1)
        s4 = s2 + fwd(2) + fwd(3)
        s8 = s4 + fwd(4) + fwd(5) + fwd(6) + fwd(7)
        s16 = s8
        for k in range(8, 16):
            s16 = s16 + fwd(k)
        o_ref[:, 2 * CONV_CH :] = _lane_select(lane, s2, s4, s8, s16) - dpt

    def nxt_spec(w):
        return pl.BlockSpec((HALO, w), lambda i: (jnp.minimum((i + 1) * nh, last), 0))

    return _pcall(
        body,
        grid=(ni,),
        in_specs=[
            pl.BlockSpec((TR, CONV_CH), lambda i: (i, 0)),
            nxt_spec(CONV_CH),
            pl.BlockSpec((TR, POOL_CH), lambda i: (i, 0)),
            nxt_spec(POOL_CH),
            pl.BlockSpec((TR, TC), lambda i: (i, 0)),
            pl.BlockSpec((HALO, TC), lambda i: (jnp.maximum(i * nh - 1, 0), 0)),
            pl.BlockSpec((HALO, CONV_CH), lambda i: (0, 0)),
        ],
        out_specs=[pl.BlockSpec((TR, TC), lambda i: (i, 0)), pl.BlockSpec((HALO, CONV_CH), lambda i: (0, 0))],
        out_shape=[jax.ShapeDtypeStruct((R, TC), F32), jax.ShapeDtypeStruct((HALO, CONV_CH), F32)],
        scratch_shapes=[
            pltpu.VMEM((TR + HALO, CONV_CH), F32),
            pltpu.VMEM((TR + HALO, CONV_CH), F32),
            pltpu.VMEM((TR + HALO, POOL_CH), F32),
        ],
        compiler_params=_cp(("arbitrary",)),
        name=name,
    )(dc, dc, dpooled, dpooled, proj_a, proj_a, conv_w)


def _in_bwd(dfront, dq, dk, dv, dgates, w_a, w_g, h, dh2, gain, name, side=None):
    R = h.shape[0]
    nj = 1 + D_G // TC
    ni = R // TR
    w_a = w_a.astype(_MXU)
    w_f, w_q = w_a[:, :TC], w_a[:, TC : TC + ATT_W]
    w_k, w_v = w_a[:, TC + ATT_W : TC + 2 * ATT_W], w_a[:, TC + 2 * ATT_W :]

    def body(df_ref, dq_ref, dk_ref, dv_ref, dg_ref, wf_ref, wq_ref, wk_ref, wv_ref, wg_ref, h_ref, dh_ref, g_ref,
             o_ref, hn_ref, dgain_ref, acc):
        i, j = pl.program_id(0), pl.program_id(1)

        @pl.when(j == 0)
        def _():
            acc[...] = (_dot_t(df_ref[...], wf_ref[...]) + _dot_t(dq_ref[...], wq_ref[...])
                        + _dot_t(dk_ref[...], wk_ref[...]) + _dot_t(dv_ref[...], wv_ref[...]))

        @pl.when(j > 0)
        def _():
            acc[...] += _dot_t(dg_ref[...], wg_ref[...])

        @pl.when(jnp.logical_and(i == 0, j == 0))
        def _():
            dgain_ref[...] = jnp.zeros_like(dgain_ref)

        @pl.when(j == nj - 1)
        def _():
            g = g_ref[...]
            r, n, y = _rms_fwd(h_ref[...], g)
            dx, dgain = _rms_bwd(acc[...], r, n, g)
            dgain_ref[...] += dgain
            rows = _row_ids(TR, i, (TR, D))
            o_ref[...] = jnp.where(rows >= PAD_ROWS, dh_ref[...] + dx, 0.0)
            hn_ref[...] = y.astype(hn_ref.dtype)

    def row(w):
        return pl.BlockSpec((TR, w), lambda i, j: (i, 0))

    def whole(w):
        return pl.BlockSpec((D, w), lambda i, j: (0, 0))

    def gcol(i, j):
        return jnp.maximum(j - 1, 0)

    return _hosted_call(
        body, side,
        grid=(ni, nj),
        in_specs=[
            row(TC), row(ATT_W), row(ATT_W), row(ATT_W),
            pl.BlockSpec((TR, TC), lambda i, j: (i, gcol(i, j))),
            whole(TC), whole(ATT_W), whole(ATT_W), whole(ATT_W),
            pl.BlockSpec((D, TC), lambda i, j: (0, gcol(i, j))),
            row(D), row(D),
            pl.BlockSpec((1, D), lambda i, j: (0, 0)),
        ],
        out_specs=[row(D), row(D), pl.BlockSpec((1, D), lambda i, j: (0, 0))],
        out_shape=[jax.ShapeDtypeStruct((R, D), F32), jax.ShapeDtypeStruct((R, D), _MXU), jax.ShapeDtypeStruct((1, D), F32)],
        scratch_shapes=[pltpu.VMEM((TR, D), F32)],
        name=name,
        operands=(dfront, dq, dk, dv, dgates, w_f, w_q, w_k, w_v, w_g.astype(_MXU), h, dh2, gain),
    )


def _pool_blockdiag(w_grp):
    eye = jnp.eye(len(POOL_WINDOWS), dtype=w_grp.dtype)
    return jnp.einsum("gcd,gh->gchd", w_grp, eye).reshape(POOL_CH, D)


def _pool_blockdiag_grad(dw_bd):
    d4 = dw_bd.reshape(len(POOL_WINDOWS), POOL_GC, len(POOL_WINDOWS), D // len(POOL_WINDOWS))
    return jnp.stack([d4[g, :, g, :] for g in range(len(POOL_WINDOWS))])


def _local_step(x, target, p, riders=None):
    row = lambda a: a.reshape(1, -1)
    riders = riders or {}
    g = {k: [None] * N_LAYERS for k in ("norm1", "w_in", "conv_dw_w", "conv_dw_b", "conv_ln_g", "conv_ln_b", "w_conv_out",
                                        "b_conv_out", "w_pool_grp", "pool_scale", "w_attn_out", "w_o", "norm2", "w_up",
                                        "ffn_dw_w", "ffn_dw_b", "w_down")}

    def with_rider(name, call):
        make, land = riders.get(name, (None, None))
        side = make(g) if make else None
        res = call(side)
        res = list(res) if isinstance(res, (list, tuple)) else [res]
        if side is None:
            return res
        n_own = len(res) - len(side.out_shape)
        land(res[n_own:])
        return res[:n_own]

    h = jnp.concatenate([jnp.zeros((PAD_ROWS, D), F32), p["meta"], x], axis=0)
    saved = []
    for l in range(N_LAYERS):
        w_a, w_g = p["w_in"][l][:, :D_A], p["w_in"][l][:, D_A:]
        conv_w = jnp.concatenate([p["conv_dw_w"][l], jnp.zeros((1, CONV_CH), F32)], axis=0)
        cw3 = p["ffn_dw_w"][l].reshape(FFN_K, 2, D_FF)
        cb2 = p["ffn_dw_b"][l].reshape(2, D_FF)
        mw = (row(p["conv_ln_g"][l]), row(p["conv_ln_b"][l]), p["w_conv_out"][l].astype(_MXU), row(p["b_conv_out"][l]),
              _pool_blockdiag(p["w_pool_grp"][l]).astype(_MXU), row(p["pool_scale"][l]),
              p["w_attn_out"][l].astype(_MXU), p["w_o"][l].astype(_MXU))
        proj_a = _rms_matmul(h, row(p["norm1"][l]), w_a, f"proj_a{l}")
        (gates,) = with_rider(f"proj_g{l}", lambda side: _rms_matmul(h, row(p["norm1"][l]), w_g, f"proj_g{l}", side=side))
        qkv = _qkv_cast(proj_a, f"qkv_cast{l}")
        c, pooled = _branch_pre(proj_a, conv_w, row(p["conv_dw_b"][l]), f"branch_pre{l}")
        att = _attn_fwd(qkv, f"attn_fwd{l}")
        h2 = _mix_fwd(h, gates, c, pooled, att, mw, f"mix_fwd{l}")
        (u3,) = with_rider(f"ffn_up{l}", lambda side: _rms_matmul(h2, row(p["norm2"][l]), p["w_up"][l], f"ffn_up{l}",
                                                                split_out=True, side=side))
        h3 = _ffn_fwd(u3, h2, cw3, cb2, p["w_down"][l], f"ffn_fwd{l}")
        saved.append(dict(h=h, w_a=w_a, w_g=w_g, conv_w=conv_w, cw3=cw3, cb2=cb2, mw=mw, proj_a=proj_a, gates=gates,
                          qkv=qkv, c=c, pooled=pooled, att=att, h2=h2, u3=u3))
        h = h3

    dh, loss, d_final = _loss_bwd(h, target, row(p["final_norm"]), "loss_bwd")

    for l in reversed(range(N_LAYERS)):
        s = saved[l]
        dc3, g["w_down"][l], dwb = with_rider(f"ffn_bwd1_{l}", lambda side: _ffn_bwd1(
            dh, s["u3"], s["cw3"], s["cb2"], p["w_down"][l], f"ffn_bwd1_{l}", side=side))
        g["ffn_dw_w"][l] = jnp.transpose(dwb[:, :FFN_K, :], (1, 0, 2)).reshape(FFN_K, 2 * D_FF)
        g["ffn_dw_b"][l] = dwb[:, FFN_K, :].reshape(2 * D_FF)
        du3, dh2, xn2, dg2 = _ffn_bwd2(dc3, s["cw3"], p["w_up"][l], s["h2"], dh, row(p["norm2"][l]), f"ffn_bwd2_{l}")
        g["norm2"][l] = dg2[0]
        g["w_up"][l] = _tdot_acc(xn2, du3, f"dw_up{l}")
        (dgates, dc, dpooled, datt, mixed, s_act, dya, dybs, dyc, acc_d, acc_c) = _mix_bwd(
            dh2, s["gates"], s["c"], s["pooled"], s["att"], s["mw"], f"mix_bwd{l}")
        g["w_o"][l] = _tdot_acc(mixed, dh2, f"dw_o{l}")
        g["w_conv_out"][l] = _tdot_acc(s_act, dya, f"dw_conv_out{l}")
        g["w_pool_grp"][l] = _pool_blockdiag_grad(_tdot_acc(s["pooled"], dybs, f"dw_pool{l}"))
        g["w_attn_out"][l] = _tdot_acc(s["att"], dyc, f"dw_attn_out{l}")
        g["b_conv_out"][l] = acc_d[0]
        g["pool_scale"][l] = acc_d[1]
        g["conv_ln_g"][l] = acc_c[0]
        g["conv_ln_b"][l] = acc_c[1]
        g["conv_dw_b"][l] = acc_c[2]
        dq, dk, dv = _attn_bwd(s["qkv"], s["att"], datt, f"attn_bwd{l}")
        dfront, dcw = _branch_bwd(dc, dpooled, s["proj_a"], s["conv_w"], f"branch_bwd{l}")
        g["conv_dw_w"][l] = dcw[:CONV_K]
        dh, hn, dg1 = with_rider(f"in_bwd{l}", lambda side: _in_bwd(
            dfront, dq, dk, dv, dgates, s["w_a"], s["w_g"], s["h"], dh2, row(p["norm1"][l]), f"in_bwd{l}", side=side))
        g["norm1"][l] = dg1[0]
        g["w_in"][l] = jnp.concatenate(
            [_tdot_acc(hn, part, f"dw_in{l}_{k}") for k, part in enumerate((dfront, dq, dk, dv, dgates))], axis=1)

    grads = dict(g, final_norm=d_final[0], meta=dh[PAD_ROWS:BLK])
    return loss, dh[BLK:], grads


def _axes():
    return lax.axis_index("x"), lax.axis_index("y"), lax.axis_index("c")


def _transfer(side, name):
    def body(*refs):
        s_in, s_out = len(side.arrays), len(side.out_shape)
        ins, outs, scratch = refs[:s_in], refs[s_in : s_in + s_out], refs[s_in + s_out :]
        side.start(ins, outs, scratch)
        side.finish(ins, outs, scratch)

    return _pcall(
        body,
        in_specs=[ANY] * len(side.arrays),
        out_specs=[ANY] * len(side.out_shape),
        out_shape=list(side.out_shape),
        scratch_shapes=list(side.scratch),
        compiler_params=pltpu.CompilerParams(has_side_effects=True),
        name=name,
    )(*side.arrays)


def _gather_side(blobs):
    nt = len(blobs)
    flips = ((1, 0), (0, 1), (1, 1))

    def local(srcs, dsts, sems):
        x, y, _ = _axes()
        return [pltpu.make_async_copy(srcs[t], dsts[t].at[2 * x + y], sems[2].at[t]) for t in range(nt)]

    def remote(srcs, dsts, sems, arriving):
        send_sems, recv_sems, _ = sems
        x, y, c = _axes()
        out = []
        for t in range(nt):
            for f, (fx, fy) in enumerate(flips):
                px = 1 - x if fx else x
                py = 1 - y if fy else y
                out.append(pltpu.make_async_remote_copy(
                    srcs[t], dsts[t].at[2 * px + py if arriving else 2 * x + y], send_sems.at[t, f], recv_sems.at[t, f],
                    device_id=(px, py, c), device_id_type=MESH))
        return out

    def start(srcs, dsts, sems):
        for cp in local(srcs, dsts, sems) + remote(srcs, dsts, sems, arriving=False):
            cp.start()

    def finish(srcs, dsts, sems):
        for cp in remote(srcs, dsts, sems, arriving=False):
            cp.wait_send()
        for cp in remote(srcs, dsts, sems, arriving=True):
            cp.wait_recv()
        for cp in local(srcs, dsts, sems):
            cp.wait()

    return _Side(
        arrays=tuple(blobs),
        out_shape=tuple(jax.ShapeDtypeStruct((4,) + b.shape, b.dtype) for b in blobs),
        scratch=(pltpu.SemaphoreType.DMA((nt, 3)), pltpu.SemaphoreType.DMA((nt, 3)), pltpu.SemaphoreType.DMA((nt,))),
        start=start,
        finish=finish,
    )


def _exchange_side(contribs):
    nt = len(contribs)

    def plan(srcs, dsts, sems):
        send_sems, recv_sems, local_sems = sems
        x, y, c = _axes()
        sibling = (x, y, 1 - c)
        chips = [(1 - x, y), (x, 1 - y), (1 - x, 1 - y)]

        def slot(t, px, py, pc):
            return dsts[t].at[4 * px + 2 * py + pc]

        def copy(t, k, source, target_slot, to):
            return pltpu.make_async_remote_copy(source, target_slot, send_sems.at[t, k], recv_sems.at[t, k],
                                                device_id=to, device_id_type=MESH)

        def mine(t):
            return srcs[t].at[2 * x + y]

        def own():
            return [pltpu.make_async_copy(mine(t), slot(t, x, y, c), local_sems.at[t]) for t in range(nt)]

        def first():
            out = []
            for t in range(nt):
                out.append(copy(t, 0, mine(t), slot(t, x, y, c), sibling))
                out += [copy(t, 1 + j, srcs[t].at[2 * px + py], slot(t, x, y, c), (px, py, c))
                        for j, (px, py) in enumerate(chips)]
            return out

        def relay():
            return [(copy(t, 1 + j, mine(t), slot(t, px, py, c), (px, py, c)),
                     copy(t, 4 + j, slot(t, px, py, c), slot(t, px, py, c), sibling))
                    for j, (px, py) in enumerate(chips) for t in range(nt)]

        def from_sibling():
            out = [copy(t, 0, mine(t), slot(t, x, y, 1 - c), sibling) for t in range(nt)]
            return out + [copy(t, 4 + j, mine(t), slot(t, px, py, 1 - c), sibling)
                          for t in range(nt) for j, (px, py) in enumerate(chips)]

        return own, first, relay, from_sibling

    def start(srcs, dsts, sems):
        own, first, _, _ = plan(srcs, dsts, sems)
        for cp in own() + first():
            cp.start()

    def finish(srcs, dsts, sems):
        own, first, relay, from_sibling = plan(srcs, dsts, sems)
        passed = []
        for arriving, onward in relay():
            arriving.wait_recv()
            onward.start()
            passed.append(onward)
        for cp in from_sibling():
            cp.wait_recv()
        for cp in first() + passed:
            cp.wait_send()
        for cp in own():
            cp.wait()

    return _Side(
        arrays=tuple(contribs),
        out_shape=tuple(jax.ShapeDtypeStruct((8,) + a.shape[1:], a.dtype) for a in contribs),
        scratch=(pltpu.SemaphoreType.DMA((nt, 7)), pltpu.SemaphoreType.DMA((nt, 7)), pltpu.SemaphoreType.DMA((nt,))),
        start=start,
        finish=finish,
    )


ADAM_BLOCK_BYTES = 768 * 1024


def _adamw(w, m, v, slot_list, name):
    shape = w.shape
    parts = len(slot_list)
    cols = shape[-1]
    rows = _size(shape) // cols // parts
    blk = max(r for r in range(8, rows + 1, 8) if rows % r == 0 and r * cols * 4 <= ADAM_BLOCK_BYTES) \
        if rows % 8 == 0 else rows

    def body(w_ref, m_ref, v_ref, *refs):
        slot_refs, (g_ref, d_ref, nm_ref, nv_ref) = refs[:parts], refs[parts:]
        g = None
        for k, gs_ref in enumerate(slot_refs):
            gk = gs_ref[0]
            for dev in range(1, 8):
                gk = gk + gs_ref[dev]
            g = gk if g is None else jnp.where(pl.program_id(0) == k, gk, g)
        w_, m_, v_ = w_ref[...], m_ref[...], v_ref[...]
        m_new = ADAM_B1 * m_ + (1.0 - ADAM_B1) * g
        v_new = ADAM_B2 * v_ + (1.0 - ADAM_B2) * (g * g)
        m_hat = m_new / (1.0 - ADAM_B1 ** ADAM_STEP)
        v_hat = v_new / (1.0 - ADAM_B2 ** ADAM_STEP)
        g_ref[...] = g
        d_ref[...] = -ADAM_LR * (m_hat / (jnp.sqrt(v_hat) + ADAM_EPS) + ADAM_WD * w_)
        nm_ref[...] = m_new
        nv_ref[...] = v_new

    spec = pl.BlockSpec((None, blk, cols), lambda p, i: (p, i, 0))
    slot_specs = [pl.BlockSpec((8, blk, cols), lambda p, i, k=k: (0, jnp.where(p == k, i, 0), 0)) for k in range(parts)]
    flat = (parts, rows, cols)
    outs = _pcall(
        body,
        grid=(parts, rows // blk),
        in_specs=[spec, spec, spec] + slot_specs,
        out_specs=[spec] * 4,
        out_shape=[jax.ShapeDtypeStruct(flat, F32)] * 4,
        compiler_params=_cp(("arbitrary", "arbitrary")),
        name=name,
    )(w.reshape(flat), m.reshape(flat), v.reshape(flat), *[s.reshape(8, rows, cols) for s in slot_list])
    return [o.reshape(shape) for o in outs]


_PARAMS = (
    ("meta", (N_META, D), 1),
    ("norm1", (N_LAYERS, D), None),
    ("w_in", (N_LAYERS, D, D_A + D_G), 2),
    ("conv_dw_w", (N_LAYERS, CONV_K, CONV_CH), 2),
    ("conv_dw_b", (N_LAYERS, CONV_CH), None),
    ("conv_ln_g", (N_LAYERS, CONV_CH), None),
    ("conv_ln_b", (N_LAYERS, CONV_CH), None),
    ("w_conv_out", (N_LAYERS, CONV_CH, D), 2),
    ("b_conv_out", (N_LAYERS, D), None),
    ("w_pool_grp", (N_LAYERS, len(POOL_WINDOWS), POOL_GC, D // len(POOL_WINDOWS)), 3),
    ("pool_scale", (N_LAYERS, D), None),
    ("w_attn_out", (N_LAYERS, ATT_W, D), 2),
    ("w_o", (N_LAYERS, D, D), 1),
    ("norm2", (N_LAYERS, D), None),
    ("w_up", (N_LAYERS, D, 2 * D_FF), 2),
    ("ffn_dw_w", (N_LAYERS, FFN_K, 2 * D_FF), 2),
    ("ffn_dw_b", (N_LAYERS, 2 * D_FF), None),
    ("w_down", (N_LAYERS, D_FF, D), 1),
    ("final_norm", (D,), None),
)
_MIXER = ("w_in", "w_conv_out", "w_pool_grp", "w_attn_out", "w_o")
_FFN = ("w_up", "w_down")
_BIG = _MIXER + _FFN
_SMALL_SHARDED = ("meta", "conv_dw_w", "ffn_dw_w")
_SHARD_AXIS = {n: ax for n, _, ax in _PARAMS}


def _size(shape):
    n = 1
    for d in shape:
        n *= d
    return n


def _pack(parts, lanes, row_multiple):
    flat = jnp.concatenate([a.reshape(-1) for a in parts])
    rows = -(-flat.shape[0] // lanes)
    rows = -(-rows // row_multiple) * row_multiple
    flat = jnp.pad(flat, (0, rows * lanes - flat.shape[0]))
    return flat.reshape(rows, lanes)


def _unpack(blob, shapes):
    flat = blob.reshape(-1)
    out, off = [], 0
    for s in shapes:
        n = _size(s)
        out.append(flat[off : off + n].reshape(s))
        off += n
    return out


def _shard(a, ax, s):
    n = a.shape[ax] // 4
    return lax.slice_in_dim(a, s * n, (s + 1) * n, axis=ax)


def kernel(x, meta, norm1, w_in, conv_dw_w, conv_dw_b, conv_ln_g, conv_ln_b, w_conv_out, b_conv_out, w_pool_grp, pool_scale, w_attn_out, w_o, norm2, w_up, ffn_dw_w, ffn_dw_b, w_down, final_norm, loss_target, m_meta, m_norm1, m_w_in, m_conv_dw_w, m_conv_dw_b, m_conv_ln_g, m_conv_ln_b, m_w_conv_out, m_b_conv_out, m_w_pool_grp, m_pool_scale, m_w_attn_out, m_w_o, m_norm2, m_w_up, m_ffn_dw_w, m_ffn_dw_b, m_w_down, m_final_norm, v_meta, v_norm1, v_w_in, v_conv_dw_w, v_conv_dw_b, v_conv_ln_g, v_conv_ln_b, v_w_conv_out, v_b_conv_out, v_w_pool_grp, v_pool_scale, v_w_attn_out, v_w_o, v_norm2, v_w_up, v_ffn_dw_w, v_ffn_dw_b, v_w_down, v_final_norm):
    names = [n for n, _, _ in _PARAMS]
    w_loc = dict(zip(names, (meta, norm1, w_in, conv_dw_w, conv_dw_b, conv_ln_g, conv_ln_b, w_conv_out, b_conv_out, w_pool_grp, pool_scale, w_attn_out, w_o, norm2, w_up, ffn_dw_w, ffn_dw_b, w_down, final_norm)))
    m_loc = dict(zip(names, (m_meta, m_norm1, m_w_in, m_conv_dw_w, m_conv_dw_b, m_conv_ln_g, m_conv_ln_b, m_w_conv_out, m_b_conv_out, m_w_pool_grp, m_pool_scale, m_w_attn_out, m_w_o, m_norm2, m_w_up, m_ffn_dw_w, m_ffn_dw_b, m_w_down, m_final_norm)))
    v_loc = dict(zip(names, (v_meta, v_norm1, v_w_in, v_conv_dw_w, v_conv_dw_b, v_conv_ln_g, v_conv_ln_b, v_w_conv_out, v_b_conv_out, v_w_pool_grp, v_pool_scale, v_w_attn_out, v_w_o, v_norm2, v_w_up, v_ffn_dw_w, v_ffn_dw_b, v_w_down, v_final_norm)))

    full = {n: w_loc[n] for n, _, ax in _PARAMS if ax is None}
    full.update({n: [None, None] for n in _BIG})

    def gather(group, l, extra=()):
        return _gather_side([w_loc[n][l].astype(_MXU) for n in group] + list(extra))

    def install(group, l, gathered):
        for n, g4 in zip(group, gathered):
            full[n][l] = jnp.concatenate([g4[s] for s in range(4)], axis=_SHARD_AXIS[n] - 1)

    small = _pack([w_loc[n] for n in _SMALL_SHARDED], BLK, 8)
    *first, small4 = _transfer(gather(_MIXER, 0, [small]), "allgather_weights")
    install(_MIXER, 0, first)
    per_chip = [_unpack(small4[s], [w_loc[n].shape for n in _SMALL_SHARDED]) for s in range(4)]
    for k, n in enumerate(_SMALL_SHARDED):
        full[n] = jnp.concatenate([per_chip[s][k] for s in range(4)], axis=_SHARD_AXIS[n])

    slots = {}

    def exchange(group, l):
        def make(g):
            return _exchange_side([jnp.stack([_shard(g[n][l], _SHARD_AXIS[n] - 1, s) for s in range(4)]) for n in group])

        return make, lambda landed: slots.update({(n, l): a for n, a in zip(group, landed)})

    riders = {
        "proj_g0": (lambda g: gather(_FFN, 0), functools.partial(install, _FFN, 0)),
        "ffn_up0": (lambda g: gather(_BIG, 1), functools.partial(install, _BIG, 1)),
        "ffn_bwd1_0": exchange(_BIG, 1),
        "in_bwd0": exchange(_FFN, 0),
    }
    loss, grad_x, grads = _local_step(x[0], loss_target[0], full, riders)

    small_names = [n for n in names if n not in _BIG]
    small_grads = {n: jnp.stack(grads[n]) if isinstance(grads[n], list) else grads[n] for n in small_names}

    def small_blob(src, s=None):
        return _pack([src[n] if s is None or _SHARD_AXIS[n] is None else _shard(src[n], _SHARD_AXIS[n], s)
                      for n in small_names], BLOB_LANES, 8)

    make, land = exchange(_MIXER, 0)
    last = make(grads)
    last = _exchange_side(list(last.arrays) + [jnp.stack([small_blob(small_grads, s) for s in range(4)])])
    *late, small_slots = _transfer(last, "grad_exchange")
    land(late)
    results = {n: _adamw(w_loc[n], m_loc[n], v_loc[n], [slots[n, 0], slots[n, 1]], f"adamw_{n}") for n in _BIG}
    small_out = _adamw(small_blob(w_loc), small_blob(m_loc), small_blob(v_loc), [small_slots], "adamw_small")
    small_shapes = [w_loc[n].shape for n in small_names]
    for k, blob in enumerate(small_out):
        for n, a in zip(small_names, _unpack(blob, small_shapes)):
            results.setdefault(n, [None] * 4)[k] = a
    loss = lax.psum(loss[0, 0], ("x", "y", "c"))
    outs = [loss, grad_x[None]]
    for k in range(4):
        outs.extend(results[n][k] for n in names)
    return tuple(outs)
```

```python
import functools
from typing import Callable, NamedTuple

import jax
import jax.numpy as jnp
from jax import lax
from jax.experimental import pallas as pl
from jax.experimental.pallas import tpu as pltpu

F32 = jnp.float32
_MXU = jnp.bfloat16

D = 1024
N_META = 16
BLK = 128
CH = 2 * BLK
ATTN_UNROLLS = (8, 2, 1)
PAD_ROWS = BLK - N_META
N_LAYERS = 2
CONV_CH = 256
CONV_K = 31
POOL_CH = 256
POOL_WINDOWS = (2, 4, 8, 16)
POOL_GC = 64
ATT_W = 512
NH = 4
HD = 128
D_A = 2 * CONV_CH + POOL_CH + 3 * ATT_W
D_G = 3 * D
D_FF = 3 * D
FFN_K = 3
EPS = 1e-6
Q_SCALE = HD ** -0.5
LOG2E = 1.4426950408889634
LN2 = 0.6931471805599453

TR = 384
TB = 128
HALO = 32
HALO8 = 8
TC = 768
TF = 512
VMEM_LIMIT = 48 * 1024 * 1024
DW_STEPS = 12

ADAM_LR = 0.001
ADAM_B1 = 0.9
ADAM_B2 = 0.999
ADAM_EPS = 1e-08
ADAM_WD = 0.01
ADAM_STEP = 10

GELU_C0 = 0.7978845608028654
GELU_C1 = 0.044715

MESH = pl.DeviceIdType.MESH
ANY = pl.BlockSpec(memory_space=pl.ANY)

BLOB_LANES = 1024


def _pcall(body, **kw):
    return pl.pallas_call(body, **kw)


def _cp(sem):
    return pltpu.CompilerParams(dimension_semantics=sem, vmem_limit_bytes=VMEM_LIMIT)


def _dot(a, b):
    return jnp.dot(a.astype(_MXU), b.astype(_MXU), preferred_element_type=F32)


def _dot_t(a, b):
    return lax.dot_general(a.astype(_MXU), b.astype(_MXU), (((1,), (1,)), ((), ())), preferred_element_type=F32)


def _tdot(a, b):
    return lax.dot_general(a.astype(_MXU), b.astype(_MXU), (((0,), (0,)), ((), ())), preferred_element_type=F32)


def _sig(x):
    return 1.0 / (1.0 + jnp.exp(-x))


def _rms_fwd(x, g):
    r = lax.rsqrt(jnp.mean(x * x, axis=-1, keepdims=True) + EPS)
    n = x * r
    return r, n, n * g


def _rms_bwd(dy, r, n, g):
    dgain = jnp.sum(dy * n, axis=0, keepdims=True)
    dn = dy * g
    dx = r * (dn - n * jnp.mean(dn * n, axis=-1, keepdims=True))
    return dx, dgain


def _row_ids(tile_rows, i, shape):
    return i * tile_rows + lax.broadcasted_iota(jnp.int32, shape, 0)


class _Side(NamedTuple):
    arrays: tuple
    out_shape: tuple
    scratch: tuple
    start: Callable
    finish: Callable


def _hosted_call(body, side, grid, in_specs, out_specs, out_shape, scratch_shapes, name, operands):
    n_in, n_out, n_scr = len(in_specs), len(out_specs), len(scratch_shapes)
    if side is None:
        return _pcall(body, grid=grid, in_specs=in_specs, out_specs=out_specs, out_shape=out_shape,
                      scratch_shapes=scratch_shapes, compiler_params=_cp(("arbitrary",) * len(grid)), name=name)(*operands)
    s_in, s_out = len(side.arrays), len(side.out_shape)

    def hosted(*refs):
        ins, refs = refs[:n_in], refs[n_in:]
        side_ins, refs = refs[:s_in], refs[s_in:]
        outs, refs = refs[:n_out], refs[n_out:]
        side_outs, refs = refs[:s_out], refs[s_out:]
        scratch, side_scratch = refs[:n_scr], refs[n_scr:]
        ids = [pl.program_id(a) for a in range(len(grid))]
        first = functools.reduce(jnp.logical_and, [p == 0 for p in ids])
        last = functools.reduce(jnp.logical_and, [p == n - 1 for p, n in zip(ids, grid)])

        @pl.when(first)
        def _():
            side.start(side_ins, side_outs, side_scratch)

        body(*ins, *outs, *scratch)

        @pl.when(last)
        def _():
            side.finish(side_ins, side_outs, side_scratch)

    return _pcall(
        hosted,
        grid=grid,
        in_specs=list(in_specs) + [ANY] * s_in,
        out_specs=list(out_specs) + [ANY] * s_out,
        out_shape=list(out_shape) + list(side.out_shape),
        scratch_shapes=list(scratch_shapes) + list(side.scratch),
        compiler_params=_cp(("arbitrary",) * len(grid)),
        name=name,
    )(*operands, *side.arrays)


def _rms_matmul(h, gain, w, name, split_out=False, qkv_out=False, side=None):
    R, N = h.shape[0], w.shape[1]
    nj = N // TC
    tr = R // DW_STEPS
    half = nj // 2

    def body(h_ref, g_ref, w_ref, o_ref, *rest):
        xn_ref = rest[-1]
        j = pl.program_id(1)

        @pl.when(j == 0)
        def _():
            _, _, y = _rms_fwd(h_ref[...], g_ref[...])
            xn_ref[...] = y.astype(xn_ref.dtype)

        out = jnp.dot(xn_ref[...], w_ref[...], preferred_element_type=F32)
        o_ref[...] = out
        if qkv_out:
            @pl.when(j >= 1)
            def _():
                col = lax.broadcasted_iota(jnp.int32, (1, TC), 1) + (j - 1) * TC
                sc = jnp.where(col < ATT_W, Q_SCALE * LOG2E, 1.0).astype(F32)
                rest[0][...] = (out * sc).astype(rest[0].dtype)

    if split_out:
        out_shapes = [jax.ShapeDtypeStruct((2, R, N // 2), F32)]
        out_specs = [pl.BlockSpec((None, tr, TC), lambda i, j: (j // half, i, j % half))]
    else:
        out_shapes = [jax.ShapeDtypeStruct((R, N), F32)]
        out_specs = [pl.BlockSpec((tr, TC), lambda i, j: (i, j))]
    if qkv_out:
        out_shapes.append(jax.ShapeDtypeStruct((R, 3 * ATT_W), _MXU))
        out_specs.append(pl.BlockSpec((tr, TC), lambda i, j: (i, jnp.maximum(j - 1, 0))))
    res = _hosted_call(
        body, side,
        grid=(R // tr, nj),
        in_specs=[
            pl.BlockSpec((tr, D), lambda i, j: (i, 0)),
            pl.BlockSpec((1, D), lambda i, j: (0, 0)),
            pl.BlockSpec((D, TC), lambda i, j: (0, j)),
        ],
        out_specs=out_specs,
        out_shape=out_shapes,
        scratch_shapes=[pltpu.VMEM((tr, D), _MXU)],
        name=name,
        operands=(h, gain, w.astype(_MXU)),
    )
    return res[0] if side is None and not qkv_out else res


def _sum_mat(w):
    rowi = lax.broadcasted_iota(jnp.int32, (w, w), 0)
    coli = lax.broadcasted_iota(jnp.int32, (w, w), 1)
    return (rowi > coli).astype(_MXU)


def _hi_lo_rows(x):
    hi = x.astype(_MXU)
    lo = (x - hi.astype(F32)).astype(_MXU)
    return jnp.concatenate([hi, lo], axis=0)


def _fold_sums(r):
    return r[:BLK] + r[BLK:]


def _attn_logs(z2, valid):
    m = jnp.minimum(z2, 0.0)
    d = m - z2
    t = jnp.log2(1.0 + jnp.exp2(m + d))
    lk = d - t
    if valid is not None:
        lk = jnp.where(valid, lk, 0.0)
    return lk, m - t


def _attn_weights(log_beta, r, valid, s_after):
    log_a = log_beta + r + s_after
    if valid is not None:
        log_a = jnp.where(valid, log_a, -1e30)
    return jnp.exp2(log_a)


def _attn_walk(i, phases, n_free, carry):
    ci = jnp.maximum(i - 1, 0) // 2

    def advance(states, stages, cr):
        for ph in stages:
            for u in range(len(states)):
                states[u], cr = ph(states[u], cr)
        return states, cr

    def at(c):
        return pl.multiple_of(BLK + c * CH, BLK)

    ends = [phases[0](at(ci), CH, True), phases[0](0, BLK, True)]
    ends, _ = advance(ends, phases[1 : 1 + n_free], None)
    _, carry = advance(ends[:1], phases[1 + n_free :], carry)
    left = ci
    for unroll in ATTN_UNROLLS:
        def group(t, cr, unroll=unroll, left=left):
            states = [phases[0](at(left - 1 - unroll * t - u), CH, False) for u in range(unroll)]
            return advance(states, phases[1:], cr)[1]

        carry = lax.fori_loop(0, left // unroll, group, carry)
        left = left % unroll
    return advance(ends[1:], phases[1 + n_free :], carry)[1]


def _attn_mask(i, off, w):
    qpos = i * BLK + lax.broadcasted_iota(jnp.int32, (BLK, w), 0)
    kpos = off + lax.broadcasted_iota(jnp.int32, (BLK, w), 1)
    return jnp.logical_and(kpos < qpos, kpos >= PAD_ROWS)


def _attn_fwd(qkv, name):
    R = qkv.shape[0]
    nb = R // BLK

    def body(q_ref, k_ref, v_ref, o_ref):
        i = pl.program_id(1)
        q = q_ref[...]
        sms = {CH: _sum_mat(CH), BLK: _sum_mat(BLK)}

        def scores(off, w, masked):
            valid = _attn_mask(i, off, w) if masked else None
            return off, w, valid, _dot_t(q, k_ref[pl.ds(off, w), :])

        def sums(state, carry):
            off, w, valid, z = state
            lk, log_beta = _attn_logs(z, valid)
            return (off, w, valid, log_beta, _dot(lk, sms[w]), jnp.sum(lk, axis=1, keepdims=True)), carry

        def output(state, carry):
            off, w, valid, log_beta, r, total = state
            acc, s_after = carry
            a = _attn_weights(log_beta, r, valid, s_after)
            return None, (acc + _dot(a, v_ref[pl.ds(off, w), :]), s_after + total)

        acc, _ = _attn_walk(i, (scores, sums, output), 1, (jnp.zeros((BLK, HD), F32), jnp.zeros((BLK, 1), F32)))
        o_ref[...] = acc

    return _pcall(
        body,
        grid=(NH, nb),
        in_specs=[
            pl.BlockSpec((BLK, HD), lambda h, i: (i, h)),
            pl.BlockSpec((R, HD), lambda h, i: (0, NH + h)),
            pl.BlockSpec((R, HD), lambda h, i: (0, 2 * NH + h)),
        ],
        out_specs=pl.BlockSpec((BLK, HD), lambda h, i: (i, h)),
        out_shape=jax.ShapeDtypeStruct((R, ATT_W), F32),
        compiler_params=_cp(("parallel", "arbitrary")),
        name=name,
    )(qkv, qkv, qkv)


def _pool_consts(i):
    lane = lax.broadcasted_iota(jnp.int32, (1, POOL_CH), 1)
    wsize = jnp.where(lane < POOL_GC, 2.0, jnp.where(lane < 2 * POOL_GC, 4.0, jnp.where(lane < 3 * POOL_GC, 8.0, 16.0)))
    return lane, wsize


def _pool_div(rows, wsize):
    pos1 = (rows - (PAD_ROWS - 1)).astype(F32)
    return jnp.clip(pos1, 1.0, wsize)


def _lane_select(lane, s2, s4, s8, s16):
    return jnp.where(lane < POOL_GC, s2, jnp.where(lane < 2 * POOL_GC, s4, jnp.where(lane < 3 * POOL_GC, s8, s16)))


def _branch_pre(proj_a, conv_w, conv_b, name):
    R = proj_a.shape[0]
    nh = TR // HALO

    def body(t_ref, h_ref, w_ref, b_ref, c_ref, p_ref, gext, pext):
        i = pl.program_id(0)
        t = t_ref[...]
        hl = jnp.where(i == 0, 0.0, h_ref[...])
        gext[pl.ds(0, HALO), :] = hl[:, :CONV_CH] * _sig(hl[:, CONV_CH : 2 * CONV_CH])
        gext[pl.ds(HALO, TR), :] = t[:, :CONV_CH] * _sig(t[:, CONV_CH : 2 * CONV_CH])
        acc = jnp.zeros((TR, CONV_CH), F32) + b_ref[...]
        for k in range(CONV_K):
            acc = acc + w_ref[pl.ds(k, 1), :] * gext[pl.ds(HALO - (CONV_K - 1) + k, TR), :]
        c_ref[...] = acc

        p = t[:, 2 * CONV_CH :]
        pext[pl.ds(0, HALO), :] = hl[:, 2 * CONV_CH :]
        pext[pl.ds(HALO, TR), :] = p

        def back(k):
            return pext[pl.ds(HALO - k, TR), :]

        s2 = p + back(1)
        s4 = s2 + back(2) + back(3)
        s8 = s4 + back(4) + back(5) + back(6) + back(7)
        s16 = s8
        for k in range(8, 16):
            s16 = s16 + back(k)
        lane, wsize = _pool_consts(i)
        div = _pool_div(_row_ids(TR, i, (TR, POOL_CH)), wsize)
        p_ref[...] = (_lane_select(lane, s2, s4, s8, s16) / div - p).astype(p_ref.dtype)

    return _pcall(
        body,
        grid=(R // TR,),
        in_specs=[
            pl.BlockSpec((TR, TC), lambda i: (i, 0)),
            pl.BlockSpec((HALO, TC), lambda i: (jnp.maximum(i * nh - 1, 0), 0)),
            pl.BlockSpec((HALO, CONV_CH), lambda i: (0, 0)),
            pl.BlockSpec((1, CONV_CH), lambda i: (0, 0)),
        ],
        out_specs=[pl.BlockSpec((TR, CONV_CH), lambda i: (i, 0)), pl.BlockSpec((TR, POOL_CH), lambda i: (i, 0))],
        out_shape=[jax.ShapeDtypeStruct((R, CONV_CH), F32), jax.ShapeDtypeStruct((R, POOL_CH), _MXU)],
        scratch_shapes=[pltpu.VMEM((TR + HALO, CONV_CH), F32), pltpu.VMEM((TR + HALO, POOL_CH), F32)],
        compiler_params=_cp(("parallel",)),
        name=name,
    )(proj_a, proj_a, conv_w, conv_b)


def _mix_values(c, pooled, att, gates, ln_g, ln_b, w_co, b_co, w_bd, p_scale, w_ao):
    mu = jnp.mean(c, axis=-1, keepdims=True)
    xc = c - mu
    rstd = lax.rsqrt(jnp.mean(xc * xc, axis=-1, keepdims=True) + EPS)
    nl = xc * rstd
    ln = nl * ln_g + ln_b
    sg = _sig(ln)
    s = ln * sg
    ya = _dot(s, w_co) + b_co
    ybr = _dot(pooled, w_bd)
    yb = ybr * p_scale
    yc = _dot(att, w_ao)
    g = _sig(gates)
    g0, g1, g2 = g[:, :D], g[:, D : 2 * D], g[:, 2 * D :]
    mixed = g0 * ya + g1 * yb + g2 * yc
    return dict(rstd=rstd, nl=nl, ln=ln, sg=sg, s=s, ya=ya, ybr=ybr, yb=yb, yc=yc, g0=g0, g1=g1, g2=g2, mixed=mixed)


_MIX_W_SPECS = [
    pl.BlockSpec((1, CONV_CH), lambda i: (0, 0)),
    pl.BlockSpec((1, CONV_CH), lambda i: (0, 0)),
    pl.BlockSpec((CONV_CH, D), lambda i: (0, 0)),
    pl.BlockSpec((1, D), lambda i: (0, 0)),
    pl.BlockSpec((POOL_CH, D), lambda i: (0, 0)),
    pl.BlockSpec((1, D), lambda i: (0, 0)),
    pl.BlockSpec((ATT_W, D), lambda i: (0, 0)),
    pl.BlockSpec((D, D), lambda i: (0, 0)),
]


def _mix_act_specs(t):
    return [
        pl.BlockSpec((t, D_G), lambda i: (i, 0)),
        pl.BlockSpec((t, CONV_CH), lambda i: (i, 0)),
        pl.BlockSpec((t, POOL_CH), lambda i: (i, 0)),
        pl.BlockSpec((t, ATT_W), lambda i: (i, 0)),
    ]


def _mix_fwd(h, gates, c, pooled, att, mw, name):
    R = h.shape[0]

    def body(h_ref, g_ref, c_ref, p_ref, a_ref, lg, lb, wco, bco, wbd, ps, wao, wo, o_ref):
        v = _mix_values(c_ref[...], p_ref[...], a_ref[...], g_ref[...], lg[...], lb[...], wco[...], bco[...],
                        wbd[...], ps[...], wao[...])
        out = h_ref[...] + _dot(v["mixed"], wo[...])
        rows = _row_ids(TB, pl.program_id(0), (TB, D))
        o_ref[...] = jnp.where(rows >= PAD_ROWS, out, 0.0)

    return _pcall(
        body,
        grid=(R // TB,),
        in_specs=[pl.BlockSpec((TB, D), lambda i: (i, 0))] + _mix_act_specs(TB) + _MIX_W_SPECS,
        out_specs=pl.BlockSpec((TB, D), lambda i: (i, 0)),
        out_shape=jax.ShapeDtypeStruct((R, D), F32),
        compiler_params=_cp(("parallel",)),
        name=name,
    )(h, gates, c, pooled, att, *mw)


def _ffn_conv(ut, uh, cw_ref, cb_ref, ext):
    ext[:, pl.ds(0, HALO8), :] = uh
    ext[:, pl.ds(HALO8, TR), :] = ut
    um1 = ext[:, pl.ds(HALO8 - 1, TR), :]
    um2 = ext[:, pl.ds(HALO8 - 2, TR), :]
    cw = cw_ref[...]
    conv = cw[0][:, None, :] * um2 + cw[1][:, None, :] * um1 + cw[2][:, None, :] * ut + cb_ref[...][:, None, :]
    return conv, um1, um2


def _gelu_parts(x):
    th = jnp.tanh(GELU_C0 * (x + GELU_C1 * x * x * x))
    return th, 0.5 * x * (1.0 + th)


def _ffn_in_specs(nrow8, order):
    n8 = TR // HALO8
    return [
        pl.BlockSpec((2, TR, TF), lambda *g: (0, order(*g)[0], order(*g)[1])),
        pl.BlockSpec((2, HALO8, TF), lambda *g: (0, jnp.maximum(order(*g)[0] * n8 - 1, 0), order(*g)[1])),
        pl.BlockSpec((FFN_K, 2, TF), lambda *g: (0, 0, order(*g)[1])),
        pl.BlockSpec((2, TF), lambda *g: (0, order(*g)[1])),
    ]


def _ffn_fwd(u3, h2, cw, cb, w_down, name):
    R = h2.shape[0]
    nj = D_FF // TF

    def body(u_ref, uh_ref, cw_ref, cb_ref, wd_ref, h_ref, o_ref, ext, acc):
        i, j = pl.program_id(0), pl.program_id(1)
        uh = jnp.where(i == 0, 0.0, uh_ref[...])
        conv, _, _ = _ffn_conv(u_ref[...], uh, cw_ref, cb_ref, ext)
        _, a = _gelu_parts(conv[0])
        part = _dot(a * conv[1], wd_ref[...])

        @pl.when(j == 0)
        def _():
            acc[...] = part

        @pl.when(j > 0)
        def _():
            acc[...] += part

        @pl.when(j == nj - 1)
        def _():
            rows = _row_ids(TR, i, (TR, D))
            o_ref[...] = jnp.where(rows >= PAD_ROWS, h_ref[...] + acc[...], 0.0)

    return _pcall(
        body,
        grid=(R // TR, nj),
        in_specs=_ffn_in_specs(R // HALO8, lambda i, j: (i, j))
        + [pl.BlockSpec((TF, D), lambda i, j: (j, 0)), pl.BlockSpec((TR, D), lambda i, j: (i, 0))],
        out_specs=pl.BlockSpec((TR, D), lambda i, j: (i, 0)),
        out_shape=jax.ShapeDtypeStruct((R, D), F32),
        scratch_shapes=[pltpu.VMEM((2, TR + HALO8, TF), F32), pltpu.VMEM((TR, D), F32)],
        compiler_params=_cp(("parallel", "arbitrary")),
        name=name,
    )(u3, u3, cw, cb, w_down.astype(_MXU), h2)


def _loss_bwd(h, target, gain, name):
    R = h.shape[0]

    def body(h_ref, t_ref, g_ref, dh_ref, loss_ref, dg_ref):
        i = pl.program_id(0)

        @pl.when(i == 0)
        def _():
            loss_ref[...] = jnp.zeros_like(loss_ref)
            dg_ref[...] = jnp.zeros_like(dg_ref)
            dh_ref[...] = jnp.zeros_like(dh_ref)

        @pl.when(i > 0)
        def _():
            g = g_ref[...]
            r, n, y = _rms_fwd(h_ref[...], g)
            e = y - t_ref[...]
            loss_ref[...] += (0.5 / D) * jnp.sum(jnp.sum(e * e, axis=1, keepdims=True), axis=0, keepdims=True)
            dx, dgain = _rms_bwd(e * (1.0 / D), r, n, g)
            dg_ref[...] += dgain
            dh_ref[...] = dx

    return _pcall(
        body,
        grid=(R // BLK,),
        in_specs=[
            pl.BlockSpec((BLK, D), lambda i: (i, 0)),
            pl.BlockSpec((BLK, D), lambda i: (jnp.maximum(i - 1, 0), 0)),
            pl.BlockSpec((1, D), lambda i: (0, 0)),
        ],
        out_specs=[
            pl.BlockSpec((BLK, D), lambda i: (i, 0)),
            pl.BlockSpec((1, 1), lambda i: (0, 0)),
            pl.BlockSpec((1, D), lambda i: (0, 0)),
        ],
        out_shape=[
            jax.ShapeDtypeStruct((R, D), F32),
            jax.ShapeDtypeStruct((1, 1), F32),
            jax.ShapeDtypeStruct((1, D), F32),
        ],
        compiler_params=_cp(("arbitrary",)),
        name=name,
    )(h, target, gain)


DW_BLOCK_BYTES = 6 * 1024 * 1024


def _tdot_acc(a, b, name):
    R, M = a.shape
    split = b.ndim == 3
    width = b.shape[-1]
    N = 2 * width if split else width
    tm = min(M, 512)
    tk = R // DW_STEPS
    tn = max(t for t in range(BLK, width + 1, BLK) if width % t == 0 and tk * t * b.dtype.itemsize <= DW_BLOCK_BYTES)
    per = width // tn

    def body(a_ref, b_ref, o_ref):
        part = _tdot(a_ref[...], b_ref[...])

        @pl.when(pl.program_id(2) == 0)
        def _():
            o_ref[...] = part

        @pl.when(pl.program_id(2) > 0)
        def _():
            o_ref[...] += part

    if split:
        b_spec = pl.BlockSpec((None, tk, tn), lambda m, n, k: (n // per, k, n % per))
    else:
        b_spec = pl.BlockSpec((tk, tn), lambda m, n, k: (k, n))
    return _pcall(
        body,
        grid=(M // tm, N // tn, R // tk),
        in_specs=[pl.BlockSpec((tk, tm), lambda m, n, k: (k, m)), b_spec],
        out_specs=pl.BlockSpec((tm, tn), lambda m, n, k: (m, n)),
        out_shape=jax.ShapeDtypeStruct((M, N), F32),
        compiler_params=_cp(("parallel", "parallel", "arbitrary")),
        name=name,
    )(a, b)


def _ffn_bwd1(dh3, u3, cw, cb, w_down, name, side=None):
    R = dh3.shape[0]
    nj = D_FF // TF

    def body(u_ref, uh_ref, cw_ref, cb_ref, wd_ref, dh_ref, dc_ref, dwd_ref, dwb_ref, ext):
        j, i = pl.program_id(0), pl.program_id(1)
        ut = u_ref[...]
        uh = jnp.where(i == 0, 0.0, uh_ref[...])
        conv, um1, um2 = _ffn_conv(ut, uh, cw_ref, cb_ref, ext)
        gt, val = conv[0], conv[1]
        th, a = _gelu_parts(gt)
        dh = dh_ref[...].astype(_MXU)
        dact = _dot_t(dh, wd_ref[...])
        dgelu = 0.5 * (1.0 + th) + 0.5 * gt * (1.0 - th * th) * (GELU_C0 * (1.0 + 3.0 * GELU_C1 * gt * gt))
        dgt = dact * val * dgelu
        dval = dact * a
        dc_ref[0] = dgt
        dc_ref[1] = dval

        @pl.when(i == 0)
        def _():
            dwb_ref[...] = jnp.zeros_like(dwb_ref)
            dwd_ref[...] = jnp.zeros_like(dwd_ref)

        dwd_ref[...] += _tdot(a * val, dh)

        for half, dcv in ((0, dgt), (1, dval)):
            dwb_ref[half, pl.ds(0, 1), :] += jnp.sum(um2[half] * dcv, axis=0, keepdims=True)
            dwb_ref[half, pl.ds(1, 1), :] += jnp.sum(um1[half] * dcv, axis=0, keepdims=True)
            dwb_ref[half, pl.ds(2, 1), :] += jnp.sum(ut[half] * dcv, axis=0, keepdims=True)
            dwb_ref[half, pl.ds(3, 1), :] += jnp.sum(dcv, axis=0, keepdims=True)

    return _hosted_call(
        body, side,
        grid=(nj, R // TR),
        in_specs=_ffn_in_specs(R // HALO8, lambda j, i: (i, j))
        + [pl.BlockSpec((TF, D), lambda j, i: (j, 0)), pl.BlockSpec((TR, D), lambda j, i: (i, 0))],
        out_specs=[
            pl.BlockSpec((2, TR, TF), lambda j, i: (0, i, j)),
            pl.BlockSpec((TF, D), lambda j, i: (j, 0)),
            pl.BlockSpec((2, 8, TF), lambda j, i: (0, 0, j)),
        ],
        out_shape=[
            jax.ShapeDtypeStruct((2, R, D_FF), F32),
            jax.ShapeDtypeStruct((D_FF, D), F32),
            jax.ShapeDtypeStruct((2, 8, D_FF), F32),
        ],
        scratch_shapes=[pltpu.VMEM((2, TR + HALO8, TF), F32)],
        name=name,
        operands=(u3, u3, cw, cb, w_down.astype(_MXU), dh3),
    )


def _ffn_bwd2(dc3, cw, w_up, h2, dh3, gain, name):
    R = h2.shape[0]
    nj = D_FF // TF
    n8 = TR // HALO8
    last8 = R // HALO8 - 1
    ni = R // TR

    def body(dc_ref, dn_ref, cw_ref, wg_ref, wv_ref, h_ref, dh_ref, g_ref, du_ref, o_ref, xn_ref, dg_ref, ext, acc):
        i, j = pl.program_id(0), pl.program_id(1)
        dc = dc_ref[...]
        ext[:, pl.ds(0, TR), :] = dc
        ext[:, pl.ds(TR, HALO8), :] = jnp.where(i == ni - 1, 0.0, dn_ref[...])
        cw = cw_ref[...]
        du = (cw[2][:, None, :] * dc + cw[1][:, None, :] * ext[:, pl.ds(1, TR), :]
              + cw[0][:, None, :] * ext[:, pl.ds(2, TR), :])
        du_ref[...] = du.astype(du_ref.dtype)
        part = _dot_t(du[0], wg_ref[...]) + _dot_t(du[1], wv_ref[...])

        @pl.when(j == 0)
        def _():
            acc[...] = part

        @pl.when(j > 0)
        def _():
            acc[...] += part

        @pl.when(jnp.logical_and(i == 0, j == 0))
        def _():
            dg_ref[...] = jnp.zeros_like(dg_ref)

        @pl.when(j == nj - 1)
        def _():
            g = g_ref[...]
            r, n, y = _rms_fwd(h_ref[...], g)
            dx, dgain = _rms_bwd(acc[...], r, n, g)
            dg_ref[...] += dgain
            rows = _row_ids(TR, i, (TR, D))
            o_ref[...] = jnp.where(rows >= PAD_ROWS, dh_ref[...] + dx, 0.0)
            xn_ref[...] = y.astype(xn_ref.dtype)

    w_up = w_up.astype(_MXU)
    return _pcall(
        body,
        grid=(ni, nj),
        in_specs=[
            pl.BlockSpec((2, TR, TF), lambda i, j: (0, i, j)),
            pl.BlockSpec((2, HALO8, TF), lambda i, j: (0, jnp.minimum((i + 1) * n8, last8), j)),
            pl.BlockSpec((FFN_K, 2, TF), lambda i, j: (0, 0, j)),
            pl.BlockSpec((D, TF), lambda i, j: (0, j)),
            pl.BlockSpec((D, TF), lambda i, j: (0, nj + j)),
            pl.BlockSpec((TR, D), lambda i, j: (i, 0)),
            pl.BlockSpec((TR, D), lambda i, j: (i, 0)),
            pl.BlockSpec((1, D), lambda i, j: (0, 0)),
        ],
        out_specs=[
            pl.BlockSpec((2, TR, TF), lambda i, j: (0, i, j)),
            pl.BlockSpec((TR, D), lambda i, j: (i, 0)),
            pl.BlockSpec((TR, D), lambda i, j: (i, 0)),
            pl.BlockSpec((1, D), lambda i, j: (0, 0)),
        ],
        out_shape=[
            jax.ShapeDtypeStruct((2, R, D_FF), _MXU),
            jax.ShapeDtypeStruct((R, D), F32),
            jax.ShapeDtypeStruct((R, D), _MXU),
            jax.ShapeDtypeStruct((1, D), F32),
        ],
        scratch_shapes=[pltpu.VMEM((2, TR + HALO8, TF), F32), pltpu.VMEM((TR, D), F32)],
        compiler_params=_cp(("arbitrary", "arbitrary")),
        name=name,
    )(dc3, dc3, cw, w_up, w_up, h2, dh3, gain)


def _mix_bwd(dh2, gates, c, pooled, att, mw, name):
    R = dh2.shape[0]

    def body(dh_ref, g_ref, c_ref, p_ref, a_ref, lg, lb, wco, bco, wbd, ps, wao, wo,
             dg_ref, dc_ref, dp_ref, da_ref, mx_ref, s_ref, dya_ref, dyb_ref, dyc_ref, accd_ref, accc_ref):
        v = _mix_values(c_ref[...], p_ref[...], a_ref[...], g_ref[...], lg[...], lb[...], wco[...], bco[...],
                        wbd[...], ps[...], wao[...])
        dmix = _dot_t(dh_ref[...], wo[...])
        for k, (gk, yk) in enumerate(((v["g0"], v["ya"]), (v["g1"], v["yb"]), (v["g2"], v["yc"]))):
            dg_ref[:, k * D : (k + 1) * D] = dmix * yk * gk * (1.0 - gk)
        dya = dmix * v["g0"]
        dyb = dmix * v["g1"]
        dyc = dmix * v["g2"]
        ds = _dot_t(dya, wco[...])
        ln, sg, nl = v["ln"], v["sg"], v["nl"]
        dln = ds * (sg * (1.0 + ln * (1.0 - sg)))
        dn = dln * lg[...]
        dc = v["rstd"] * (dn - jnp.mean(dn, axis=-1, keepdims=True) - nl * jnp.mean(dn * nl, axis=-1, keepdims=True))
        dybs = dyb * ps[...]
        dc_ref[...] = dc
        dp_ref[...] = _dot_t(dybs, wbd[...])
        da_ref[...] = _dot_t(dyc, wao[...])
        mx_ref[...] = v["mixed"].astype(mx_ref.dtype)
        s_ref[...] = v["s"].astype(s_ref.dtype)
        dya_ref[...] = dya.astype(dya_ref.dtype)
        dyb_ref[...] = dybs.astype(dyb_ref.dtype)
        dyc_ref[...] = dyc.astype(dyc_ref.dtype)

        @pl.when(pl.program_id(0) == 0)
        def _():
            accd_ref[...] = jnp.zeros_like(accd_ref)
            accc_ref[...] = jnp.zeros_like(accc_ref)

        accd_ref[pl.ds(0, 1), :] += jnp.sum(dya, axis=0, keepdims=True)
        accd_ref[pl.ds(1, 1), :] += jnp.sum(dyb * v["ybr"], axis=0, keepdims=True)
        accc_ref[pl.ds(0, 1), :] += jnp.sum(dln * nl, axis=0, keepdims=True)
        accc_ref[pl.ds(1, 1), :] += jnp.sum(dln, axis=0, keepdims=True)
        accc_ref[pl.ds(2, 1), :] += jnp.sum(dc, axis=0, keepdims=True)

    def row(w):
        return pl.BlockSpec((TB, w), lambda i: (i, 0))

    return _pcall(
        body,
        grid=(R // TB,),
        in_specs=[row(D)] + _mix_act_specs(TB) + _MIX_W_SPECS,
        out_specs=[row(D_G), row(CONV_CH), row(POOL_CH), row(ATT_W), row(D), row(CONV_CH), row(D), row(D), row(D),
                   pl.BlockSpec((8, D), lambda i: (0, 0)), pl.BlockSpec((8, CONV_CH), lambda i: (0, 0))],
        out_shape=[
            jax.ShapeDtypeStruct((R, D_G), F32),
            jax.ShapeDtypeStruct((R, CONV_CH), F32),
            jax.ShapeDtypeStruct((R, POOL_CH), F32),
            jax.ShapeDtypeStruct((R, ATT_W), F32),
            jax.ShapeDtypeStruct((R, D), _MXU),
            jax.ShapeDtypeStruct((R, CONV_CH), _MXU),
            jax.ShapeDtypeStruct((R, D), _MXU),
            jax.ShapeDtypeStruct((R, D), _MXU),
            jax.ShapeDtypeStruct((R, D), _MXU),
            jax.ShapeDtypeStruct((8, D), F32),
            jax.ShapeDtypeStruct((8, CONV_CH), F32),
        ],
        compiler_params=_cp(("arbitrary",)),
        name=name,
    )(dh2, gates, c, pooled, att, *mw)


def _attn_bwd(qkv, att, datt, name):
    R = qkv.shape[0]
    nb = R // BLK

    def body(q_ref, k_ref, v_ref, o_ref, do_ref, dq_ref, dk_ref, dv_ref):
        i = pl.program_id(1)

        @pl.when(i == 0)
        def _():
            dk_ref[...] = jnp.zeros_like(dk_ref)
            dv_ref[...] = jnp.zeros_like(dv_ref)

        q = q_ref[...]
        dob = do_ref[...].astype(_MXU)
        dof = dob.astype(F32)
        sms = {CH: _sum_mat(CH), BLK: _sum_mat(BLK)}
        e_all = jnp.sum(dof * o_ref[...], axis=1, keepdims=True)

        def scores(off, w, masked):
            valid = _attn_mask(i, off, w) if masked else None
            z = _dot_t(q, k_ref[pl.ds(off, w), :])
            return off, w, valid, z, _dot_t(dob, v_ref[pl.ds(off, w), :])

        def sums(state, carry):
            off, w, valid, z, da = state
            lk, log_beta = _attn_logs(z, valid)
            return (off, w, valid, log_beta, _dot(lk, sms[w]), jnp.sum(lk, axis=1, keepdims=True), da), carry

        def weights(state, carry):
            off, w, valid, log_beta, r, total, da = state
            dq, s_after, e_done = carry
            ab = _attn_weights(log_beta, r, valid, s_after).astype(_MXU)
            e = ab.astype(F32) * da
            re = jnp.dot(_hi_lo_rows(e), sms[w], preferred_element_type=F32)
            return (off, w, valid, jnp.exp2(log_beta), ab, e, re), (dq, s_after + total, e_done)

        def grads(state, carry):
            off, w, valid, beta, ab, e, re = state
            dq, s_after, e_done = carry
            dz = e - beta * ((e_all - e_done) - _fold_sums(re))
            if valid is not None:
                dz = jnp.where(valid, dz, 0.0)
            dzb = (dz * LN2).astype(_MXU)
            dk_ref[pl.ds(off, w), :] += _tdot(dzb, q)
            dv_ref[pl.ds(off, w), :] += _tdot(ab, dob)
            dq = dq + jnp.dot(dzb, k_ref[pl.ds(off, w), :], preferred_element_type=F32)
            return None, (dq, s_after, e_done + jnp.sum(e, axis=1, keepdims=True))

        zero = jnp.zeros((BLK, 1), F32)
        dq, _, _ = _attn_walk(i, (scores, sums, weights, grads), 1, (jnp.zeros((BLK, HD), F32), zero, zero))
        dq_ref[...] = dq * (Q_SCALE * LOG2E)

    blk = pl.BlockSpec((BLK, HD), lambda h, i: (i, h))
    col = pl.BlockSpec((R, HD), lambda h, i: (0, h))
    return _pcall(
        body,
        grid=(NH, nb),
        in_specs=[
            blk,
            pl.BlockSpec((R, HD), lambda h, i: (0, NH + h)),
            pl.BlockSpec((R, HD), lambda h, i: (0, 2 * NH + h)),
            blk,
            blk,
        ],
        out_specs=[blk, col, col],
        out_shape=[jax.ShapeDtypeStruct((R, ATT_W), F32)] * 3,
        compiler_params=pltpu.CompilerParams(dimension_semantics=("arbitrary", "arbitrary"),
                                             vmem_limit_bytes=56 * 1024 * 1024),
        name=name,
    )(qkv, qkv, qkv, att, datt)


def _branch_bwd(dc, dpooled, proj_a, conv_w, name):
    R = dc.shape[0]
    nh = TR // HALO
    last = R // HALO - 1
    ni = R // TR

    def body(dc_ref, dcn_ref, dp_ref, dpn_ref, t_ref, h_ref, w_ref, o_ref, dcw_ref, gext, dcext, eext):
        i = pl.program_id(0)
        is_last = i == ni - 1
        t = t_ref[...]
        hl = jnp.where(i == 0, 0.0, h_ref[...])
        a = t[:, :CONV_CH]
        sg = _sig(t[:, CONV_CH : 2 * CONV_CH])
        gext[pl.ds(0, HALO), :] = hl[:, :CONV_CH] * _sig(hl[:, CONV_CH : 2 * CONV_CH])
        gext[pl.ds(HALO, TR), :] = a * sg
        dct = dc_ref[...]
        dcext[pl.ds(0, TR), :] = dct
        dcext[pl.ds(TR, HALO), :] = jnp.where(is_last, 0.0, dcn_ref[...])

        @pl.when(i == 0)
        def _():
            dcw_ref[...] = jnp.zeros_like(dcw_ref)

        dglu = jnp.zeros((TR, CONV_CH), F32)
        for k in range(CONV_K):
            dglu = dglu + w_ref[pl.ds(k, 1), :] * dcext[pl.ds(CONV_K - 1 - k, TR), :]
            dcw_ref[pl.ds(k, 1), :] += jnp.sum(gext[pl.ds(HALO - (CONV_K - 1) + k, TR), :] * dct, axis=0, keepdims=True)
        o_ref[:, :CONV_CH] = dglu * sg
        o_ref[:, CONV_CH : 2 * CONV_CH] = dglu * a * sg * (1.0 - sg)

        lane, wsize = _pool_consts(i)
        dpt = dp_ref[...]
        eext[pl.ds(0, TR), :] = dpt / _pool_div(_row_ids(TR, i, (TR, POOL_CH)), wsize)
        nxt = dpn_ref[...] / _pool_div(_row_ids(TR, i + 1, (HALO, POOL_CH)), wsize)
        eext[pl.ds(TR, HALO), :] = jnp.where(is_last, 0.0, nxt)

        def fwd(k):
            return eext[pl.ds(k, TR), :]

        s2 = fwd(0) + fwd(1)
        s4 = s2 + fwd(2) + fwd(3)
        s8 = s4 + fwd(4) + fwd(5) + fwd(6) + fwd(7)
        s16 = s8
        for k in range(8, 16):
            s16 = s16 + fwd(k)
        o_ref[:, 2 * CONV_CH :] = _lane_select(lane, s2, s4, s8, s16) - dpt

    def nxt_spec(w):
        return pl.BlockSpec((HALO, w), lambda i: (jnp.minimum((i + 1) * nh, last), 0))

    return _pcall(
        body,
        grid=(ni,),
        in_specs=[
            pl.BlockSpec((TR, CONV_CH), lambda i: (i, 0)),
            nxt_spec(CONV_CH),
            pl.BlockSpec((TR, POOL_CH), lambda i: (i, 0)),
            nxt_spec(POOL_CH),
            pl.BlockSpec((TR, TC), lambda i: (i, 0)),
            pl.BlockSpec((HALO, TC), lambda i: (jnp.maximum(i * nh - 1, 0), 0)),
            pl.BlockSpec((HALO, CONV_CH), lambda i: (0, 0)),
        ],
        out_specs=[pl.BlockSpec((TR, TC), lambda i: (i, 0)), pl.BlockSpec((HALO, CONV_CH), lambda i: (0, 0))],
        out_shape=[jax.ShapeDtypeStruct((R, TC), F32), jax.ShapeDtypeStruct((HALO, CONV_CH), F32)],
        scratch_shapes=[
            pltpu.VMEM((TR + HALO, CONV_CH), F32),
            pltpu.VMEM((TR + HALO, CONV_CH), F32),
            pltpu.VMEM((TR + HALO, POOL_CH), F32),
        ],
        compiler_params=_cp(("arbitrary",)),
        name=name,
    )(dc, dc, dpooled, dpooled, proj_a, proj_a, conv_w)


def _in_bwd(dfront, dq, dk, dv, dgates, w_a, w_g, h, dh2, gain, name, side=None):
    R = h.shape[0]
    nj = 1 + D_G // TC
    ni = R // TR
    w_a = w_a.astype(_MXU)
    w_f, w_q = w_a[:, :TC], w_a[:, TC : TC + ATT_W]
    w_k, w_v = w_a[:, TC + ATT_W : TC + 2 * ATT_W], w_a[:, TC + 2 * ATT_W :]

    def body(df_ref, dq_ref, dk_ref, dv_ref, dg_ref, wf_ref, wq_ref, wk_ref, wv_ref, wg_ref, h_ref, dh_ref, g_ref,
             o_ref, hn_ref, dgain_ref, acc):
        i, j = pl.program_id(0), pl.program_id(1)

        @pl.when(j == 0)
        def _():
            acc[...] = (_dot_t(df_ref[...], wf_ref[...]) + _dot_t(dq_ref[...], wq_ref[...])
                        + _dot_t(dk_ref[...], wk_ref[...]) + _dot_t(dv_ref[...], wv_ref[...]))

        @pl.when(j > 0)
        def _():
            acc[...] += _dot_t(dg_ref[...], wg_ref[...])

        @pl.when(jnp.logical_and(i == 0, j == 0))
        def _():
            dgain_ref[...] = jnp.zeros_like(dgain_ref)

        @pl.when(j == nj - 1)
        def _():
            g = g_ref[...]
            r, n, y = _rms_fwd(h_ref[...], g)
            dx, dgain = _rms_bwd(acc[...], r, n, g)
            dgain_ref[...] += dgain
            rows = _row_ids(TR, i, (TR, D))
            o_ref[...] = jnp.where(rows >= PAD_ROWS, dh_ref[...] + dx, 0.0)
            hn_ref[...] = y.astype(hn_ref.dtype)

    def row(w):
        return pl.BlockSpec((TR, w), lambda i, j: (i, 0))

    def whole(w):
        return pl.BlockSpec((D, w), lambda i, j: (0, 0))

    def gcol(i, j):
        return jnp.maximum(j - 1, 0)

    return _hosted_call(
        body, side,
        grid=(ni, nj),
        in_specs=[
            row(TC), row(ATT_W), row(ATT_W), row(ATT_W),
            pl.BlockSpec((TR, TC), lambda i, j: (i, gcol(i, j))),
            whole(TC), whole(ATT_W), whole(ATT_W), whole(ATT_W),
            pl.BlockSpec((D, TC), lambda i, j: (0, gcol(i, j))),
            row(D), row(D),
            pl.BlockSpec((1, D), lambda i, j: (0, 0)),
        ],
        out_specs=[row(D), row(D), pl.BlockSpec((1, D), lambda i, j: (0, 0))],
        out_shape=[jax.ShapeDtypeStruct((R, D), F32), jax.ShapeDtypeStruct((R, D), _MXU), jax.ShapeDtypeStruct((1, D), F32)],
        scratch_shapes=[pltpu.VMEM((TR, D), F32)],
        name=name,
        operands=(dfront, dq, dk, dv, dgates, w_f, w_q, w_k, w_v, w_g.astype(_MXU), h, dh2, gain),
    )


def _pool_blockdiag(w_grp):
    eye = jnp.eye(len(POOL_WINDOWS), dtype=w_grp.dtype)
    return jnp.einsum("gcd,gh->gchd", w_grp, eye).reshape(POOL_CH, D)


def _pool_blockdiag_grad(dw_bd):
    d4 = dw_bd.reshape(len(POOL_WINDOWS), POOL_GC, len(POOL_WINDOWS), D // len(POOL_WINDOWS))
    return jnp.stack([d4[g, :, g, :] for g in range(len(POOL_WINDOWS))])


def _local_step(x, target, p, riders=None):
    row = lambda a: a.reshape(1, -1)
    riders = riders or {}
    g = {k: [None] * N_LAYERS for k in ("norm1", "w_in", "conv_dw_w", "conv_dw_b", "conv_ln_g", "conv_ln_b", "w_conv_out",
                                        "b_conv_out", "w_pool_grp", "pool_scale", "w_attn_out", "w_o", "norm2", "w_up",
                                        "ffn_dw_w", "ffn_dw_b", "w_down")}

    def with_rider(name, call):
        make, land = riders.get(name, (None, None))
        side = make(g) if make else None
        res = call(side)
        res = list(res) if isinstance(res, (list, tuple)) else [res]
        if side is None:
            return res
        n_own = len(res) - len(side.out_shape)
        land(res[n_own:])
        return res[:n_own]

    h = jnp.concatenate([jnp.zeros((PAD_ROWS, D), F32), p["meta"], x], axis=0)
    saved = []
    for l in range(N_LAYERS):
        w_a, w_g = p["w_in"][l][:, :D_A], p["w_in"][l][:, D_A:]
        conv_w = jnp.concatenate([p["conv_dw_w"][l], jnp.zeros((1, CONV_CH), F32)], axis=0)
        cw3 = p["ffn_dw_w"][l].reshape(FFN_K, 2, D_FF)
        cb2 = p["ffn_dw_b"][l].reshape(2, D_FF)
        mw = (row(p["conv_ln_g"][l]), row(p["conv_ln_b"][l]), p["w_conv_out"][l].astype(_MXU), row(p["b_conv_out"][l]),
              _pool_blockdiag(p["w_pool_grp"][l]).astype(_MXU), row(p["pool_scale"][l]),
              p["w_attn_out"][l].astype(_MXU), p["w_o"][l].astype(_MXU))
        proj_a, qkv = _rms_matmul(h, row(p["norm1"][l]), w_a, f"proj_a{l}", qkv_out=True)
        (gates,) = with_rider(f"proj_g{l}", lambda side: _rms_matmul(h, row(p["norm1"][l]), w_g, f"proj_g{l}", side=side))
        c, pooled = _branch_pre(proj_a, conv_w, row(p["conv_dw_b"][l]), f"branch_pre{l}")
        att = _attn_fwd(qkv, f"attn_fwd{l}")
        h2 = _mix_fwd(h, gates, c, pooled, att, mw, f"mix_fwd{l}")
        (u3,) = with_rider(f"ffn_up{l}", lambda side: _rms_matmul(h2, row(p["norm2"][l]), p["w_up"][l], f"ffn_up{l}",
                                                                split_out=True, side=side))
        h3 = _ffn_fwd(u3, h2, cw3, cb2, p["w_down"][l], f"ffn_fwd{l}")
        saved.append(dict(h=h, w_a=w_a, w_g=w_g, conv_w=conv_w, cw3=cw3, cb2=cb2, mw=mw, proj_a=proj_a, gates=gates,
                          qkv=qkv, c=c, pooled=pooled, att=att, h2=h2, u3=u3))
        h = h3

    dh, loss, d_final = _loss_bwd(h, target, row(p["final_norm"]), "loss_bwd")

    for l in reversed(range(N_LAYERS)):
        s = saved[l]
        dc3, g["w_down"][l], dwb = with_rider(f"ffn_bwd1_{l}", lambda side: _ffn_bwd1(
            dh, s["u3"], s["cw3"], s["cb2"], p["w_down"][l], f"ffn_bwd1_{l}", side=side))
        g["ffn_dw_w"][l] = jnp.transpose(dwb[:, :FFN_K, :], (1, 0, 2)).reshape(FFN_K, 2 * D_FF)
        g["ffn_dw_b"][l] = dwb[:, FFN_K, :].reshape(2 * D_FF)
        du3, dh2, xn2, dg2 = _ffn_bwd2(dc3, s["cw3"], p["w_up"][l], s["h2"], dh, row(p["norm2"][l]), f"ffn_bwd2_{l}")
        g["norm2"][l] = dg2[0]
        g["w_up"][l] = _tdot_acc(xn2, du3, f"dw_up{l}")
        (dgates, dc, dpooled, datt, mixed, s_act, dya, dybs, dyc, acc_d, acc_c) = _mix_bwd(
            dh2, s["gates"], s["c"], s["pooled"], s["att"], s["mw"], f"mix_bwd{l}")
        g["w_o"][l] = _tdot_acc(mixed, dh2, f"dw_o{l}")
        g["w_conv_out"][l] = _tdot_acc(s_act, dya, f"dw_conv_out{l}")
        g["w_pool_grp"][l] = _pool_blockdiag_grad(_tdot_acc(s["pooled"], dybs, f"dw_pool{l}"))
        g["w_attn_out"][l] = _tdot_acc(s["att"], dyc, f"dw_attn_out{l}")
        g["b_conv_out"][l] = acc_d[0]
        g["pool_scale"][l] = acc_d[1]
        g["conv_ln_g"][l] = acc_c[0]
        g["conv_ln_b"][l] = acc_c[1]
        g["conv_dw_b"][l] = acc_c[2]
        dq, dk, dv = _attn_bwd(s["qkv"], s["att"], datt, f"attn_bwd{l}")
        dfront, dcw = _branch_bwd(dc, dpooled, s["proj_a"], s["conv_w"], f"branch_bwd{l}")
        g["conv_dw_w"][l] = dcw[:CONV_K]
        dh, hn, dg1 = with_rider(f"in_bwd{l}", lambda side: _in_bwd(
            dfront, dq, dk, dv, dgates, s["w_a"], s["w_g"], s["h"], dh2, row(p["norm1"][l]), f"in_bwd{l}", side=side))
        g["norm1"][l] = dg1[0]
        g["w_in"][l] = jnp.concatenate(
            [_tdot_acc(hn, part, f"dw_in{l}_{k}") for k, part in enumerate((dfront, dq, dk, dv, dgates))], axis=1)

    grads = dict(g, final_norm=d_final[0], meta=dh[PAD_ROWS:BLK])
    return loss, dh[BLK:], grads


def _axes():
    return lax.axis_index("x"), lax.axis_index("y"), lax.axis_index("c")


def _transfer(side, name):
    def body(*refs):
        s_in, s_out = len(side.arrays), len(side.out_shape)
        ins, outs, scratch = refs[:s_in], refs[s_in : s_in + s_out], refs[s_in + s_out :]
        side.start(ins, outs, scratch)
        side.finish(ins, outs, scratch)

    return _pcall(
        body,
        in_specs=[ANY] * len(side.arrays),
        out_specs=[ANY] * len(side.out_shape),
        out_shape=list(side.out_shape),
        scratch_shapes=list(side.scratch),
        compiler_params=pltpu.CompilerParams(has_side_effects=True),
        name=name,
    )(*side.arrays)


def _gather_side(blobs):
    nt = len(blobs)
    flips = ((1, 0), (0, 1), (1, 1))

    def local(srcs, dsts, sems):
        x, y, _ = _axes()
        return [pltpu.make_async_copy(srcs[t], dsts[t].at[2 * x + y], sems[2].at[t]) for t in range(nt)]

    def remote(srcs, dsts, sems, arriving):
        send_sems, recv_sems, _ = sems
        x, y, c = _axes()
        out = []
        for t in range(nt):
            for f, (fx, fy) in enumerate(flips):
                px = 1 - x if fx else x
                py = 1 - y if fy else y
                out.append(pltpu.make_async_remote_copy(
                    srcs[t], dsts[t].at[2 * px + py if arriving else 2 * x + y], send_sems.at[t, f], recv_sems.at[t, f],
                    device_id=(px, py, c), device_id_type=MESH))
        return out

    def start(srcs, dsts, sems):
        for cp in local(srcs, dsts, sems) + remote(srcs, dsts, sems, arriving=False):
            cp.start()

    def finish(srcs, dsts, sems):
        for cp in remote(srcs, dsts, sems, arriving=False):
            cp.wait_send()
        for cp in remote(srcs, dsts, sems, arriving=True):
            cp.wait_recv()
        for cp in local(srcs, dsts, sems):
            cp.wait()

    return _Side(
        arrays=tuple(blobs),
        out_shape=tuple(jax.ShapeDtypeStruct((4,) + b.shape, b.dtype) for b in blobs),
        scratch=(pltpu.SemaphoreType.DMA((nt, 3)), pltpu.SemaphoreType.DMA((nt, 3)), pltpu.SemaphoreType.DMA((nt,))),
        start=start,
        finish=finish,
    )


def _exchange_side(contribs):
    nt = len(contribs)

    def plan(srcs, dsts, sems):
        send_sems, recv_sems, local_sems = sems
        x, y, c = _axes()
        sibling = (x, y, 1 - c)
        chips = [(1 - x, y), (x, 1 - y), (1 - x, 1 - y)]

        def slot(t, px, py, pc):
            return dsts[t].at[4 * px + 2 * py + pc]

        def copy(t, k, source, target_slot, to):
            return pltpu.make_async_remote_copy(source, target_slot, send_sems.at[t, k], recv_sems.at[t, k],
                                                device_id=to, device_id_type=MESH)

        def mine(t):
            return srcs[t].at[2 * x + y]

        def own():
            return [pltpu.make_async_copy(mine(t), slot(t, x, y, c), local_sems.at[t]) for t in range(nt)]

        def first():
            out = []
            for t in range(nt):
                out.append(copy(t, 0, mine(t), slot(t, x, y, c), sibling))
                out += [copy(t, 1 + j, srcs[t].at[2 * px + py], slot(t, x, y, c), (px, py, c))
                        for j, (px, py) in enumerate(chips)]
            return out

        def relay():
            return [(copy(t, 1 + j, mine(t), slot(t, px, py, c), (px, py, c)),
                     copy(t, 4 + j, slot(t, px, py, c), slot(t, px, py, c), sibling))
                    for j, (px, py) in enumerate(chips) for t in range(nt)]

        def from_sibling():
            out = [copy(t, 0, mine(t), slot(t, x, y, 1 - c), sibling) for t in range(nt)]
            return out + [copy(t, 4 + j, mine(t), slot(t, px, py, 1 - c), sibling)
                          for t in range(nt) for j, (px, py) in enumerate(chips)]

        return own, first, relay, from_sibling

    def start(srcs, dsts, sems):
        own, first, _, _ = plan(srcs, dsts, sems)
        for cp in own() + first():
            cp.start()

    def finish(srcs, dsts, sems):
        own, first, relay, from_sibling = plan(srcs, dsts, sems)
        passed = []
        for arriving, onward in relay():
            arriving.wait_recv()
            onward.start()
            passed.append(onward)
        for cp in from_sibling():
            cp.wait_recv()
        for cp in first() + passed:
            cp.wait_send()
        for cp in own():
            cp.wait()

    return _Side(
        arrays=tuple(contribs),
        out_shape=tuple(jax.ShapeDtypeStruct((8,) + a.shape[1:], a.dtype) for a in contribs),
        scratch=(pltpu.SemaphoreType.DMA((nt, 7)), pltpu.SemaphoreType.DMA((nt, 7)), pltpu.SemaphoreType.DMA((nt,))),
        start=start,
        finish=finish,
    )


ADAM_BLOCK_BYTES = 768 * 1024


def _adamw(w, m, v, slot_list, name):
    shape = w.shape
    parts = len(slot_list)
    cols = shape[-1]
    rows = _size(shape) // cols // parts
    blk = max(r for r in range(8, rows + 1, 8) if rows % r == 0 and r * cols * 4 <= ADAM_BLOCK_BYTES) \
        if rows % 8 == 0 else rows

    def body(w_ref, m_ref, v_ref, *refs):
        slot_refs, (g_ref, d_ref, nm_ref, nv_ref) = refs[:parts], refs[parts:]
        g = None
        for k, gs_ref in enumerate(slot_refs):
            gk = gs_ref[0]
            for dev in range(1, 8):
                gk = gk + gs_ref[dev]
            g = gk if g is None else jnp.where(pl.program_id(0) == k, gk, g)
        w_, m_, v_ = w_ref[...], m_ref[...], v_ref[...]
        m_new = ADAM_B1 * m_ + (1.0 - ADAM_B1) * g
        v_new = ADAM_B2 * v_ + (1.0 - ADAM_B2) * (g * g)
        m_hat = m_new / (1.0 - ADAM_B1 ** ADAM_STEP)
        v_hat = v_new / (1.0 - ADAM_B2 ** ADAM_STEP)
        g_ref[...] = g
        d_ref[...] = -ADAM_LR * (m_hat / (jnp.sqrt(v_hat) + ADAM_EPS) + ADAM_WD * w_)
        nm_ref[...] = m_new
        nv_ref[...] = v_new

    spec = pl.BlockSpec((None, blk, cols), lambda p, i: (p, i, 0))
    slot_specs = [pl.BlockSpec((8, blk, cols), lambda p, i, k=k: (0, jnp.where(p == k, i, 0), 0)) for k in range(parts)]
    flat = (parts, rows, cols)
    outs = _pcall(
        body,
        grid=(parts, rows // blk),
        in_specs=[spec, spec, spec] + slot_specs,
        out_specs=[spec] * 4,
        out_shape=[jax.ShapeDtypeStruct(flat, F32)] * 4,
        compiler_params=_cp(("arbitrary", "arbitrary")),
        name=name,
    )(w.reshape(flat), m.reshape(flat), v.reshape(flat), *[s.reshape(8, rows, cols) for s in slot_list])
    return [o.reshape(shape) for o in outs]


_PARAMS = (
    ("meta", (N_META, D), 1),
    ("norm1", (N_LAYERS, D), None),
    ("w_in", (N_LAYERS, D, D_A + D_G), 2),
    ("conv_dw_w", (N_LAYERS, CONV_K, CONV_CH), 2),
    ("conv_dw_b", (N_LAYERS, CONV_CH), None),
    ("conv_ln_g", (N_LAYERS, CONV_CH), None),
    ("conv_ln_b", (N_LAYERS, CONV_CH), None),
    ("w_conv_out", (N_LAYERS, CONV_CH, D), 2),
    ("b_conv_out", (N_LAYERS, D), None),
    ("w_pool_grp", (N_LAYERS, len(POOL_WINDOWS), POOL_GC, D // len(POOL_WINDOWS)), 3),
    ("pool_scale", (N_LAYERS, D), None),
    ("w_attn_out", (N_LAYERS, ATT_W, D), 2),
    ("w_o", (N_LAYERS, D, D), 1),
    ("norm2", (N_LAYERS, D), None),
    ("w_up", (N_LAYERS, D, 2 * D_FF), 2),
    ("ffn_dw_w", (N_LAYERS, FFN_K, 2 * D_FF), 2),
    ("ffn_dw_b", (N_LAYERS, 2 * D_FF), None),
    ("w_down", (N_LAYERS, D_FF, D), 1),
    ("final_norm", (D,), None),
)
_MIXER = ("w_in", "w_conv_out", "w_pool_grp", "w_attn_out", "w_o")
_FFN = ("w_up", "w_down")
_BIG = _MIXER + _FFN
_SMALL_SHARDED = ("meta", "conv_dw_w", "ffn_dw_w")
_SHARD_AXIS = {n: ax for n, _, ax in _PARAMS}


def _size(shape):
    n = 1
    for d in shape:
        n *= d
    return n


def _pack(parts, lanes, row_multiple):
    flat = jnp.concatenate([a.reshape(-1) for a in parts])
    rows = -(-flat.shape[0] // lanes)
    rows = -(-rows // row_multiple) * row_multiple
    flat = jnp.pad(flat, (0, rows * lanes - flat.shape[0]))
    return flat.reshape(rows, lanes)


def _unpack(blob, shapes):
    flat = blob.reshape(-1)
    out, off = [], 0
    for s in shapes:
        n = _size(s)
        out.append(flat[off : off + n].reshape(s))
        off += n
    return out


def _shard(a, ax, s):
    n = a.shape[ax] // 4
    return lax.slice_in_dim(a, s * n, (s + 1) * n, axis=ax)


def kernel(x, meta, norm1, w_in, conv_dw_w, conv_dw_b, conv_ln_g, conv_ln_b, w_conv_out, b_conv_out, w_pool_grp, pool_scale, w_attn_out, w_o, norm2, w_up, ffn_dw_w, ffn_dw_b, w_down, final_norm, loss_target, m_meta, m_norm1, m_w_in, m_conv_dw_w, m_conv_dw_b, m_conv_ln_g, m_conv_ln_b, m_w_conv_out, m_b_conv_out, m_w_pool_grp, m_pool_scale, m_w_attn_out, m_w_o, m_norm2, m_w_up, m_ffn_dw_w, m_ffn_dw_b, m_w_down, m_final_norm, v_meta, v_norm1, v_w_in, v_conv_dw_w, v_conv_dw_b, v_conv_ln_g, v_conv_ln_b, v_w_conv_out, v_b_conv_out, v_w_pool_grp, v_pool_scale, v_w_attn_out, v_w_o, v_norm2, v_w_up, v_ffn_dw_w, v_ffn_dw_b, v_w_down, v_final_norm):
    names = [n for n, _, _ in _PARAMS]
    w_loc = dict(zip(names, (meta, norm1, w_in, conv_dw_w, conv_dw_b, conv_ln_g, conv_ln_b, w_conv_out, b_conv_out, w_pool_grp, pool_scale, w_attn_out, w_o, norm2, w_up, ffn_dw_w, ffn_dw_b, w_down, final_norm)))
    m_loc = dict(zip(names, (m_meta, m_norm1, m_w_in, m_conv_dw_w, m_conv_dw_b, m_conv_ln_g, m_conv_ln_b, m_w_conv_out, m_b_conv_out, m_w_pool_grp, m_pool_scale, m_w_attn_out, m_w_o, m_norm2, m_w_up, m_ffn_dw_w, m_ffn_dw_b, m_w_down, m_final_norm)))
    v_loc = dict(zip(names, (v_meta, v_norm1, v_w_in, v_conv_dw_w, v_conv_dw_b, v_conv_ln_g, v_conv_ln_b, v_w_conv_out, v_b_conv_out, v_w_pool_grp, v_pool_scale, v_w_attn_out, v_w_o, v_norm2, v_w_up, v_ffn_dw_w, v_ffn_dw_b, v_w_down, v_final_norm)))

    full = {n: w_loc[n] for n, _, ax in _PARAMS if ax is None}
    full.update({n: [None, None] for n in _BIG})

    def gather(group, l, extra=()):
        return _gather_side([w_loc[n][l].astype(_MXU) for n in group] + list(extra))

    def install(group, l, gathered):
        for n, g4 in zip(group, gathered):
            full[n][l] = jnp.concatenate([g4[s] for s in range(4)], axis=_SHARD_AXIS[n] - 1)

    small = _pack([w_loc[n] for n in _SMALL_SHARDED], BLK, 8)
    *first, small4 = _transfer(gather(_MIXER, 0, [small]), "allgather_weights")
    install(_MIXER, 0, first)
    per_chip = [_unpack(small4[s], [w_loc[n].shape for n in _SMALL_SHARDED]) for s in range(4)]
    for k, n in enumerate(_SMALL_SHARDED):
        full[n] = jnp.concatenate([per_chip[s][k] for s in range(4)], axis=_SHARD_AXIS[n])

    slots = {}

    def exchange(group, l):
        def make(g):
            return _exchange_side([jnp.stack([_shard(g[n][l], _SHARD_AXIS[n] - 1, s) for s in range(4)]) for n in group])

        return make, lambda landed: slots.update({(n, l): a for n, a in zip(group, landed)})

    riders = {
        "proj_g0": (lambda g: gather(_FFN, 0), functools.partial(install, _FFN, 0)),
        "ffn_up0": (lambda g: gather(_BIG, 1), functools.partial(install, _BIG, 1)),
        "ffn_bwd1_0": exchange(_BIG, 1),
        "in_bwd0": exchange(_FFN, 0),
    }
    loss, grad_x, grads = _local_step(x[0], loss_target[0], full, riders)

    small_names = [n for n in names if n not in _BIG]
    small_grads = {n: jnp.stack(grads[n]) if isinstance(grads[n], list) else grads[n] for n in small_names}

    def small_blob(src, s=None):
        return _pack([src[n] if s is None or _SHARD_AXIS[n] is None else _shard(src[n], _SHARD_AXIS[n], s)
                      for n in small_names], BLOB_LANES, 8)

    make, land = exchange(_MIXER, 0)
    last = make(grads)
    last = _exchange_side(list(last.arrays) + [jnp.stack([small_blob(small_grads, s) for s in range(4)])])
    *late, small_slots = _transfer(last, "grad_exchange")
    land(late)
    results = {n: _adamw(w_loc[n], m_loc[n], v_loc[n], [slots[n, 0], slots[n, 1]], f"adamw_{n}") for n in _BIG}
    small_out = _adamw(small_blob(w_loc), small_blob(m_loc), small_blob(v_loc), [small_slots], "adamw_small")
    small_shapes = [w_loc[n].shape for n in small_names]
    for k, blob in enumerate(small_out):
        for n, a in zip(small_names, _unpack(blob, small_shapes)):
            results.setdefault(n, [None] * 4)[k] = a
    loss = lax.psum(loss[0, 0], ("x", "y", "c"))
    outs = [loss, grad_x[None]]
    for k in range(4):
        outs.extend(results[n][k] for n in names)
    return tuple(outs)
```

```python
import functools
from typing import Callable, NamedTuple

import jax
import jax.numpy as jnp
from jax import lax
from jax.experimental import pallas as pl
from jax.experimental.pallas import tpu as pltpu

F32 = jnp.float32
_MXU = jnp.bfloat16

D = 1024
N_META = 16
BLK = 128
CH = 2 * BLK
ATTN_UNROLLS = (8, 4, 2, 1)
PAD_ROWS = BLK - N_META
N_LAYERS = 2
CONV_CH = 256
CONV_K = 31
POOL_CH = 256
POOL_WINDOWS = (2, 4, 8, 16)
POOL_GC = 64
ATT_W = 512
NH = 4
HD = 128
D_A = 2 * CONV_CH + POOL_CH + 3 * ATT_W
D_G = 3 * D
D_FF = 3 * D
FFN_K = 3
EPS = 1e-6
Q_SCALE = HD ** -0.5
LOG2E = 1.4426950408889634
LN2 = 0.6931471805599453

TR = 384
TB = 128
HALO = 32
HALO8 = 8
TC = 768
TF = 512
VMEM_LIMIT = 48 * 1024 * 1024
DW_STEPS = 12

ADAM_LR = 0.001
ADAM_B1 = 0.9
ADAM_B2 = 0.999
ADAM_EPS = 1e-08
ADAM_WD = 0.01
ADAM_STEP = 10

GELU_C0 = 0.7978845608028654
GELU_C1 = 0.044715

MESH = pl.DeviceIdType.MESH
ANY = pl.BlockSpec(memory_space=pl.ANY)

BLOB_LANES = 1024


def _pcall(body, **kw):
    return pl.pallas_call(body, **kw)


def _cp(sem):
    return pltpu.CompilerParams(dimension_semantics=sem, vmem_limit_bytes=VMEM_LIMIT)


def _dot(a, b):
    return jnp.dot(a.astype(_MXU), b.astype(_MXU), preferred_element_type=F32)


def _dot_t(a, b):
    return lax.dot_general(a.astype(_MXU), b.astype(_MXU), (((1,), (1,)), ((), ())), preferred_element_type=F32)


def _tdot(a, b):
    return lax.dot_general(a.astype(_MXU), b.astype(_MXU), (((0,), (0,)), ((), ())), preferred_element_type=F32)


def _sig(x):
    return 1.0 / (1.0 + jnp.exp(-x))


def _rms_fwd(x, g):
    r = lax.rsqrt(jnp.mean(x * x, axis=-1, keepdims=True) + EPS)
    n = x * r
    return r, n, n * g


def _rms_bwd(dy, r, n, g):
    dgain = jnp.sum(dy * n, axis=0, keepdims=True)
    dn = dy * g
    dx = r * (dn - n * jnp.mean(dn * n, axis=-1, keepdims=True))
    return dx, dgain


def _row_ids(tile_rows, i, shape):
    return i * tile_rows + lax.broadcasted_iota(jnp.int32, shape, 0)


class _Side(NamedTuple):
    arrays: tuple
    out_shape: tuple
    scratch: tuple
    start: Callable
    finish: Callable


def _hosted_call(body, side, grid, in_specs, out_specs, out_shape, scratch_shapes, name, operands):
    n_in, n_out, n_scr = len(in_specs), len(out_specs), len(scratch_shapes)
    if side is None:
        return _pcall(body, grid=grid, in_specs=in_specs, out_specs=out_specs, out_shape=out_shape,
                      scratch_shapes=scratch_shapes, compiler_params=_cp(("arbitrary",) * len(grid)), name=name)(*operands)
    s_in, s_out = len(side.arrays), len(side.out_shape)

    def hosted(*refs):
        ins, refs = refs[:n_in], refs[n_in:]
        side_ins, refs = refs[:s_in], refs[s_in:]
        outs, refs = refs[:n_out], refs[n_out:]
        side_outs, refs = refs[:s_out], refs[s_out:]
        scratch, side_scratch = refs[:n_scr], refs[n_scr:]
        ids = [pl.program_id(a) for a in range(len(grid))]
        first = functools.reduce(jnp.logical_and, [p == 0 for p in ids])
        last = functools.reduce(jnp.logical_and, [p == n - 1 for p, n in zip(ids, grid)])

        @pl.when(first)
        def _():
            side.start(side_ins, side_outs, side_scratch)

        body(*ins, *outs, *scratch)

        @pl.when(last)
        def _():
            side.finish(side_ins, side_outs, side_scratch)

    return _pcall(
        hosted,
        grid=grid,
        in_specs=list(in_specs) + [ANY] * s_in,
        out_specs=list(out_specs) + [ANY] * s_out,
        out_shape=list(out_shape) + list(side.out_shape),
        scratch_shapes=list(scratch_shapes) + list(side.scratch),
        compiler_params=_cp(("arbitrary",) * len(grid)),
        name=name,
    )(*operands, *side.arrays)


def _rms_matmul(h, gain, w, name, split_out=False, qkv_out=False, side=None):
    R, N = h.shape[0], w.shape[1]
    nj = N // TC
    tr = R // DW_STEPS
    half = nj // 2

    def body(h_ref, g_ref, w_ref, o_ref, *rest):
        xn_ref = rest[-1]
        j = pl.program_id(1)

        @pl.when(j == 0)
        def _():
            _, _, y = _rms_fwd(h_ref[...], g_ref[...])
            xn_ref[...] = y.astype(xn_ref.dtype)

        out = jnp.dot(xn_ref[...], w_ref[...], preferred_element_type=F32)
        o_ref[...] = out
        if qkv_out:
            @pl.when(j >= 1)
            def _():
                col = lax.broadcasted_iota(jnp.int32, (1, TC), 1) + (j - 1) * TC
                sc = jnp.where(col < ATT_W, Q_SCALE * LOG2E, 1.0).astype(F32)
                rest[0][...] = (out * sc).astype(rest[0].dtype)

    if split_out:
        out_shapes = [jax.ShapeDtypeStruct((2, R, N // 2), F32)]
        out_specs = [pl.BlockSpec((None, tr, TC), lambda i, j: (j // half, i, j % half))]
    else:
        out_shapes = [jax.ShapeDtypeStruct((R, N), F32)]
        out_specs = [pl.BlockSpec((tr, TC), lambda i, j: (i, j))]
    if qkv_out:
        out_shapes.append(jax.ShapeDtypeStruct((R, 3 * ATT_W), _MXU))
        out_specs.append(pl.BlockSpec((tr, TC), lambda i, j: (i, jnp.maximum(j - 1, 0))))
    res = _hosted_call(
        body, side,
        grid=(R // tr, nj),
        in_specs=[
            pl.BlockSpec((tr, D), lambda i, j: (i, 0)),
            pl.BlockSpec((1, D), lambda i, j: (0, 0)),
            pl.BlockSpec((D, TC), lambda i, j: (0, j)),
        ],
        out_specs=out_specs,
        out_shape=out_shapes,
        scratch_shapes=[pltpu.VMEM((tr, D), _MXU)],
        name=name,
        operands=(h, gain, w.astype(_MXU)),
    )
    return res[0] if side is None and not qkv_out else res


def _sum_mat(w):
    rowi = lax.broadcasted_iota(jnp.int32, (w, w), 0)
    coli = lax.broadcasted_iota(jnp.int32, (w, w), 1)
    return (rowi > coli).astype(_MXU)


def _hi_lo_rows(x):
    hi = x.astype(_MXU)
    lo = (x - hi.astype(F32)).astype(_MXU)
    return jnp.concatenate([hi, lo], axis=0)


def _fold_sums(r):
    return r[:BLK] + r[BLK:]


def _attn_logs(z2, valid):
    m = jnp.minimum(z2, 0.0)
    d = m - z2
    t = jnp.log2(1.0 + jnp.exp2(m + d))
    lk = d - t
    if valid is not None:
        lk = jnp.where(valid, lk, 0.0)
    return lk, m - t


def _attn_weights(log_beta, r, valid, s_after):
    log_a = log_beta + r + s_after
    if valid is not None:
        log_a = jnp.where(valid, log_a, -1e30)
    return jnp.exp2(log_a)


def _attn_walk(i, phases, n_free, carry):
    ci = jnp.maximum(i - 1, 0) // 2

    def advance(states, stages, cr):
        for ph in stages:
            for u in range(len(states)):
                states[u], cr = ph(states[u], cr)
        return states, cr

    def at(c):
        return pl.multiple_of(BLK + c * CH, BLK)

    ends = [phases[0](at(ci), CH, True), phases[0](0, BLK, True)]
    ends, _ = advance(ends, phases[1 : 1 + n_free], None)
    _, carry = advance(ends[:1], phases[1 + n_free :], carry)
    left = ci
    for unroll in ATTN_UNROLLS:
        def group(t, cr, unroll=unroll, left=left):
            states = [phases[0](at(left - 1 - unroll * t - u), CH, False) for u in range(unroll)]
            return advance(states, phases[1:], cr)[1]

        carry = lax.fori_loop(0, left // unroll, group, carry)
        left = left % unroll
    return advance(ends[1:], phases[1 + n_free :], carry)[1]


def _attn_mask(i, off, w):
    qpos = i * BLK + lax.broadcasted_iota(jnp.int32, (BLK, w), 0)
    kpos = off + lax.broadcasted_iota(jnp.int32, (BLK, w), 1)
    return jnp.logical_and(kpos < qpos, kpos >= PAD_ROWS)


def _attn_fwd(qkv, name):
    R = qkv.shape[0]
    nb = R // BLK

    def body(q_ref, k_ref, v_ref, o_ref):
        i = pl.program_id(1)
        q = q_ref[...]
        sms = {CH: _sum_mat(CH), BLK: _sum_mat(BLK)}

        def scores(off, w, masked):
            valid = _attn_mask(i, off, w) if masked else None
            return off, w, valid, _dot_t(q, k_ref[pl.ds(off, w), :])

        def sums(state, carry):
            off, w, valid, z = state
            lk, log_beta = _attn_logs(z, valid)
            return (off, w, valid, log_beta, _dot(lk, sms[w]), jnp.sum(lk, axis=1, keepdims=True)), carry

        def output(state, carry):
            off, w, valid, log_beta, r, total = state
            acc, s_after = carry
            a = _attn_weights(log_beta, r, valid, s_after)
            return None, (acc + _dot(a, v_ref[pl.ds(off, w), :]), s_after + total)

        acc, _ = _attn_walk(i, (scores, sums, output), 1, (jnp.zeros((BLK, HD), F32), jnp.zeros((BLK, 1), F32)))
        o_ref[...] = acc

    return _pcall(
        body,
        grid=(NH, nb),
        in_specs=[
            pl.BlockSpec((BLK, HD), lambda h, i: (i, h)),
            pl.BlockSpec((R, HD), lambda h, i: (0, NH + h)),
            pl.BlockSpec((R, HD), lambda h, i: (0, 2 * NH + h)),
        ],
        out_specs=pl.BlockSpec((BLK, HD), lambda h, i: (i, h)),
        out_shape=jax.ShapeDtypeStruct((R, ATT_W), F32),
        compiler_params=_cp(("parallel", "arbitrary")),
        name=name,
    )(qkv, qkv, qkv)


def _pool_consts(i):
    lane = lax.broadcasted_iota(jnp.int32, (1, POOL_CH), 1)
    wsize = jnp.where(lane < POOL_GC, 2.0, jnp.where(lane < 2 * POOL_GC, 4.0, jnp.where(lane < 3 * POOL_GC, 8.0, 16.0)))
    return lane, wsize


def _pool_div(rows, wsize):
    pos1 = (rows - (PAD_ROWS - 1)).astype(F32)
    return jnp.clip(pos1, 1.0, wsize)


def _lane_select(lane, s2, s4, s8, s16):
    return jnp.where(lane < POOL_GC, s2, jnp.where(lane < 2 * POOL_GC, s4, jnp.where(lane < 3 * POOL_GC, s8, s16)))


def _branch_pre(proj_a, conv_w, conv_b, name):
    R = proj_a.shape[0]
    nh = TR // HALO

    def body(t_ref, h_ref, w_ref, b_ref, c_ref, p_ref, gext, pext):
        i = pl.program_id(0)
        t = t_ref[...]
        hl = jnp.where(i == 0, 0.0, h_ref[...])
        gext[pl.ds(0, HALO), :] = hl[:, :CONV_CH] * _sig(hl[:, CONV_CH : 2 * CONV_CH])
        gext[pl.ds(HALO, TR), :] = t[:, :CONV_CH] * _sig(t[:, CONV_CH : 2 * CONV_CH])
        acc = jnp.zeros((TR, CONV_CH), F32) + b_ref[...]
        for k in range(CONV_K):
            acc = acc + w_ref[pl.ds(k, 1), :] * gext[pl.ds(HALO - (CONV_K - 1) + k, TR), :]
        c_ref[...] = acc

        p = t[:, 2 * CONV_CH :]
        pext[pl.ds(0, HALO), :] = hl[:, 2 * CONV_CH :]
        pext[pl.ds(HALO, TR), :] = p

        def back(k):
            return pext[pl.ds(HALO - k, TR), :]

        s2 = p + back(1)
        s4 = s2 + back(2) + back(3)
        s8 = s4 + back(4) + back(5) + back(6) + back(7)
        s16 = s8
        for k in range(8, 16):
            s16 = s16 + back(k)
        lane, wsize = _pool_consts(i)
        div = _pool_div(_row_ids(TR, i, (TR, POOL_CH)), wsize)
        p_ref[...] = (_lane_select(lane, s2, s4, s8, s16) / div - p).astype(p_ref.dtype)

    return _pcall(
        body,
        grid=(R // TR,),
        in_specs=[
            pl.BlockSpec((TR, TC), lambda i: (i, 0)),
            pl.BlockSpec((HALO, TC), lambda i: (jnp.maximum(i * nh - 1, 0), 0)),
            pl.BlockSpec((HALO, CONV_CH), lambda i: (0, 0)),
            pl.BlockSpec((1, CONV_CH), lambda i: (0, 0)),
        ],
        out_specs=[pl.BlockSpec((TR, CONV_CH), lambda i: (i, 0)), pl.BlockSpec((TR, POOL_CH), lambda i: (i, 0))],
        out_shape=[jax.ShapeDtypeStruct((R, CONV_CH), F32), jax.ShapeDtypeStruct((R, POOL_CH), _MXU)],
        scratch_shapes=[pltpu.VMEM((TR + HALO, CONV_CH), F32), pltpu.VMEM((TR + HALO, POOL_CH), F32)],
        compiler_params=_cp(("parallel",)),
        name=name,
    )(proj_a, proj_a, conv_w, conv_b)


def _mix_values(c, pooled, att, gates, ln_g, ln_b, w_co, b_co, w_bd, p_scale, w_ao):
    mu = jnp.mean(c, axis=-1, keepdims=True)
    xc = c - mu
    rstd = lax.rsqrt(jnp.mean(xc * xc, axis=-1, keepdims=True) + EPS)
    nl = xc * rstd
    ln = nl * ln_g + ln_b
    sg = _sig(ln)
    s = ln * sg
    ya = _dot(s, w_co) + b_co
    ybr = _dot(pooled, w_bd)
    yb = ybr * p_scale
    yc = _dot(att, w_ao)
    g = _sig(gates)
    g0, g1, g2 = g[:, :D], g[:, D : 2 * D], g[:, 2 * D :]
    mixed = g0 * ya + g1 * yb + g2 * yc
    return dict(rstd=rstd, nl=nl, ln=ln, sg=sg, s=s, ya=ya, ybr=ybr, yb=yb, yc=yc, g0=g0, g1=g1, g2=g2, mixed=mixed)


_MIX_W_SPECS = [
    pl.BlockSpec((1, CONV_CH), lambda i: (0, 0)),
    pl.BlockSpec((1, CONV_CH), lambda i: (0, 0)),
    pl.BlockSpec((CONV_CH, D), lambda i: (0, 0)),
    pl.BlockSpec((1, D), lambda i: (0, 0)),
    pl.BlockSpec((POOL_CH, D), lambda i: (0, 0)),
    pl.BlockSpec((1, D), lambda i: (0, 0)),
    pl.BlockSpec((ATT_W, D), lambda i: (0, 0)),
    pl.BlockSpec((D, D), lambda i: (0, 0)),
]


def _mix_act_specs(t):
    return [
        pl.BlockSpec((t, D_G), lambda i: (i, 0)),
        pl.BlockSpec((t, CONV_CH), lambda i: (i, 0)),
        pl.BlockSpec((t, POOL_CH), lambda i: (i, 0)),
        pl.BlockSpec((t, ATT_W), lambda i: (i, 0)),
    ]


def _mix_fwd(h, gates, c, pooled, att, mw, name):
    R = h.shape[0]

    def body(h_ref, g_ref, c_ref, p_ref, a_ref, lg, lb, wco, bco, wbd, ps, wao, wo, o_ref):
        v = _mix_values(c_ref[...], p_ref[...], a_ref[...], g_ref[...], lg[...], lb[...], wco[...], bco[...],
                        wbd[...], ps[...], wao[...])
        out = h_ref[...] + _dot(v["mixed"], wo[...])
        rows = _row_ids(TB, pl.program_id(0), (TB, D))
        o_ref[...] = jnp.where(rows >= PAD_ROWS, out, 0.0)

    return _pcall(
        body,
        grid=(R // TB,),
        in_specs=[pl.BlockSpec((TB, D), lambda i: (i, 0))] + _mix_act_specs(TB) + _MIX_W_SPECS,
        out_specs=pl.BlockSpec((TB, D), lambda i: (i, 0)),
        out_shape=jax.ShapeDtypeStruct((R, D), F32),
        compiler_params=_cp(("parallel",)),
        name=name,
    )(h, gates, c, pooled, att, *mw)


def _ffn_conv(ut, uh, cw_ref, cb_ref, ext):
    ext[:, pl.ds(0, HALO8), :] = uh
    ext[:, pl.ds(HALO8, TR), :] = ut
    um1 = ext[:, pl.ds(HALO8 - 1, TR), :]
    um2 = ext[:, pl.ds(HALO8 - 2, TR), :]
    cw = cw_ref[...]
    conv = cw[0][:, None, :] * um2 + cw[1][:, None, :] * um1 + cw[2][:, None, :] * ut + cb_ref[...][:, None, :]
    return conv, um1, um2


def _gelu_parts(x):
    th = jnp.tanh(GELU_C0 * (x + GELU_C1 * x * x * x))
    return th, 0.5 * x * (1.0 + th)


def _ffn_in_specs(nrow8, order):
    n8 = TR // HALO8
    return [
        pl.BlockSpec((2, TR, TF), lambda *g: (0, order(*g)[0], order(*g)[1])),
        pl.BlockSpec((2, HALO8, TF), lambda *g: (0, jnp.maximum(order(*g)[0] * n8 - 1, 0), order(*g)[1])),
        pl.BlockSpec((FFN_K, 2, TF), lambda *g: (0, 0, order(*g)[1])),
        pl.BlockSpec((2, TF), lambda *g: (0, order(*g)[1])),
    ]


def _ffn_fwd(u3, h2, cw, cb, w_down, name):
    R = h2.shape[0]
    nj = D_FF // TF

    def body(u_ref, uh_ref, cw_ref, cb_ref, wd_ref, h_ref, o_ref, ext, acc):
        i, j = pl.program_id(0), pl.program_id(1)
        uh = jnp.where(i == 0, 0.0, uh_ref[...])
        conv, _, _ = _ffn_conv(u_ref[...], uh, cw_ref, cb_ref, ext)
        _, a = _gelu_parts(conv[0])
        part = _dot(a * conv[1], wd_ref[...])

        @pl.when(j == 0)
        def _():
            acc[...] = part

        @pl.when(j > 0)
        def _():
            acc[...] += part

        @pl.when(j == nj - 1)
        def _():
            rows = _row_ids(TR, i, (TR, D))
            o_ref[...] = jnp.where(rows >= PAD_ROWS, h_ref[...] + acc[...], 0.0)

    return _pcall(
        body,
        grid=(R // TR, nj),
        in_specs=_ffn_in_specs(R // HALO8, lambda i, j: (i, j))
        + [pl.BlockSpec((TF, D), lambda i, j: (j, 0)), pl.BlockSpec((TR, D), lambda i, j: (i, 0))],
        out_specs=pl.BlockSpec((TR, D), lambda i, j: (i, 0)),
        out_shape=jax.ShapeDtypeStruct((R, D), F32),
        scratch_shapes=[pltpu.VMEM((2, TR + HALO8, TF), F32), pltpu.VMEM((TR, D), F32)],
        compiler_params=_cp(("parallel", "arbitrary")),
        name=name,
    )(u3, u3, cw, cb, w_down.astype(_MXU), h2)


def _loss_bwd(h, target, gain, name):
    R = h.shape[0]

    def body(h_ref, t_ref, g_ref, dh_ref, loss_ref, dg_ref):
        i = pl.program_id(0)

        @pl.when(i == 0)
        def _():
            loss_ref[...] = jnp.zeros_like(loss_ref)
            dg_ref[...] = jnp.zeros_like(dg_ref)
            dh_ref[...] = jnp.zeros_like(dh_ref)

        @pl.when(i > 0)
        def _():
            g = g_ref[...]
            r, n, y = _rms_fwd(h_ref[...], g)
            e = y - t_ref[...]
            loss_ref[...] += (0.5 / D) * jnp.sum(jnp.sum(e * e, axis=1, keepdims=True), axis=0, keepdims=True)
            dx, dgain = _rms_bwd(e * (1.0 / D), r, n, g)
            dg_ref[...] += dgain
            dh_ref[...] = dx

    return _pcall(
        body,
        grid=(R // BLK,),
        in_specs=[
            pl.BlockSpec((BLK, D), lambda i: (i, 0)),
            pl.BlockSpec((BLK, D), lambda i: (jnp.maximum(i - 1, 0), 0)),
            pl.BlockSpec((1, D), lambda i: (0, 0)),
        ],
        out_specs=[
            pl.BlockSpec((BLK, D), lambda i: (i, 0)),
            pl.BlockSpec((1, 1), lambda i: (0, 0)),
            pl.BlockSpec((1, D), lambda i: (0, 0)),
        ],
        out_shape=[
            jax.ShapeDtypeStruct((R, D), F32),
            jax.ShapeDtypeStruct((1, 1), F32),
            jax.ShapeDtypeStruct((1, D), F32),
        ],
        compiler_params=_cp(("arbitrary",)),
        name=name,
    )(h, target, gain)


DW_BLOCK_BYTES = 6 * 1024 * 1024


def _tdot_acc(a, b, name):
    R, M = a.shape
    split = b.ndim == 3
    width = b.shape[-1]
    N = 2 * width if split else width
    tm = min(M, 512)
    tk = R // DW_STEPS
    tn = max(t for t in range(BLK, width + 1, BLK) if width % t == 0 and tk * t * b.dtype.itemsize <= DW_BLOCK_BYTES)
    per = width // tn

    def body(a_ref, b_ref, o_ref):
        part = _tdot(a_ref[...], b_ref[...])

        @pl.when(pl.program_id(2) == 0)
        def _():
            o_ref[...] = part

        @pl.when(pl.program_id(2) > 0)
        def _():
            o_ref[...] += part

    if split:
        b_spec = pl.BlockSpec((None, tk, tn), lambda m, n, k: (n // per, k, n % per))
    else:
        b_spec = pl.BlockSpec((tk, tn), lambda m, n, k: (k, n))
    return _pcall(
        body,
        grid=(M // tm, N // tn, R // tk),
        in_specs=[pl.BlockSpec((tk, tm), lambda m, n, k: (k, m)), b_spec],
        out_specs=pl.BlockSpec((tm, tn), lambda m, n, k: (m, n)),
        out_shape=jax.ShapeDtypeStruct((M, N), F32),
        compiler_params=_cp(("parallel", "parallel", "arbitrary")),
        name=name,
    )(a, b)


def _ffn_bwd1(dh3, u3, cw, cb, w_down, name, side=None):
    R = dh3.shape[0]
    nj = D_FF // TF

    def body(u_ref, uh_ref, cw_ref, cb_ref, wd_ref, dh_ref, dc_ref, dwd_ref, dwb_ref, ext):
        j, i = pl.program_id(0), pl.program_id(1)
        ut = u_ref[...]
        uh = jnp.where(i == 0, 0.0, uh_ref[...])
        conv, um1, um2 = _ffn_conv(ut, uh, cw_ref, cb_ref, ext)
        gt, val = conv[0], conv[1]
        th, a = _gelu_parts(gt)
        dh = dh_ref[...].astype(_MXU)
        dact = _dot_t(dh, wd_ref[...])
        dgelu = 0.5 * (1.0 + th) + 0.5 * gt * (1.0 - th * th) * (GELU_C0 * (1.0 + 3.0 * GELU_C1 * gt * gt))
        dgt = dact * val * dgelu
        dval = dact * a
        dc_ref[0] = dgt
        dc_ref[1] = dval

        @pl.when(i == 0)
        def _():
            dwb_ref[...] = jnp.zeros_like(dwb_ref)
            dwd_ref[...] = jnp.zeros_like(dwd_ref)

        dwd_ref[...] += _tdot(a * val, dh)

        for half, dcv in ((0, dgt), (1, dval)):
            dwb_ref[half, pl.ds(0, 1), :] += jnp.sum(um2[half] * dcv, axis=0, keepdims=True)
            dwb_ref[half, pl.ds(1, 1), :] += jnp.sum(um1[half] * dcv, axis=0, keepdims=True)
            dwb_ref[half, pl.ds(2, 1), :] += jnp.sum(ut[half] * dcv, axis=0, keepdims=True)
            dwb_ref[half, pl.ds(3, 1), :] += jnp.sum(dcv, axis=0, keepdims=True)

    return _hosted_call(
        body, side,
        grid=(nj, R // TR),
        in_specs=_ffn_in_specs(R // HALO8, lambda j, i: (i, j))
        + [pl.BlockSpec((TF, D), lambda j, i: (j, 0)), pl.BlockSpec((TR, D), lambda j, i: (i, 0))],
        out_specs=[
            pl.BlockSpec((2, TR, TF), lambda j, i: (0, i, j)),
            pl.BlockSpec((TF, D), lambda j, i: (j, 0)),
            pl.BlockSpec((2, 8, TF), lambda j, i: (0, 0, j)),
        ],
        out_shape=[
            jax.ShapeDtypeStruct((2, R, D_FF), F32),
            jax.ShapeDtypeStruct((D_FF, D), F32),
            jax.ShapeDtypeStruct((2, 8, D_FF), F32),
        ],
        scratch_shapes=[pltpu.VMEM((2, TR + HALO8, TF), F32)],
        name=name,
        operands=(u3, u3, cw, cb, w_down.astype(_MXU), dh3),
    )


def _ffn_bwd2(dc3, cw, w_up, h2, dh3, gain, name):
    R = h2.shape[0]
    nj = D_FF // TF
    n8 = TR // HALO8
    last8 = R // HALO8 - 1
    ni = R // TR

    def body(dc_ref, dn_ref, cw_ref, wg_ref, wv_ref, h_ref, dh_ref, g_ref, du_ref, o_ref, xn_ref, dg_ref, ext, acc):
        i, j = pl.program_id(0), pl.program_id(1)
        dc = dc_ref[...]
        ext[:, pl.ds(0, TR), :] = dc
        ext[:, pl.ds(TR, HALO8), :] = jnp.where(i == ni - 1, 0.0, dn_ref[...])
        cw = cw_ref[...]
        du = (cw[2][:, None, :] * dc + cw[1][:, None, :] * ext[:, pl.ds(1, TR), :]
              + cw[0][:, None, :] * ext[:, pl.ds(2, TR), :])
        du_ref[...] = du.astype(du_ref.dtype)
        part = _dot_t(du[0], wg_ref[...]) + _dot_t(du[1], wv_ref[...])

        @pl.when(j == 0)
        def _():
            acc[...] = part

        @pl.when(j > 0)
        def _():
            acc[...] += part

        @pl.when(jnp.logical_and(i == 0, j == 0))
        def _():
            dg_ref[...] = jnp.zeros_like(dg_ref)

        @pl.when(j == nj - 1)
        def _():
            g = g_ref[...]
            r, n, y = _rms_fwd(h_ref[...], g)
            dx, dgain = _rms_bwd(acc[...], r, n, g)
            dg_ref[...] += dgain
            rows = _row_ids(TR, i, (TR, D))
            o_ref[...] = jnp.where(rows >= PAD_ROWS, dh_ref[...] + dx, 0.0)
            xn_ref[...] = y.astype(xn_ref.dtype)

    w_up = w_up.astype(_MXU)
    return _pcall(
        body,
        grid=(ni, nj),
        in_specs=[
            pl.BlockSpec((2, TR, TF), lambda i, j: (0, i, j)),
            pl.BlockSpec((2, HALO8, TF), lambda i, j: (0, jnp.minimum((i + 1) * n8, last8), j)),
            pl.BlockSpec((FFN_K, 2, TF), lambda i, j: (0, 0, j)),
            pl.BlockSpec((D, TF), lambda i, j: (0, j)),
            pl.BlockSpec((D, TF), lambda i, j: (0, nj + j)),
            pl.BlockSpec((TR, D), lambda i, j: (i, 0)),
            pl.BlockSpec((TR, D), lambda i, j: (i, 0)),
            pl.BlockSpec((1, D), lambda i, j: (0, 0)),
        ],
        out_specs=[
            pl.BlockSpec((2, TR, TF), lambda i, j: (0, i, j)),
            pl.BlockSpec((TR, D), lambda i, j: (i, 0)),
            pl.BlockSpec((TR, D), lambda i, j: (i, 0)),
            pl.BlockSpec((1, D), lambda i, j: (0, 0)),
        ],
        out_shape=[
            jax.ShapeDtypeStruct((2, R, D_FF), _MXU),
            jax.ShapeDtypeStruct((R, D), F32),
            jax.ShapeDtypeStruct((R, D), _MXU),
            jax.ShapeDtypeStruct((1, D), F32),
        ],
        scratch_shapes=[pltpu.VMEM((2, TR + HALO8, TF), F32), pltpu.VMEM((TR, D), F32)],
        compiler_params=_cp(("arbitrary", "arbitrary")),
        name=name,
    )(dc3, dc3, cw, w_up, w_up, h2, dh3, gain)


def _mix_bwd(dh2, gates, c, pooled, att, mw, name):
    R = dh2.shape[0]

    def body(dh_ref, g_ref, c_ref, p_ref, a_ref, lg, lb, wco, bco, wbd, ps, wao, wo,
             dg_ref, dc_ref, dp_ref, da_ref, mx_ref, s_ref, dya_ref, dyb_ref, dyc_ref, accd_ref, accc_ref):
        v = _mix_values(c_ref[...], p_ref[...], a_ref[...], g_ref[...], lg[...], lb[...], wco[...], bco[...],
                        wbd[...], ps[...], wao[...])
        dmix = _dot_t(dh_ref[...], wo[...])
        for k, (gk, yk) in enumerate(((v["g0"], v["ya"]), (v["g1"], v["yb"]), (v["g2"], v["yc"]))):
            dg_ref[:, k * D : (k + 1) * D] = dmix * yk * gk * (1.0 - gk)
        dya = dmix * v["g0"]
        dyb = dmix * v["g1"]
        dyc = dmix * v["g2"]
        ds = _dot_t(dya, wco[...])
        ln, sg, nl = v["ln"], v["sg"], v["nl"]
        dln = ds * (sg * (1.0 + ln * (1.0 - sg)))
        dn = dln * lg[...]
        dc = v["rstd"] * (dn - jnp.mean(dn, axis=-1, keepdims=True) - nl * jnp.mean(dn * nl, axis=-1, keepdims=True))
        dybs = dyb * ps[...]
        dc_ref[...] = dc
        dp_ref[...] = _dot_t(dybs, wbd[...])
        da_ref[...] = _dot_t(dyc, wao[...])
        mx_ref[...] = v["mixed"].astype(mx_ref.dtype)
        s_ref[...] = v["s"].astype(s_ref.dtype)
        dya_ref[...] = dya.astype(dya_ref.dtype)
        dyb_ref[...] = dybs.astype(dyb_ref.dtype)
        dyc_ref[...] = dyc.astype(dyc_ref.dtype)

        @pl.when(pl.program_id(0) == 0)
        def _():
            accd_ref[...] = jnp.zeros_like(accd_ref)
            accc_ref[...] = jnp.zeros_like(accc_ref)

        accd_ref[pl.ds(0, 1), :] += jnp.sum(dya, axis=0, keepdims=True)
        accd_ref[pl.ds(1, 1), :] += jnp.sum(dyb * v["ybr"], axis=0, keepdims=True)
        accc_ref[pl.ds(0, 1), :] += jnp.sum(dln * nl, axis=0, keepdims=True)
        accc_ref[pl.ds(1, 1), :] += jnp.sum(dln, axis=0, keepdims=True)
        accc_ref[pl.ds(2, 1), :] += jnp.sum(dc, axis=0, keepdims=True)

    def row(w):
        return pl.BlockSpec((TB, w), lambda i: (i, 0))

    return _pcall(
        body,
        grid=(R // TB,),
        in_specs=[row(D)] + _mix_act_specs(TB) + _MIX_W_SPECS,
        out_specs=[row(D_G), row(CONV_CH), row(POOL_CH), row(ATT_W), row(D), row(CONV_CH), row(D), row(D), row(D),
                   pl.BlockSpec((8, D), lambda i: (0, 0)), pl.BlockSpec((8, CONV_CH), lambda i: (0, 0))],
        out_shape=[
            jax.ShapeDtypeStruct((R, D_G), F32),
            jax.ShapeDtypeStruct((R, CONV_CH), F32),
            jax.ShapeDtypeStruct((R, POOL_CH), F32),
            jax.ShapeDtypeStruct((R, ATT_W), F32),
            jax.ShapeDtypeStruct((R, D), _MXU),
            jax.ShapeDtypeStruct((R, CONV_CH), _MXU),
            jax.ShapeDtypeStruct((R, D), _MXU),
            jax.ShapeDtypeStruct((R, D), _MXU),
            jax.ShapeDtypeStruct((R, D), _MXU),
            jax.ShapeDtypeStruct((8, D), F32),
            jax.ShapeDtypeStruct((8, CONV_CH), F32),
        ],
        compiler_params=_cp(("arbitrary",)),
        name=name,
    )(dh2, gates, c, pooled, att, *mw)


def _attn_bwd(qkv, att, datt, name):
    R = qkv.shape[0]
    nb = R // BLK

    def body(q_ref, k_ref, v_ref, o_ref, do_ref, dq_ref, dk_ref, dv_ref):
        i = pl.program_id(1)

        @pl.when(i == 0)
        def _():
            dk_ref[...] = jnp.zeros_like(dk_ref)
            dv_ref[...] = jnp.zeros_like(dv_ref)

        q = q_ref[...]
        dob = do_ref[...].astype(_MXU)
        dof = dob.astype(F32)
        sms = {CH: _sum_mat(CH), BLK: _sum_mat(BLK)}
        e_all = jnp.sum(dof * o_ref[...], axis=1, keepdims=True)

        def scores(off, w, masked):
            valid = _attn_mask(i, off, w) if masked else None
            z = _dot_t(q, k_ref[pl.ds(off, w), :])
            return off, w, valid, z, _dot_t(dob, v_ref[pl.ds(off, w), :])

        def sums(state, carry):
            off, w, valid, z, da = state
            lk, log_beta = _attn_logs(z, valid)
            return (off, w, valid, log_beta, _dot(lk, sms[w]), jnp.sum(lk, axis=1, keepdims=True), da), carry

        def weights(state, carry):
            off, w, valid, log_beta, r, total, da = state
            dq, s_after, e_done = carry
            ab = _attn_weights(log_beta, r, valid, s_after).astype(_MXU)
            e = ab.astype(F32) * da
            re = jnp.dot(_hi_lo_rows(e), sms[w], preferred_element_type=F32)
            return (off, w, valid, jnp.exp2(log_beta), ab, e, re), (dq, s_after + total, e_done)

        def grads(state, carry):
            off, w, valid, beta, ab, e, re = state
            dq, s_after, e_done = carry
            dz = e - beta * ((e_all - e_done) - _fold_sums(re))
            if valid is not None:
                dz = jnp.where(valid, dz, 0.0)
            dzb = (dz * LN2).astype(_MXU)
            dk_ref[pl.ds(off, w), :] += _tdot(dzb, q)
            dv_ref[pl.ds(off, w), :] += _tdot(ab, dob)
            dq = dq + jnp.dot(dzb, k_ref[pl.ds(off, w), :], preferred_element_type=F32)
            return None, (dq, s_after, e_done + jnp.sum(e, axis=1, keepdims=True))

        zero = jnp.zeros((BLK, 1), F32)
        dq, _, _ = _attn_walk(i, (scores, sums, weights, grads), 1, (jnp.zeros((BLK, HD), F32), zero, zero))
        dq_ref[...] = dq * (Q_SCALE * LOG2E)

    blk = pl.BlockSpec((BLK, HD), lambda h, i: (i, h))
    col = pl.BlockSpec((R, HD), lambda h, i: (0, h))
    return _pcall(
        body,
        grid=(NH, nb),
        in_specs=[
            blk,
            pl.BlockSpec((R, HD), lambda h, i: (0, NH + h)),
            pl.BlockSpec((R, HD), lambda h, i: (0, 2 * NH + h)),
            blk,
            blk,
        ],
        out_specs=[blk, col, col],
        out_shape=[jax.ShapeDtypeStruct((R, ATT_W), F32)] * 3,
        compiler_params=pltpu.CompilerParams(dimension_semantics=("arbitrary", "arbitrary"),
                                             vmem_limit_bytes=56 * 1024 * 1024),
        name=name,
    )(qkv, qkv, qkv, att, datt)


def _branch_bwd(dc, dpooled, proj_a, conv_w, name):
    R = dc.shape[0]
    nh = TR // HALO
    last = R // HALO - 1
    ni = R // TR

    def body(dc_ref, dcn_ref, dp_ref, dpn_ref, t_ref, h_ref, w_ref, o_ref, dcw_ref, gext, dcext, eext):
        i = pl.program_id(0)
        is_last = i == ni - 1
        t = t_ref[...]
        hl = jnp.where(i == 0, 0.0, h_ref[...])
        a = t[:, :CONV_CH]
        sg = _sig(t[:, CONV_CH : 2 * CONV_CH])
        gext[pl.ds(0, HALO), :] = hl[:, :CONV_CH] * _sig(hl[:, CONV_CH : 2 * CONV_CH])
        gext[pl.ds(HALO, TR), :] = a * sg
        dct = dc_ref[...]
        dcext[pl.ds(0, TR), :] = dct
        dcext[pl.ds(TR, HALO), :] = jnp.where(is_last, 0.0, dcn_ref[...])

        @pl.when(i == 0)
        def _():
            dcw_ref[...] = jnp.zeros_like(dcw_ref)

        dglu = jnp.zeros((TR, CONV_CH), F32)
        for k in range(CONV_K):
            dglu = dglu + w_ref[pl.ds(k, 1), :] * dcext[pl.ds(CONV_K - 1 - k, TR), :]
            dcw_ref[pl.ds(k, 1), :] += jnp.sum(gext[pl.ds(HALO - (CONV_K - 1) + k, TR), :] * dct, axis=0, keepdims=True)
        o_ref[:, :CONV_CH] = dglu * sg
        o_ref[:, CONV_CH : 2 * CONV_CH] = dglu * a * sg * (1.0 - sg)

        lane, wsize = _pool_consts(i)
        dpt = dp_ref[...]
        eext[pl.ds(0, TR), :] = dpt / _pool_div(_row_ids(TR, i, (TR, POOL_CH)), wsize)
        nxt = dpn_ref[...] / _pool_div(_row_ids(TR, i + 1, (HALO, POOL_CH)), wsize)
        eext[pl.ds(TR, HALO), :] = jnp.where(is_last, 0.0, nxt)

        def fwd(k):
            return eext[pl.ds(k, TR), :]

        s2 = fwd(0) + fwd(1)
        s4 = s2 + fwd(2) + fwd(3)
        s8 = s4 + fwd(4) + fwd(5) + fwd(6) + fwd(7)
        s16 = s8
        for k in range(8, 16):
            s16 = s16 + fwd(k)
        o_ref[:, 2 * CONV_CH :] = _lane_select(lane, s2, s4, s8, s16) - dpt

    def nxt_spec(w):
        return pl.BlockSpec((HALO, w), lambda i: (jnp.minimum((i + 1) * nh, last), 0))

    return _pcall(
        body,
        grid=(ni,),
        in_specs=[
            pl.BlockSpec((TR, CONV_CH), lambda i: (i, 0)),
            nxt_spec(CONV_CH),
            pl.BlockSpec((TR, POOL_CH), lambda i: (i, 0)),
            nxt_spec(POOL_CH),
            pl.BlockSpec((TR, TC), lambda i: (i, 0)),
            pl.BlockSpec((HALO, TC), lambda i: (jnp.maximum(i * nh - 1, 0), 0)),
            pl.BlockSpec((HALO, CONV_CH), lambda i: (0, 0)),
        ],
        out_specs=[pl.BlockSpec((TR, TC), lambda i: (i, 0)), pl.BlockSpec((HALO, CONV_CH), lambda i: (0, 0))],
        out_shape=[jax.ShapeDtypeStruct((R, TC), F32), jax.ShapeDtypeStruct((HALO, CONV_CH), F32)],
        scratch_shapes=[
            pltpu.VMEM((TR + HALO, CONV_CH), F32),
            pltpu.VMEM((TR + HALO, CONV_CH), F32),
            pltpu.VMEM((TR + HALO, POOL_CH), F32),
        ],
        compiler_params=_cp(("arbitrary",)),
        name=name,
    )(dc, dc, dpooled, dpooled, proj_a, proj_a, conv_w)


def _in_bwd(dfront, dq, dk, dv, dgates, w_a, w_g, h, dh2, gain, name, side=None):
    R = h.shape[0]
    nj = 1 + D_G // TC
    ni = R // TR
    w_a = w_a.astype(_MXU)
    w_f, w_q = w_a[:, :TC], w_a[:, TC : TC + ATT_W]
    w_k, w_v = w_a[:, TC + ATT_W : TC + 2 * ATT_W], w_a[:, TC + 2 * ATT_W :]

    def body(df_ref, dq_ref, dk_ref, dv_ref, dg_ref, wf_ref, wq_ref, wk_ref, wv_ref, wg_ref, h_ref, dh_ref, g_ref,
             o_ref, hn_ref, dgain_ref, acc):
        i, j = pl.program_id(0), pl.program_id(1)

        @pl.when(j == 0)
        def _():
            acc[...] = (_dot_t(df_ref[...], wf_ref[...]) + _dot_t(dq_ref[...], wq_ref[...])
                        + _dot_t(dk_ref[...], wk_ref[...]) + _dot_t(dv_ref[...], wv_ref[...]))

        @pl.when(j > 0)
        def _():
            acc[...] += _dot_t(dg_ref[...], wg_ref[...])

        @pl.when(jnp.logical_and(i == 0, j == 0))
        def _():
            dgain_ref[...] = jnp.zeros_like(dgain_ref)

        @pl.when(j == nj - 1)
        def _():
            g = g_ref[...]
            r, n, y = _rms_fwd(h_ref[...], g)
            dx, dgain = _rms_bwd(acc[...], r, n, g)
            dgain_ref[...] += dgain
            rows = _row_ids(TR, i, (TR, D))
            o_ref[...] = jnp.where(rows >= PAD_ROWS, dh_ref[...] + dx, 0.0)
            hn_ref[...] = y.astype(hn_ref.dtype)

    def row(w):
        return pl.BlockSpec((TR, w), lambda i, j: (i, 0))

    def whole(w):
        return pl.BlockSpec((D, w), lambda i, j: (0, 0))

    def gcol(i, j):
        return jnp.maximum(j - 1, 0)

    return _hosted_call(
        body, side,
        grid=(ni, nj),
        in_specs=[
            row(TC), row(ATT_W), row(ATT_W), row(ATT_W),
            pl.BlockSpec((TR, TC), lambda i, j: (i, gcol(i, j))),
            whole(TC), whole(ATT_W), whole(ATT_W), whole(ATT_W),
            pl.BlockSpec((D, TC), lambda i, j: (0, gcol(i, j))),
            row(D), row(D),
            pl.BlockSpec((1, D), lambda i, j: (0, 0)),
        ],
        out_specs=[row(D), row(D), pl.BlockSpec((1, D), lambda i, j: (0, 0))],
        out_shape=[jax.ShapeDtypeStruct((R, D), F32), jax.ShapeDtypeStruct((R, D), _MXU), jax.ShapeDtypeStruct((1, D), F32)],
        scratch_shapes=[pltpu.VMEM((TR, D), F32)],
        name=name,
        operands=(dfront, dq, dk, dv, dgates, w_f, w_q, w_k, w_v, w_g.astype(_MXU), h, dh2, gain),
    )


def _pool_blockdiag(w_grp):
    eye = jnp.eye(len(POOL_WINDOWS), dtype=w_grp.dtype)
    return jnp.einsum("gcd,gh->gchd", w_grp, eye).reshape(POOL_CH, D)


def _pool_blockdiag_grad(dw_bd):
    d4 = dw_bd.reshape(len(POOL_WINDOWS), POOL_GC, len(POOL_WINDOWS), D // len(POOL_WINDOWS))
    return jnp.stack([d4[g, :, g, :] for g in range(len(POOL_WINDOWS))])


def _local_step(x, target, p, riders=None):
    row = lambda a: a.reshape(1, -1)
    riders = riders or {}
    g = {k: [None] * N_LAYERS for k in ("norm1", "w_in", "conv_dw_w", "conv_dw_b", "conv_ln_g", "conv_ln_b", "w_conv_out",
                                        "b_conv_out", "w_pool_grp", "pool_scale", "w_attn_out", "w_o", "norm2", "w_up",
                                        "ffn_dw_w", "ffn_dw_b", "w_down")}

    def with_rider(name, call):
        make, land = riders.get(name, (None, None))
        side = make(g) if make else None
        res = call(side)
        res = list(res) if isinstance(res, (list, tuple)) else [res]
        if side is None:
            return res
        n_own = len(res) - len(side.out_shape)
        land(res[n_own:])
        return res[:n_own]

    h = jnp.concatenate([jnp.zeros((PAD_ROWS, D), F32), p["meta"], x], axis=0)
    saved = []
    for l in range(N_LAYERS):
        w_a, w_g = p["w_in"][l][:, :D_A], p["w_in"][l][:, D_A:]
        conv_w = jnp.concatenate([p["conv_dw_w"][l], jnp.zeros((1, CONV_CH), F32)], axis=0)
        cw3 = p["ffn_dw_w"][l].reshape(FFN_K, 2, D_FF)
        cb2 = p["ffn_dw_b"][l].reshape(2, D_FF)
        mw = (row(p["conv_ln_g"][l]), row(p["conv_ln_b"][l]), p["w_conv_out"][l].astype(_MXU), row(p["b_conv_out"][l]),
              _pool_blockdiag(p["w_pool_grp"][l]).astype(_MXU), row(p["pool_scale"][l]),
              p["w_attn_out"][l].astype(_MXU), p["w_o"][l].astype(_MXU))
        proj_a, qkv = _rms_matmul(h, row(p["norm1"][l]), w_a, f"proj_a{l}", qkv_out=True)
        (gates,) = with_rider(f"proj_g{l}", lambda side: _rms_matmul(h, row(p["norm1"][l]), w_g, f"proj_g{l}", side=side))
        c, pooled = _branch_pre(proj_a, conv_w, row(p["conv_dw_b"][l]), f"branch_pre{l}")
        att = _attn_fwd(qkv, f"attn_fwd{l}")
        h2 = _mix_fwd(h, gates, c, pooled, att, mw, f"mix_fwd{l}")
        (u3,) = with_rider(f"ffn_up{l}", lambda side: _rms_matmul(h2, row(p["norm2"][l]), p["w_up"][l], f"ffn_up{l}",
                                                                split_out=True, side=side))
        h3 = _ffn_fwd(u3, h2, cw3, cb2, p["w_down"][l], f"ffn_fwd{l}")
        saved.append(dict(h=h, w_a=w_a, w_g=w_g, conv_w=conv_w, cw3=cw3, cb2=cb2, mw=mw, proj_a=proj_a, gates=gates,
                          qkv=qkv, c=c, pooled=pooled, att=att, h2=h2, u3=u3))
        h = h3

    dh, loss, d_final = _loss_bwd(h, target, row(p["final_norm"]), "loss_bwd")

    for l in reversed(range(N_LAYERS)):
        s = saved[l]
        dc3, g["w_down"][l], dwb = with_rider(f"ffn_bwd1_{l}", lambda side: _ffn_bwd1(
            dh, s["u3"], s["cw3"], s["cb2"], p["w_down"][l], f"ffn_bwd1_{l}", side=side))
        g["ffn_dw_w"][l] = jnp.transpose(dwb[:, :FFN_K, :], (1, 0, 2)).reshape(FFN_K, 2 * D_FF)
        g["ffn_dw_b"][l] = dwb[:, FFN_K, :].reshape(2 * D_FF)
        du3, dh2, xn2, dg2 = _ffn_bwd2(dc3, s["cw3"], p["w_up"][l], s["h2"], dh, row(p["norm2"][l]), f"ffn_bwd2_{l}")
        g["norm2"][l] = dg2[0]
        g["w_up"][l] = _tdot_acc(xn2, du3, f"dw_up{l}")
        (dgates, dc, dpooled, datt, mixed, s_act, dya, dybs, dyc, acc_d, acc_c) = _mix_bwd(
            dh2, s["gates"], s["c"], s["pooled"], s["att"], s["mw"], f"mix_bwd{l}")
        g["w_o"][l] = _tdot_acc(mixed, dh2, f"dw_o{l}")
        g["w_conv_out"][l] = _tdot_acc(s_act, dya, f"dw_conv_out{l}")
        g["w_pool_grp"][l] = _pool_blockdiag_grad(_tdot_acc(s["pooled"], dybs, f"dw_pool{l}"))
        g["w_attn_out"][l] = _tdot_acc(s["att"], dyc, f"dw_attn_out{l}")
        g["b_conv_out"][l] = acc_d[0]
        g["pool_scale"][l] = acc_d[1]
        g["conv_ln_g"][l] = acc_c[0]
        g["conv_ln_b"][l] = acc_c[1]
        g["conv_dw_b"][l] = acc_c[2]
        dq, dk, dv = _attn_bwd(s["qkv"], s["att"], datt, f"attn_bwd{l}")
        dfront, dcw = _branch_bwd(dc, dpooled, s["proj_a"], s["conv_w"], f"branch_bwd{l}")
        g["conv_dw_w"][l] = dcw[:CONV_K]
        dh, hn, dg1 = with_rider(f"in_bwd{l}", lambda side: _in_bwd(
            dfront, dq, dk, dv, dgates, s["w_a"], s["w_g"], s["h"], dh2, row(p["norm1"][l]), f"in_bwd{l}", side=side))
        g["norm1"][l] = dg1[0]
        g["w_in"][l] = jnp.concatenate(
            [_tdot_acc(hn, part, f"dw_in{l}_{k}") for k, part in enumerate((dfront, dq, dk, dv, dgates))], axis=1)

    grads = dict(g, final_norm=d_final[0], meta=dh[PAD_ROWS:BLK])
    return loss, dh[BLK:], grads


def _axes():
    return lax.axis_index("x"), lax.axis_index("y"), lax.axis_index("c")


def _transfer(side, name):
    def body(*refs):
        s_in, s_out = len(side.arrays), len(side.out_shape)
        ins, outs, scratch = refs[:s_in], refs[s_in : s_in + s_out], refs[s_in + s_out :]
        side.start(ins, outs, scratch)
        side.finish(ins, outs, scratch)

    return _pcall(
        body,
        in_specs=[ANY] * len(side.arrays),
        out_specs=[ANY] * len(side.out_shape),
        out_shape=list(side.out_shape),
        scratch_shapes=list(side.scratch),
        compiler_params=pltpu.CompilerParams(has_side_effects=True),
        name=name,
    )(*side.arrays)


def _gather_side(blobs):
    nt = len(blobs)
    flips = ((1, 0), (0, 1), (1, 1))

    def local(srcs, dsts, sems):
        x, y, _ = _axes()
        return [pltpu.make_async_copy(srcs[t], dsts[t].at[2 * x + y], sems[2].at[t]) for t in range(nt)]

    def remote(srcs, dsts, sems, arriving):
        send_sems, recv_sems, _ = sems
        x, y, c = _axes()
        out = []
        for t in range(nt):
            for f, (fx, fy) in enumerate(flips):
                px = 1 - x if fx else x
                py = 1 - y if fy else y
                out.append(pltpu.make_async_remote_copy(
                    srcs[t], dsts[t].at[2 * px + py if arriving else 2 * x + y], send_sems.at[t, f], recv_sems.at[t, f],
                    device_id=(px, py, c), device_id_type=MESH))
        return out

    def start(srcs, dsts, sems):
        for cp in local(srcs, dsts, sems) + remote(srcs, dsts, sems, arriving=False):
            cp.start()

    def finish(srcs, dsts, sems):
        for cp in remote(srcs, dsts, sems, arriving=False):
            cp.wait_send()
        for cp in remote(srcs, dsts, sems, arriving=True):
            cp.wait_recv()
        for cp in local(srcs, dsts, sems):
            cp.wait()

    return _Side(
        arrays=tuple(blobs),
        out_shape=tuple(jax.ShapeDtypeStruct((4,) + b.shape, b.dtype) for b in blobs),
        scratch=(pltpu.SemaphoreType.DMA((nt, 3)), pltpu.SemaphoreType.DMA((nt, 3)), pltpu.SemaphoreType.DMA((nt,))),
        start=start,
        finish=finish,
    )


def _exchange_side(contribs):
    nt = len(contribs)

    def plan(srcs, dsts, sems):
        send_sems, recv_sems, local_sems = sems
        x, y, c = _axes()
        sibling = (x, y, 1 - c)
        chips = [(1 - x, y), (x, 1 - y), (1 - x, 1 - y)]

        def slot(t, px, py, pc):
            return dsts[t].at[4 * px + 2 * py + pc]

        def copy(t, k, source, target_slot, to):
            return pltpu.make_async_remote_copy(source, target_slot, send_sems.at[t, k], recv_sems.at[t, k],
                                                device_id=to, device_id_type=MESH)

        def mine(t):
            return srcs[t].at[2 * x + y]

        def own():
            return [pltpu.make_async_copy(mine(t), slot(t, x, y, c), local_sems.at[t]) for t in range(nt)]

        def first():
            out = []
            for t in range(nt):
                out.append(copy(t, 0, mine(t), slot(t, x, y, c), sibling))
                out += [copy(t, 1 + j, srcs[t].at[2 * px + py], slot(t, x, y, c), (px, py, c))
                        for j, (px, py) in enumerate(chips)]
            return out

        def relay():
            return [(copy(t, 1 + j, mine(t), slot(t, px, py, c), (px, py, c)),
                     copy(t, 4 + j, slot(t, px, py, c), slot(t, px, py, c), sibling))
                    for j, (px, py) in enumerate(chips) for t in range(nt)]

        def from_sibling():
            out = [copy(t, 0, mine(t), slot(t, x, y, 1 - c), sibling) for t in range(nt)]
            return out + [copy(t, 4 + j, mine(t), slot(t, px, py, 1 - c), sibling)
                          for t in range(nt) for j, (px, py) in enumerate(chips)]

        return own, first, relay, from_sibling

    def start(srcs, dsts, sems):
        own, first, _, _ = plan(srcs, dsts, sems)
        for cp in own() + first():
            cp.start()

    def finish(srcs, dsts, sems):
        own, first, relay, from_sibling = plan(srcs, dsts, sems)
        passed = []
        for arriving, onward in relay():
            arriving.wait_recv()
            onward.start()
            passed.append(onward)
        for cp in from_sibling():
            cp.wait_recv()
        for cp in first() + passed:
            cp.wait_send()
        for cp in own():
            cp.wait()

    return _Side(
        arrays=tuple(contribs),
        out_shape=tuple(jax.ShapeDtypeStruct((8,) + a.shape[1:], a.dtype) for a in contribs),
        scratch=(pltpu.SemaphoreType.DMA((nt, 7)), pltpu.SemaphoreType.DMA((nt, 7)), pltpu.SemaphoreType.DMA((nt,))),
        start=start,
        finish=finish,
    )


ADAM_BLOCK_BYTES = 768 * 1024


def _adamw(w, m, v, slot_list, name):
    shape = w.shape
    parts = len(slot_list)
    cols = shape[-1]
    rows = _size(shape) // cols // parts
    blk = max(r for r in range(8, rows + 1, 8) if rows % r == 0 and r * cols * 4 <= ADAM_BLOCK_BYTES) \
        if rows % 8 == 0 else rows

    def body(w_ref, m_ref, v_ref, *refs):
        slot_refs, (g_ref, d_ref, nm_ref, nv_ref) = refs[:parts], refs[parts:]
        g = None
        for k, gs_ref in enumerate(slot_refs):
            gk = gs_ref[0]
            for dev in range(1, 8):
                gk = gk + gs_ref[dev]
            g = gk if g is None else jnp.where(pl.program_id(0) == k, gk, g)
        w_, m_, v_ = w_ref[...], m_ref[...], v_ref[...]
        m_new = ADAM_B1 * m_ + (1.0 - ADAM_B1) * g
        v_new = ADAM_B2 * v_ + (1.0 - ADAM_B2) * (g * g)
        m_hat = m_new / (1.0 - ADAM_B1 ** ADAM_STEP)
        v_hat = v_new / (1.0 - ADAM_B2 ** ADAM_STEP)
        g_ref[...] = g
        d_ref[...] = -ADAM_LR * (m_hat / (jnp.sqrt(v_hat) + ADAM_EPS) + ADAM_WD * w_)
        nm_ref[...] = m_new
        nv_ref[...] = v_new

    spec = pl.BlockSpec((None, blk, cols), lambda p, i: (p, i, 0))
    slot_specs = [pl.BlockSpec((8, blk, cols), lambda p, i, k=k: (0, jnp.where(p == k, i, 0), 0)) for k in range(parts)]
    flat = (parts, rows, cols)
    outs = _pcall(
        body,
        grid=(parts, rows // blk),
        in_specs=[spec, spec, spec] + slot_specs,
        out_specs=[spec] * 4,
        out_shape=[jax.ShapeDtypeStruct(flat, F32)] * 4,
        compiler_params=_cp(("arbitrary", "arbitrary")),
        name=name,
    )(w.reshape(flat), m.reshape(flat), v.reshape(flat), *[s.reshape(8, rows, cols) for s in slot_list])
    return [o.reshape(shape) for o in outs]


_PARAMS = (
    ("meta", (N_META, D), 1),
    ("norm1", (N_LAYERS, D), None),
    ("w_in", (N_LAYERS, D, D_A + D_G), 2),
    ("conv_dw_w", (N_LAYERS, CONV_K, CONV_CH), 2),
    ("conv_dw_b", (N_LAYERS, CONV_CH), None),
    ("conv_ln_g", (N_LAYERS, CONV_CH), None),
    ("conv_ln_b", (N_LAYERS, CONV_CH), None),
    ("w_conv_out", (N_LAYERS, CONV_CH, D), 2),
    ("b_conv_out", (N_LAYERS, D), None),
    ("w_pool_grp", (N_LAYERS, len(POOL_WINDOWS), POOL_GC, D // len(POOL_WINDOWS)), 3),
    ("pool_scale", (N_LAYERS, D), None),
    ("w_attn_out", (N_LAYERS, ATT_W, D), 2),
    ("w_o", (N_LAYERS, D, D), 1),
    ("norm2", (N_LAYERS, D), None),
    ("w_up", (N_LAYERS, D, 2 * D_FF), 2),
    ("ffn_dw_w", (N_LAYERS, FFN_K, 2 * D_FF), 2),
    ("ffn_dw_b", (N_LAYERS, 2 * D_FF), None),
    ("w_down", (N_LAYERS, D_FF, D), 1),
    ("final_norm", (D,), None),
)
_MIXER = ("w_in", "w_conv_out", "w_pool_grp", "w_attn_out", "w_o")
_FFN = ("w_up", "w_down")
_BIG = _MIXER + _FFN
_SMALL_SHARDED = ("meta", "conv_dw_w", "ffn_dw_w")
_SHARD_AXIS = {n: ax for n, _, ax in _PARAMS}


def _size(shape):
    n = 1
    for d in shape:
        n *= d
    return n


def _pack(parts, lanes, row_multiple):
    flat = jnp.concatenate([a.reshape(-1) for a in parts])
    rows = -(-flat.shape[0] // lanes)
    rows = -(-rows // row_multiple) * row_multiple
    flat = jnp.pad(flat, (0, rows * lanes - flat.shape[0]))
    return flat.reshape(rows, lanes)


def _unpack(blob, shapes):
    flat = blob.reshape(-1)
    out, off = [], 0
    for s in shapes:
        n = _size(s)
        out.append(flat[off : off + n].reshape(s))
        off += n
    return out


def _shard(a, ax, s):
    n = a.shape[ax] // 4
    return lax.slice_in_dim(a, s * n, (s + 1) * n, axis=ax)


def kernel(x, meta, norm1, w_in, conv_dw_w, conv_dw_b, conv_ln_g, conv_ln_b, w_conv_out, b_conv_out, w_pool_grp, pool_scale, w_attn_out, w_o, norm2, w_up, ffn_dw_w, ffn_dw_b, w_down, final_norm, loss_target, m_meta, m_norm1, m_w_in, m_conv_dw_w, m_conv_dw_b, m_conv_ln_g, m_conv_ln_b, m_w_conv_out, m_b_conv_out, m_w_pool_grp, m_pool_scale, m_w_attn_out, m_w_o, m_norm2, m_w_up, m_ffn_dw_w, m_ffn_dw_b, m_w_down, m_final_norm, v_meta, v_norm1, v_w_in, v_conv_dw_w, v_conv_dw_b, v_conv_ln_g, v_conv_ln_b, v_w_conv_out, v_b_conv_out, v_w_pool_grp, v_pool_scale, v_w_attn_out, v_w_o, v_norm2, v_w_up, v_ffn_dw_w, v_ffn_dw_b, v_w_down, v_final_norm):
    names = [n for n, _, _ in _PARAMS]
    w_loc = dict(zip(names, (meta, norm1, w_in, conv_dw_w, conv_dw_b, conv_ln_g, conv_ln_b, w_conv_out, b_conv_out, w_pool_grp, pool_scale, w_attn_out, w_o, norm2, w_up, ffn_dw_w, ffn_dw_b, w_down, final_norm)))
    m_loc = dict(zip(names, (m_meta, m_norm1, m_w_in, m_conv_dw_w, m_conv_dw_b, m_conv_ln_g, m_conv_ln_b, m_w_conv_out, m_b_conv_out, m_w_pool_grp, m_pool_scale, m_w_attn_out, m_w_o, m_norm2, m_w_up, m_ffn_dw_w, m_ffn_dw_b, m_w_down, m_final_norm)))
    v_loc = dict(zip(names, (v_meta, v_norm1, v_w_in, v_conv_dw_w, v_conv_dw_b, v_conv_ln_g, v_conv_ln_b, v_w_conv_out, v_b_conv_out, v_w_pool_grp, v_pool_scale, v_w_attn_out, v_w_o, v_norm2, v_w_up, v_ffn_dw_w, v_ffn_dw_b, v_w_down, v_final_norm)))

    full = {n: w_loc[n] for n, _, ax in _PARAMS if ax is None}
    full.update({n: [None, None] for n in _BIG})

    def gather(group, l, extra=()):
        return _gather_side([w_loc[n][l].astype(_MXU) for n in group] + list(extra))

    def install(group, l, gathered):
        for n, g4 in zip(group, gathered):
            full[n][l] = jnp.concatenate([g4[s] for s in range(4)], axis=_SHARD_AXIS[n] - 1)

    small = _pack([w_loc[n] for n in _SMALL_SHARDED], BLK, 8)
    *first, small4 = _transfer(gather(_MIXER, 0, [small]), "allgather_weights")
    install(_MIXER, 0, first)
    per_chip = [_unpack(small4[s], [w_loc[n].shape for n in _SMALL_SHARDED]) for s in range(4)]
    for k, n in enumerate(_SMALL_SHARDED):
        full[n] = jnp.concatenate([per_chip[s][k] for s in range(4)], axis=_SHARD_AXIS[n])

    slots = {}

    def exchange(group, l):
        def make(g):
            return _exchange_side([jnp.stack([_shard(g[n][l], _SHARD_AXIS[n] - 1, s) for s in range(4)]) for n in group])

        return make, lambda landed: slots.update({(n, l): a for n, a in zip(group, landed)})

    riders = {
        "proj_g0": (lambda g: gather(_FFN, 0), functools.partial(install, _FFN, 0)),
        "ffn_up0": (lambda g: gather(_BIG, 1), functools.partial(install, _BIG, 1)),
        "ffn_bwd1_0": exchange(_BIG, 1),
        "in_bwd0": exchange(_FFN, 0),
    }
    loss, grad_x, grads = _local_step(x[0], loss_target[0], full, riders)

    small_names = [n for n in names if n not in _BIG]
    small_grads = {n: jnp.stack(grads[n]) if isinstance(grads[n], list) else grads[n] for n in small_names}

    def small_blob(src, s=None):
        return _pack([src[n] if s is None or _SHARD_AXIS[n] is None else _shard(src[n], _SHARD_AXIS[n], s)
                      for n in small_names], BLOB_LANES, 8)

    make, land = exchange(_MIXER, 0)
    last = make(grads)
    last = _exchange_side(list(last.arrays) + [jnp.stack([small_blob(small_grads, s) for s in range(4)])])
    *late, small_slots = _transfer(last, "grad_exchange")
    land(late)
    results = {n: _adamw(w_loc[n], m_loc[n], v_loc[n], [slots[n, 0], slots[n, 1]], f"adamw_{n}") for n in _BIG}
    small_out = _adamw(small_blob(w_loc), small_blob(m_loc), small_blob(v_loc), [small_slots], "adamw_small")
    small_shapes = [w_loc[n].shape for n in small_names]
    for k, blob in enumerate(small_out):
        for n, a in zip(small_names, _unpack(blob, small_shapes)):
            results.setdefault(n, [None] * 4)[k] = a
    loss = lax.psum(loss[0, 0], ("x", "y", "c"))
    outs = [loss, grad_x[None]]
    for k in range(4):
        outs.extend(results[n][k] for n in names)
    return tuple(outs)
```
